```python
import math
import jax, jax.numpy as jnp
from jax import lax
import numpy as np

D_MODEL = 2048
BATCH = 2
SEQ = 4096
DEPTH = 2

N_GROUPS = 4
GROUP_W = D_MODEL // N_GROUPS
D_MIX = N_GROUPS * GROUP_W
N_IN_BLOCKS = 15
CONV_W = 3
HG_HEADS = 4
HG_DK = GROUP_W // HG_HEADS
HG_CHUNK = 64
F_FLOOR = 1e-30
DA_HEADS = 4
DA_DV = GROUP_W // DA_HEADS
DA_DQK = DA_DV // 2
Q_BLOCK = 128
MASK_VALUE = -1e30
SG_CHUNK = 128
SG_GROUPS = 4
SG_CH = GROUP_W // SG_GROUPS
PLE_DIM = 256
ALPHA = (2 * DEPTH) ** 0.25
BETA = (8 * DEPTH) ** -0.25
LN_EPS = 1e-5
RMS_EPS = 1e-6

kernel_name = "hybrid_parallel_groups_conv_hgrn2_diffattn_sgu"


def _layer_norm(x, g, b):
    xf = x.astype(jnp.float32)
    mu = jnp.mean(xf, axis=-1, keepdims=True)
    var = jnp.mean(jnp.square(xf - mu), axis=-1, keepdims=True)
    y = (xf - mu) * lax.rsqrt(var + LN_EPS) * g.astype(jnp.float32) + b.astype(jnp.float32)
    return y.astype(x.dtype)


def _rms_norm(x, g):
    xf = x.astype(jnp.float32)
    y = xf * lax.rsqrt(jnp.mean(jnp.square(xf), axis=-1, keepdims=True) + RMS_EPS) * g.astype(jnp.float32)
    return y.astype(x.dtype)


def short_conv_mixer(b, c, xv, w):
    s = xv.shape[1]
    z = c * xv
    zp = jnp.pad(z, ((0, 0), (CONV_W - 1, 0), (0, 0)))
    y = sum(w[j] * zp[:, j:j + s] for j in range(CONV_W))
    return b * y


def hgrn2_mixer(q, fz, iv, lb, norm_g):
    bsz, s, _ = q.shape
    n = s // HG_CHUNK
    fz32 = fz.astype(jnp.float32)
    lb = lb.astype(jnp.float32)
    f = lb + (1.0 - lb) * jax.nn.sigmoid(fz32)
    log_f = jnp.log(jnp.maximum(f, F_FLOOR))
    k = (1.0 - lb) * jax.nn.sigmoid(-fz32)

    def chunks(t):
        return t.astype(jnp.float32).reshape(bsz, n, HG_CHUNK, HG_HEADS, HG_DK).transpose(1, 0, 3, 2, 4)

    qc, kc, vc, gc = chunks(q), chunks(k), chunks(iv), chunks(log_f)
    bc = jnp.cumsum(gc, axis=3)
    causal = jnp.tril(jnp.ones((HG_CHUNK, HG_CHUNK), dtype=bool))[:, :, None]

    def step(state, inp):
        qt, kt, vt, bt = inp
        diff = bt[:, :, :, None, :] - bt[:, :, None, :, :]
        decay = jnp.where(causal, jnp.exp(jnp.where(causal, diff, 0.0)), 0.0)
        a = jnp.einsum('bhtk,bhsk,bhtsk->bhts', qt, kt, decay)
        o = jnp.einsum('bhts,bhsv->bhtv', a, vt) + jnp.einsum('bhtk,bhkv->bhtv', qt * jnp.exp(bt), state)
        b_last = bt[:, :, -1, :]
        new_state = jnp.exp(b_last)[..., None] * state + jnp.einsum(
            'bhsk,bhsv->bhkv', kt * jnp.exp(b_last[:, :, None, :] - bt), vt)
        return new_state, o

    state0 = jnp.zeros((bsz, HG_HEADS, HG_DK, HG_DK), jnp.float32)
    _, o = lax.scan(step, state0, (qc, kc, vc, bc))
    o = o.transpose(1, 0, 3, 2, 4).reshape(bsz, s, HG_HEADS, HG_DK)
    o = _rms_norm(o, norm_g.reshape(HG_HEADS, HG_DK))
    return o.reshape(bsz, s, GROUP_W).astype(q.dtype)


def diff_attn_mixer(q, k, v, lam, lam_init, norm_g):
    bsz, s, _ = q.shape
    nb = s // Q_BLOCK
    q = q.reshape(bsz, s, DA_HEADS, 2, DA_DQK)
    k = k.reshape(bsz, s, DA_HEADS, 2, DA_DQK)
    q1, q2 = q[..., 0, :].transpose(0, 2, 1, 3), q[..., 1, :].transpose(0, 2, 1, 3)
    k1, k2 = k[..., 0, :].transpose(0, 2, 1, 3), k[..., 1, :].transpose(0, 2, 1, 3)
    vh = v.reshape(bsz, s, DA_HEADS, DA_DV).transpose(0, 2, 1, 3)

    def to_blocks(t):
        return t.reshape(bsz, DA_HEADS, nb, Q_BLOCK, DA_DQK).transpose(2, 0, 1, 3, 4)

    qpos = jnp.arange(s).reshape(nb, Q_BLOCK)
    kpos = jnp.arange(s)
    scale = DA_DQK ** -0.5

    def one_block(args):
        q1b, q2b, pos = args
        mask = kpos[None, :] <= pos[:, None]

        def probs(qb, kk):
            sc = jnp.einsum('bhqd,bhkd->bhqk', qb, kk).astype(jnp.float32) * scale
            return jax.nn.softmax(jnp.where(mask, sc, MASK_VALUE), axis=-1)

        a = probs(q1b, k1) - lam * probs(q2b, k2)
        return jnp.einsum('bhqk,bhkv->bhqv', a.astype(vh.dtype), vh)

    o = lax.map(one_block, (to_blocks(q1), to_blocks(q2), qpos))
    o = o.transpose(1, 0, 3, 2, 4).reshape(bsz, s, DA_HEADS, DA_DV)
    o = _rms_norm(o, norm_g.reshape(DA_HEADS, DA_DV)) * (1.0 - lam_init)
    return o.reshape(bsz, s, GROUP_W).astype(q.dtype)


def spatial_gate_mixer(u, v, ln_g, ln_b, ws, bs):
    bsz, s, _ = u.shape
    n = s // SG_CHUNK
    vn = _layer_norm(v, ln_g, ln_b).reshape(bsz, n, SG_CHUNK, SG_GROUPS, SG_CH)
    w = ws * jnp.tril(jnp.ones((SG_CHUNK, SG_CHUNK), ws.dtype))
    sv = jnp.einsum('gts,bnsgc->bntgc', w, vn) + bs.T[:, :, None]
    return u * sv.reshape(bsz, s, GROUP_W)


def setup_inputs(seed: int = 0) -> dict:
    key = jax.random.key(seed)
    ks = jax.random.split(key, 20)
    f32 = jnp.float32
    x = jax.random.normal(ks[0], (BATCH, SEQ, D_MODEL), f32)
    p = jax.random.normal(ks[1], (DEPTH, BATCH, SEQ, PLE_DIM), f32)
    value_blocks = np.array([1, 1, BETA, 1, 1, BETA, 1, 1, BETA, BETA, 1, 1, 1, 1, 1], np.float32)
    col_scale = jnp.asarray(np.repeat(value_blocks, GROUP_W))
    w_in = jax.random.normal(ks[2], (DEPTH, D_MODEL, N_IN_BLOCKS * GROUP_W), f32) * (D_MODEL ** -0.5) * col_scale
    conv_w = jax.random.normal(ks[3], (DEPTH, CONV_W, GROUP_W), f32) * (CONV_W ** -0.5)
    hgrn_lb = jax.random.normal(ks[4], (DEPTH, GROUP_W), f32) * 0.5
    hgrn_norm_g = 1.0 + 0.02 * jax.random.normal(ks[5], (DEPTH, GROUP_W), f32)
    diff_lambda = 0.1 * jax.random.normal(ks[6], (DEPTH, 4, DA_DQK), f32)
    diff_norm_g = 1.0 + 0.02 * jax.random.normal(ks[7], (DEPTH, GROUP_W), f32)
    sg_ln_g = 1.0 + 0.02 * jax.random.normal(ks[8], (DEPTH, GROUP_W), f32)
    sg_ln_b = 0.02 * jax.random.normal(ks[9], (DEPTH, GROUP_W), f32)
    sg_w = jax.random.normal(ks[10], (DEPTH, SG_GROUPS, SG_CHUNK, SG_CHUNK), f32) * (SG_CHUNK ** -0.5)
    sg_b = 1.0 + 0.02 * jax.random.normal(ks[11], (DEPTH, SG_GROUPS, SG_CHUNK), f32)
    w_out = jax.random.normal(ks[12], (DEPTH, D_MIX, D_MODEL), f32) * (D_MIX ** -0.5) * BETA
    ln_g = 1.0 + 0.02 * jax.random.normal(ks[13], (DEPTH, D_MODEL), f32)
    ln_b = 0.02 * jax.random.normal(ks[14], (DEPTH, D_MODEL), f32)
    w_pe = jax.random.normal(ks[15], (DEPTH, PLE_DIM, D_MODEL), f32) * (PLE_DIM ** -0.5)
    w_pg = jax.random.normal(ks[16], (DEPTH, D_MODEL, D_MODEL), f32) * (D_MODEL ** -0.5)
    return {"x": x, "p": p, "w_in": w_in, "conv_w": conv_w, "hgrn_lb": hgrn_lb,
            "hgrn_norm_g": hgrn_norm_g, "diff_lambda": diff_lambda, "diff_norm_g": diff_norm_g,
            "sg_ln_g": sg_ln_g, "sg_ln_b": sg_ln_b, "sg_w": sg_w, "sg_b": sg_b,
            "w_out": w_out, "ln_g": ln_g, "ln_b": ln_b, "w_pe": w_pe, "w_pg": w_pg}


def reference(x, p, w_in, conv_w, hgrn_lb, hgrn_norm_g, diff_lambda, diff_norm_g,
              sg_ln_g, sg_ln_b, sg_w, sg_b, w_out, ln_g, ln_b, w_pe, w_pg):
    lb_sm = jax.nn.softmax(hgrn_lb.astype(jnp.float32), axis=0)
    lower_bounds = jnp.cumsum(lb_sm, axis=0) - lb_sm[0]
    for i in range(DEPTH):
        lam_init = 0.8 - 0.6 * math.exp(-0.3 * i)
        lq1, lk1, lq2, lk2 = (diff_lambda[i, j].astype(jnp.float32) for j in range(4))
        lam = jnp.exp(jnp.sum(lq1 * lk1)) - jnp.exp(jnp.sum(lq2 * lk2)) + lam_init

        h = x @ w_in[i]
        (a_b, a_c, a_x, b_q, b_f, b_i, c_q, c_k, c_v,
         d_u, d_v, g_a, g_b, g_c, g_d) = jnp.split(h, N_IN_BLOCKS, axis=-1)

        y_a = short_conv_mixer(a_b, a_c, a_x, conv_w[i])
        y_b = hgrn2_mixer(b_q, b_f, b_i, lower_bounds[i], hgrn_norm_g[i])
        y_c = diff_attn_mixer(c_q, c_k, c_v, lam, lam_init, diff_norm_g[i])
        y_d = spatial_gate_mixer(jax.nn.gelu(d_u), jax.nn.gelu(d_v), sg_ln_g[i], sg_ln_b[i], sg_w[i], sg_b[i])

        y = jnp.concatenate([y_a * jax.nn.silu(g_a), y_b * jax.nn.silu(g_b),
                             y_c * jax.nn.silu(g_c), y_d * jax.nn.silu(g_d)], axis=-1) @ w_out[i]
        x = _layer_norm(ALPHA * x + y, ln_g[i], ln_b[i])
        x = x + (p[i] @ w_pe[i]) * jax.nn.sigmoid(x @ w_pg[i])
    return x
```

```python
import functools
import math

import numpy as np
import jax
import jax.numpy as jnp
from jax import lax
from jax.experimental import pallas as pl
from jax.experimental.pallas import tpu as pltpu

F32 = jnp.float32
BF16 = jnp.bfloat16

GROUP_W = 512
HEAD_W = 128
N_HEADS = GROUP_W // HEAD_W
DA_DQK = 64
F_FLOOR = 1e-30
MASK_VALUE = -1e30
LN_EPS = 1e-5
RMS_EPS = 1e-6

A_B, A_C, A_X, B_Q, B_F, B_I, C_Q, C_K, C_V, D_U, D_V, G_A, G_B, G_C, G_D = range(15)

HG_CHUNK = 128
HG_SUB = 8
HG_LEVELS = (16, 32, 64, 128)
SG_CHUNK = 128
VMEM_LIMIT = 56 * 1024 * 1024


def _sigmoid(x):
    return 1.0 / (1.0 + jnp.exp(-x))


def _silu(x):
    return x * _sigmoid(x)


def _gelu_tanh(x):
    c = math.sqrt(2.0 / math.pi)
    return 0.5 * x * (1.0 + jnp.tanh(c * (x + 0.044715 * (x * x * x))))


def _matmul_kernel(x_ref, w_ref, o_ref):
    o_ref[...] = jnp.dot(x_ref[...], w_ref[...],
                         preferred_element_type=F32).astype(o_ref.dtype)


def _in_proj(xb, wb, tm=1024, tn=512):
    t, k = xb.shape
    n = wb.shape[1]
    return pl.pallas_call(
        _matmul_kernel,
        grid=(n // tn, t // tm),
        in_specs=[pl.BlockSpec((tm, k), lambda j, i: (i, 0)),
                  pl.BlockSpec((k, tn), lambda j, i: (0, j))],
        out_specs=pl.BlockSpec((tm, tn), lambda j, i: (i, j)),
        out_shape=jax.ShapeDtypeStruct((t, n), BF16),
        compiler_params=pltpu.CompilerParams(
            dimension_semantics=("arbitrary", "arbitrary"),
            vmem_limit_bytes=VMEM_LIMIT),
        name="in_proj",
    )(xb, wb)


def _conv_sgu_kernel(ab_ref, ac_ref, ax_ref, ga_ref, du_ref, dv_ref, gd_ref,
                     cw_ref, lng_ref, lnb_ref, ws_ref, bst_ref,
                     ya_ref, yd_ref, carry_ref, *, ts, seq):
    i = pl.program_id(0)

    @pl.when((i * ts) % seq == 0)
    def _():
        carry_ref[...] = jnp.zeros_like(carry_ref)

    z = ac_ref[...].astype(F32) * ax_ref[...].astype(F32)
    rows = lax.broadcasted_iota(jnp.int32, z.shape, 0)
    prev1 = carry_ref[7:8, :]
    prev2 = carry_ref[6:7, :]
    z1 = jnp.where(rows == 0, prev1, pltpu.roll(z, 1, 0))
    z2 = jnp.where(rows == 0, prev2, jnp.where(rows == 1, prev1, pltpu.roll(z, 2, 0)))
    cw = cw_ref[...]
    y = cw[0:1, :] * z2 + cw[1:2, :] * z1 + cw[2:3, :] * z
    ya = ab_ref[...].astype(F32) * y * _silu(ga_ref[...].astype(F32))
    ya_ref[...] = ya.astype(ya_ref.dtype)
    carry_ref[...] = z[ts - 8:ts, :]

    u = _gelu_tanh(du_ref[...].astype(F32))
    v = _gelu_tanh(dv_ref[...].astype(F32))
    mu = jnp.mean(v, axis=-1, keepdims=True)
    vc = v - mu
    var = jnp.mean(vc * vc, axis=-1, keepdims=True)
    vn = (vc * lax.rsqrt(var + LN_EPS) * lng_ref[...] + lnb_ref[...]).astype(BF16)
    gate = _silu(gd_ref[...].astype(F32))
    tri_r = lax.broadcasted_iota(jnp.int32, (SG_CHUNK, SG_CHUNK), 0)
    tri_c = lax.broadcasted_iota(jnp.int32, (SG_CHUNK, SG_CHUNK), 1)
    bst = bst_ref[...]
    for g in range(N_HEADS):
        w = jnp.where(tri_c <= tri_r, ws_ref[g], 0.0).astype(BF16)
        bias = bst[:, g:g + 1]
        lo = g * HEAD_W
        for c in range(ts // SG_CHUNK):
            r0 = c * SG_CHUNK
            sv = jnp.dot(w, vn[r0:r0 + SG_CHUNK, lo:lo + HEAD_W],
                         preferred_element_type=F32) + bias
            yd = u[r0:r0 + SG_CHUNK, lo:lo + HEAD_W] * sv * gate[r0:r0 + SG_CHUNK, lo:lo + HEAD_W]
            yd_ref[r0:r0 + SG_CHUNK, lo:lo + HEAD_W] = yd.astype(yd_ref.dtype)


def _conv_sgu(h, conv_w, sg_ln_g, sg_ln_b, sg_w, sg_b, seq, ts=256):
    t = h.shape[0]

    def hblk(blk):
        return pl.BlockSpec((ts, GROUP_W), lambda i, blk=blk: (i, blk))

    def full(a):
        nd = a.ndim
        return pl.BlockSpec(a.shape, lambda i, nd=nd: (0,) * nd)

    lng = sg_ln_g.reshape(1, GROUP_W)
    lnb = sg_ln_b.reshape(1, GROUP_W)
    bst = sg_b.T
    kern = functools.partial(_conv_sgu_kernel, ts=ts, seq=seq)
    return pl.pallas_call(
        kern,
        grid=(t // ts,),
        in_specs=[hblk(A_B), hblk(A_C), hblk(A_X), hblk(G_A), hblk(D_U), hblk(D_V), hblk(G_D),
                  full(conv_w), full(lng), full(lnb), full(sg_w), full(bst)],
        out_specs=[pl.BlockSpec((ts, GROUP_W), lambda i: (i, 0)),
                   pl.BlockSpec((ts, GROUP_W), lambda i: (i, 0))],
        out_shape=[jax.ShapeDtypeStruct((t, GROUP_W), BF16),
                   jax.ShapeDtypeStruct((t, GROUP_W), BF16)],
        scratch_shapes=[pltpu.VMEM((8, GROUP_W), F32)],
        compiler_params=pltpu.CompilerParams(
            dimension_semantics=("arbitrary",), vmem_limit_bytes=VMEM_LIMIT),
        name="conv_sgu",
    )(h, h, h, h, h, h, h, conv_w, lng, lnb, sg_w, bst)


def _hgrn_constants():
    c = HG_CHUNK
    t = np.arange(c)[:, None]
    s = np.arange(c)[None, :]
    mats = [(s <= t), (s > t)]
    for lv in HG_LEVELS:
        mid = (t // lv) * lv + lv // 2
        qside = t >= mid
        mats.append(np.where(qside, (s >= mid) & (s <= t), (s > t) & (s < mid)))
    rng = np.concatenate(mats, axis=0).astype(np.float32)
    masks = np.stack([(t // lv) == (s // lv) for lv in HG_LEVELS]).astype(np.float32)
    return jnp.asarray(rng, BF16), jnp.asarray(masks, F32)


def _hgrn_kernel(q_ref, f_ref, i_ref, g_ref, lb_ref, ng_ref, rng_ref, msk_ref,
                 y_ref, st_ref, *, seq):
    c = HG_CHUNK
    step = pl.program_id(0)

    @pl.when((step * c) % seq == 0)
    def _():
        st_ref[...] = jnp.zeros_like(st_ref)

    rng = rng_ref[...]
    rows = lax.broadcasted_iota(jnp.int32, (c, HEAD_W), 0)
    for h in range(N_HEADS):
        lo = h * HEAD_W
        fz = f_ref[:, lo:lo + HEAD_W].astype(F32)
        q = q_ref[:, lo:lo + HEAD_W].astype(F32)
        v = i_ref[:, lo:lo + HEAD_W].astype(F32)
        lb = lb_ref[:, lo:lo + HEAD_W]
        e = jnp.exp(-jnp.abs(fz))
        r = 1.0 / (1.0 + e)
        pos = fz >= 0.0
        sig = jnp.where(pos, r, e * r)
        nsig = jnp.where(pos, e * r, r)
        f = lb + (1.0 - lb) * sig
        lg = jnp.log(jnp.maximum(f, F_FLOOR))
        kk = (1.0 - lb) * nsig

        g1 = lg.astype(BF16)
        r1 = lg - g1.astype(F32)
        g2 = r1.astype(BF16)
        g3 = (r1 - g2.astype(F32)).astype(BF16)
        ex = (jnp.dot(rng, g1, preferred_element_type=F32)
              + jnp.dot(rng, g2, preferred_element_type=F32)
              + jnp.dot(rng, g3, preferred_element_type=F32))

        vb = v.astype(BF16)
        st = st_ref[h]
        bcum = ex[0:c]
        q_in = (q * jnp.exp(bcum)).astype(BF16)
        o = lax.dot_general(q_in, st.astype(BF16), (((1,), (1,)), ((), ())),
                            preferred_element_type=F32)
        k_out = (kk * jnp.exp(ex[c:2 * c])).astype(BF16)
        e_last = jnp.exp(bcum[c - 1:c, :])
        st_ref[h] = e_last * st + jnp.dot(v.T.astype(BF16), k_out,
                                          preferred_element_type=F32)

        a = jnp.zeros((c, c), F32)
        for li, lv in enumerate(HG_LEVELS):
            el = jnp.exp(ex[(2 + li) * c:(3 + li) * c])
            qside = (rows % lv) >= (lv // 2)
            ql = jnp.where(qside, q * el, 0.0).astype(BF16)
            kl = jnp.where(qside, 0.0, kk * el).astype(BF16)
            al = lax.dot_general(ql, kl, (((1,), (1,)), ((), ())),
                                 preferred_element_type=F32)
            a = a + msk_ref[li] * al
        o = o + jnp.dot(a.astype(BF16), vb, preferred_element_type=F32)

        dsum = jnp.zeros((c, HEAD_W), F32)
        for d in range(HG_SUB):
            if d == 0:
                k_d, v_d = kk, v
            else:
                k_d, v_d = pltpu.roll(kk, d, 0), pltpu.roll(v, d, 0)
                dsum = dsum + (lg if d == 1 else pltpu.roll(lg, d - 1, 0))
            valid = (rows % HG_SUB) >= d
            pd = jnp.where(valid, q * k_d * jnp.exp(dsum), 0.0)
            o = o + jnp.sum(pd, axis=-1, keepdims=True) * v_d

        ms = jnp.mean(o * o, axis=-1, keepdims=True)
        on = o * lax.rsqrt(ms + RMS_EPS) * ng_ref[:, lo:lo + HEAD_W]
        y = on * _silu(g_ref[:, lo:lo + HEAD_W].astype(F32))
        y_ref[:, lo:lo + HEAD_W] = y.astype(y_ref.dtype)


def _hgrn(h, lb, norm_g, seq):
    t = h.shape[0]
    c = HG_CHUNK
    rng, masks = _hgrn_constants()

    def hblk(blk):
        return pl.BlockSpec((c, GROUP_W), lambda i, blk=blk: (i, blk))

    def full(a):
        nd = a.ndim
        return pl.BlockSpec(a.shape, lambda i, nd=nd: (0,) * nd)

    lb2 = lb.reshape(1, GROUP_W).astype(F32)
    ng2 = norm_g.reshape(1, GROUP_W).astype(F32)
    kern = functools.partial(_hgrn_kernel, seq=seq)
    return pl.pallas_call(
        kern,
        grid=(t // c,),
        in_specs=[hblk(B_Q), hblk(B_F), hblk(B_I), hblk(G_B),
                  full(lb2), full(ng2), full(rng), full(masks)],
        out_specs=pl.BlockSpec((c, GROUP_W), lambda i: (i, 0)),
        out_shape=jax.ShapeDtypeStruct((t, GROUP_W), BF16),
        scratch_shapes=[pltpu.VMEM((N_HEADS, HEAD_W, HEAD_W), F32)],
        compiler_params=pltpu.CompilerParams(
            dimension_semantics=("arbitrary",), vmem_limit_bytes=VMEM_LIMIT),
        name="hgrn2",
    )(h, h, h, h, lb2, ng2, rng, masks)


def _diff_attn_kernel(lam_ref, q_ref, k_ref, v_ref, g_ref, ng_ref, o_ref, acc_ref,
                      *, tq, out_scale):
    i = pl.program_id(2)
    lam = lam_ref[0, 0]
    scale = DA_DQK ** -0.5

    q = q_ref[...] * jnp.asarray(scale, BF16)
    lane = lax.broadcasted_iota(jnp.int32, q.shape, 1)
    zero = jnp.zeros_like(q)
    qq = jnp.concatenate([jnp.where(lane < DA_DQK, q, zero),
                          jnp.where(lane >= DA_DQK, q, zero)], axis=0)

    def scores(j):
        kb = k_ref[pl.ds(pl.multiple_of(j * tq, tq), tq), :]
        return lax.dot_general(qq, kb, (((1,), (1,)), ((), ())),
                               preferred_element_type=F32)

    def update(j, s, m, l):
        m_new = jnp.maximum(m, jnp.max(s, axis=-1, keepdims=True))
        alpha = jnp.exp(m - m_new)
        p = jnp.exp(s - m_new)
        l_new = alpha * l + jnp.sum(p, axis=-1, keepdims=True)
        vb = v_ref[pl.ds(pl.multiple_of(j * tq, tq), tq), :]
        acc_ref[...] = alpha * acc_ref[...] + jnp.dot(p.astype(BF16), vb,
                                                      preferred_element_type=F32)
        return m_new, l_new

    acc_ref[...] = jnp.zeros_like(acc_ref)
    m0 = jnp.full((2 * tq, 1), MASK_VALUE, F32)
    l0 = jnp.zeros((2 * tq, 1), F32)

    def body(j, carry):
        m, l = carry
        return update(j, scores(j), m, l)

    m, l = lax.fori_loop(0, i, body, (m0, l0))

    s = scores(i)
    r = lax.broadcasted_iota(jnp.int32, s.shape, 0)
    cidx = lax.broadcasted_iota(jnp.int32, s.shape, 1)
    r = jnp.where(r >= tq, r - tq, r)
    s = jnp.where(cidx <= r, s, MASK_VALUE)
    m, l = update(i, s, m, l)

    acc = acc_ref[...] / l
    o = acc[0:tq] - lam * acc[tq:2 * tq]
    ms = jnp.mean(o * o, axis=-1, keepdims=True)
    on = o * lax.rsqrt(ms + RMS_EPS) * ng_ref[...] * out_scale
    o_ref[...] = (on * _silu(g_ref[...].astype(F32))).astype(o_ref.dtype)


def _diff_attn(h, lam, lam_init, norm_g, batch, seq, tq=256):
    t = h.shape[0]
    nq = seq // tq
    per = GROUP_W // HEAD_W
    ng2 = norm_g.reshape(1, GROUP_W).astype(F32)
    lam2 = jnp.reshape(lam, (1, 1)).astype(F32)
    kern = functools.partial(_diff_attn_kernel, tq=tq, out_scale=1.0 - lam_init)
    return pl.pallas_call(
        kern,
        grid=(batch, N_HEADS, nq),
        in_specs=[pl.BlockSpec(memory_space=pltpu.SMEM),
                  pl.BlockSpec((tq, HEAD_W), lambda b, hh, i: (b * nq + i, C_Q * per + hh)),
                  pl.BlockSpec((seq, HEAD_W), lambda b, hh, i: (b, C_K * per + hh)),
                  pl.BlockSpec((seq, HEAD_W), lambda b, hh, i: (b, C_V * per + hh)),
                  pl.BlockSpec((tq, HEAD_W), lambda b, hh, i: (b * nq + i, G_C * per + hh)),
                  pl.BlockSpec((1, HEAD_W), lambda b, hh, i: (0, hh))],
        out_specs=pl.BlockSpec((tq, HEAD_W), lambda b, hh, i: (b * nq + i, hh)),
        out_shape=jax.ShapeDtypeStruct((t, GROUP_W), BF16),
        scratch_shapes=[pltpu.VMEM((2 * tq, HEAD_W), F32)],
        compiler_params=pltpu.CompilerParams(
            dimension_semantics=("arbitrary", "arbitrary", "arbitrary"),
            vmem_limit_bytes=VMEM_LIMIT),
        name="diff_attn",
    )(lam2, h, h, h, h, ng2)


def _out_kernel(ya_ref, yb_ref, yc_ref, yd_ref, x_ref, p_ref, wo_ref, wpg_ref, wpe_ref,
                lng_ref, lnb_ref, xo_ref, xb_ref, *, alpha):
    acc = alpha * x_ref[...]
    for g, y_ref in enumerate((ya_ref, yb_ref, yc_ref, yd_ref)):
        acc = acc + jnp.dot(y_ref[...], wo_ref[g * GROUP_W:(g + 1) * GROUP_W, :],
                            preferred_element_type=F32)
    mu = jnp.mean(acc, axis=-1, keepdims=True)
    xc = acc - mu
    var = jnp.mean(xc * xc, axis=-1, keepdims=True)
    xn = xc * lax.rsqrt(var + LN_EPS) * lng_ref[...] + lnb_ref[...]
    z = jnp.dot(xn.astype(BF16), wpg_ref[...], preferred_element_type=F32)
    pe = jnp.dot(p_ref[...].astype(BF16), wpe_ref[...], preferred_element_type=F32)
    out = xn + pe * _sigmoid(z)
    xo_ref[...] = out
    xb_ref[...] = out.astype(BF16)


def _out_block(ya, yb, yc, yd, x, p, wo, wpg, wpe, ln_g, ln_b, alpha, tm=256):
    t, d = x.shape
    pdim = p.shape[1]

    def rows(w):
        return pl.BlockSpec((tm, w), lambda i: (i, 0))

    def resident(a):
        return pl.BlockSpec(a.shape, lambda i: (0, 0), pipeline_mode=pl.Buffered(1))

    lng = ln_g.reshape(1, d).astype(F32)
    lnb = ln_b.reshape(1, d).astype(F32)
    kern = functools.partial(_out_kernel, alpha=alpha)
    return pl.pallas_call(
        kern,
        grid=(t // tm,),
        in_specs=[rows(GROUP_W), rows(GROUP_W), rows(GROUP_W), rows(GROUP_W),
                  rows(d), rows(pdim),
                  resident(wo), resident(wpg), resident(wpe), resident(lng), resident(lnb)],
        out_specs=[rows(d), rows(d)],
        out_shape=[jax.ShapeDtypeStruct((t, d), F32), jax.ShapeDtypeStruct((t, d), BF16)],
        compiler_params=pltpu.CompilerParams(
            dimension_semantics=("arbitrary",), vmem_limit_bytes=VMEM_LIMIT),
        name="out_block",
    )(ya, yb, yc, yd, x, p, wo, wpg, wpe, lng, lnb)


def kernel(x, p, w_in, conv_w, hgrn_lb, hgrn_norm_g, diff_lambda, diff_norm_g,
           sg_ln_g, sg_ln_b, sg_w, sg_b, w_out, ln_g, ln_b, w_pe, w_pg):
    batch, seq, d_model = x.shape
    depth = w_in.shape[0]
    t = batch * seq
    alpha = (2 * depth) ** 0.25

    lb_sm = jax.nn.softmax(hgrn_lb.astype(F32), axis=0)
    lower_bounds = jnp.cumsum(lb_sm, axis=0) - lb_sm[0]

    xf = x.reshape(t, d_model)
    xb = xf.astype(BF16)
    for i in range(depth):
        lam_init = 0.8 - 0.6 * math.exp(-0.3 * i)
        dl = diff_lambda[i].astype(F32)
        lam = (jnp.exp(jnp.sum(dl[0] * dl[1])) - jnp.exp(jnp.sum(dl[2] * dl[3])) + lam_init)

        h = _in_proj(xb, w_in[i].astype(BF16))
        ya, yd = _conv_sgu(h, conv_w[i], sg_ln_g[i], sg_ln_b[i], sg_w[i], sg_b[i], seq)
        yb = _hgrn(h, lower_bounds[i], hgrn_norm_g[i], seq)
        yc = _diff_attn(h, lam, lam_init, diff_norm_g[i], batch, seq)
        xf, xb = _out_block(ya, yb, yc, yd, xf, p[i].reshape(t, -1),
                            w_out[i].astype(BF16), w_pg[i].astype(BF16), w_pe[i].astype(BF16),
                            ln_g[i], ln_b[i], alpha)
    return xf.reshape(batch, seq, d_model)
```

```python
import functools
import math

import numpy as np
import jax
import jax.numpy as jnp
from jax import lax
from jax.experimental import pallas as pl
from jax.experimental.pallas import tpu as pltpu

F32 = jnp.float32
BF16 = jnp.bfloat16

GROUP_W = 512
HEAD_W = 128
N_HEADS = GROUP_W // HEAD_W
DA_DQK = 64
F_FLOOR = 1e-30
MASK_VALUE = -1e30
LN_EPS = 1e-5
RMS_EPS = 1e-6

A_B, A_C, A_X, B_Q, B_F, B_I, C_Q, C_K, C_V, D_U, D_V, G_A, G_B, G_C, G_D = range(15)

HG_CHUNK = 128
HG_SUB = 8
HG_LEVELS = (16, 32, 64, 128)
SG_CHUNK = 128
VMEM_LIMIT = 56 * 1024 * 1024


def _sigmoid(x):
    return 1.0 / (1.0 + jnp.exp(-x))


def _silu(x):
    return x * _sigmoid(x)


def _gelu_tanh(x):
    c = math.sqrt(2.0 / math.pi)
    return 0.5 * x * (1.0 + jnp.tanh(c * (x + 0.044715 * (x * x * x))))


def _matmul_kernel(x_ref, w_ref, o_ref):
    o_ref[...] = jnp.dot(x_ref[...], w_ref[...],
                         preferred_element_type=F32).astype(o_ref.dtype)


def _in_proj(xb, wb, tm=1024, tn=512):
    t, k = xb.shape
    n = wb.shape[1]
    return pl.pallas_call(
        _matmul_kernel,
        grid=(n // tn, t // tm),
        in_specs=[pl.BlockSpec((tm, k), lambda j, i: (i, 0)),
                  pl.BlockSpec((k, tn), lambda j, i: (0, j))],
        out_specs=pl.BlockSpec((tm, tn), lambda j, i: (i, j)),
        out_shape=jax.ShapeDtypeStruct((t, n), BF16),
        compiler_params=pltpu.CompilerParams(
            dimension_semantics=("arbitrary", "arbitrary"),
            vmem_limit_bytes=VMEM_LIMIT),
        name="in_proj",
    )(xb, wb)


def _conv_sgu_kernel(ab_ref, ac_ref, ax_ref, ga_ref, du_ref, dv_ref, gd_ref,
                     cw_ref, lng_ref, lnb_ref, ws_ref, bst_ref,
                     ya_ref, yd_ref, carry_ref, *, ts, seq):
    i = pl.program_id(0)

    @pl.when((i * ts) % seq == 0)
    def _():
        carry_ref[...] = jnp.zeros_like(carry_ref)

    z = ac_ref[...].astype(F32) * ax_ref[...].astype(F32)
    rows = lax.broadcasted_iota(jnp.int32, z.shape, 0)
    prev1 = carry_ref[7:8, :]
    prev2 = carry_ref[6:7, :]
    z1 = jnp.where(rows == 0, prev1, pltpu.roll(z, 1, 0))
    z2 = jnp.where(rows == 0, prev2, jnp.where(rows == 1, prev1, pltpu.roll(z, 2, 0)))
    cw = cw_ref[...]
    y = cw[0:1, :] * z2 + cw[1:2, :] * z1 + cw[2:3, :] * z
    ya = ab_ref[...].astype(F32) * y * _silu(ga_ref[...].astype(F32))
    ya_ref[...] = ya.astype(ya_ref.dtype)
    carry_ref[...] = z[ts - 8:ts, :]

    u = _gelu_tanh(du_ref[...].astype(F32))
    v = _gelu_tanh(dv_ref[...].astype(F32))
    mu = jnp.mean(v, axis=-1, keepdims=True)
    vc = v - mu
    var = jnp.mean(vc * vc, axis=-1, keepdims=True)
    vn = (vc * lax.rsqrt(var + LN_EPS) * lng_ref[...] + lnb_ref[...]).astype(BF16)
    gate = _silu(gd_ref[...].astype(F32))
    tri_r = lax.broadcasted_iota(jnp.int32, (SG_CHUNK, SG_CHUNK), 0)
    tri_c = lax.broadcasted_iota(jnp.int32, (SG_CHUNK, SG_CHUNK), 1)
    bst = bst_ref[...]
    for g in range(N_HEADS):
        w = jnp.where(tri_c <= tri_r, ws_ref[g], 0.0).astype(BF16)
        bias = bst[:, g:g + 1]
        lo = g * HEAD_W
        for c in range(ts // SG_CHUNK):
            r0 = c * SG_CHUNK
            sv = jnp.dot(w, vn[r0:r0 + SG_CHUNK, lo:lo + HEAD_W],
                         preferred_element_type=F32) + bias
            yd = u[r0:r0 + SG_CHUNK, lo:lo + HEAD_W] * sv * gate[r0:r0 + SG_CHUNK, lo:lo + HEAD_W]
            yd_ref[r0:r0 + SG_CHUNK, lo:lo + HEAD_W] = yd.astype(yd_ref.dtype)


def _conv_sgu(h, conv_w, sg_ln_g, sg_ln_b, sg_w, sg_b, seq, ts=256):
    t = h.shape[0]

    def hblk(blk):
        return pl.BlockSpec((ts, GROUP_W), lambda i, blk=blk: (i, blk))

    def full(a):
        nd = a.ndim
        return pl.BlockSpec(a.shape, lambda i, nd=nd: (0,) * nd)

    lng = sg_ln_g.reshape(1, GROUP_W)
    lnb = sg_ln_b.reshape(1, GROUP_W)
    bst = sg_b.T
    kern = functools.partial(_conv_sgu_kernel, ts=ts, seq=seq)
    return pl.pallas_call(
        kern,
        grid=(t // ts,),
        in_specs=[hblk(A_B), hblk(A_C), hblk(A_X), hblk(G_A), hblk(D_U), hblk(D_V), hblk(G_D),
                  full(conv_w), full(lng), full(lnb), full(sg_w), full(bst)],
        out_specs=[pl.BlockSpec((ts, GROUP_W), lambda i: (i, 0)),
                   pl.BlockSpec((ts, GROUP_W), lambda i: (i, 0))],
        out_shape=[jax.ShapeDtypeStruct((t, GROUP_W), BF16),
                   jax.ShapeDtypeStruct((t, GROUP_W), BF16)],
        scratch_shapes=[pltpu.VMEM((8, GROUP_W), F32)],
        compiler_params=pltpu.CompilerParams(
            dimension_semantics=("arbitrary",), vmem_limit_bytes=VMEM_LIMIT),
        name="conv_sgu",
    )(h, h, h, h, h, h, h, conv_w, lng, lnb, sg_w, bst)


def _hgrn_constants():
    c = HG_CHUNK
    t = np.arange(c)[:, None]
    s = np.arange(c)[None, :]
    mats = [(s <= t), (s > t)]
    for lv in HG_LEVELS:
        mid = (t // lv) * lv + lv // 2
        qside = t >= mid
        mats.append(np.where(qside, (s >= mid) & (s <= t), (s > t) & (s < mid)))
    rng = np.concatenate(mats, axis=0).astype(np.float32)
    masks = np.stack([(t // lv) == (s // lv) for lv in HG_LEVELS]).astype(np.float32)
    return jnp.asarray(rng, BF16), jnp.asarray(masks, F32)


def _hgrn_kernel(q_ref, f_ref, i_ref, g_ref, lb_ref, ng_ref, rng_ref, msk_ref,
                 y_ref, st_ref, *, seq):
    c = HG_CHUNK
    step = pl.program_id(0)

    @pl.when((step * c) % seq == 0)
    def _():
        st_ref[...] = jnp.zeros_like(st_ref)

    rng = rng_ref[...]
    rows = lax.broadcasted_iota(jnp.int32, (c, HEAD_W), 0)
    for h in range(N_HEADS):
        lo = h * HEAD_W
        fz = f_ref[:, lo:lo + HEAD_W].astype(F32)
        q = q_ref[:, lo:lo + HEAD_W].astype(F32)
        v = i_ref[:, lo:lo + HEAD_W].astype(F32)
        lb = lb_ref[:, lo:lo + HEAD_W]
        e = jnp.exp(-jnp.abs(fz))
        r = 1.0 / (1.0 + e)
        pos = fz >= 0.0
        sig = jnp.where(pos, r, e * r)
        nsig = jnp.where(pos, e * r, r)
        f = lb + (1.0 - lb) * sig
        lg = jnp.log(jnp.maximum(f, F_FLOOR))
        kk = (1.0 - lb) * nsig

        g1 = lg.astype(BF16)
        r1 = lg - g1.astype(F32)
        g2 = r1.astype(BF16)
        g3 = (r1 - g2.astype(F32)).astype(BF16)
        ex = (jnp.dot(rng, g1, preferred_element_type=F32)
              + jnp.dot(rng, g2, preferred_element_type=F32)
              + jnp.dot(rng, g3, preferred_element_type=F32))

        vb = v.astype(BF16)
        st = st_ref[h]
        bcum = ex[0:c]
        q_in = (q * jnp.exp(bcum)).astype(BF16)
        o = lax.dot_general(q_in, st.astype(BF16), (((1,), (1,)), ((), ())),
                            preferred_element_type=F32)
        k_out = (kk * jnp.exp(ex[c:2 * c])).astype(BF16)
        e_last = jnp.exp(bcum[c - 1:c, :])
        st_ref[h] = e_last * st + jnp.dot(v.T.astype(BF16), k_out,
                                          preferred_element_type=F32)

        a = jnp.zeros((c, c), F32)
        for li, lv in enumerate(HG_LEVELS):
            el = jnp.exp(ex[(2 + li) * c:(3 + li) * c])
            qside = (rows % lv) >= (lv // 2)
            ql = jnp.where(qside, q * el, 0.0).astype(BF16)
            kl = jnp.where(qside, 0.0, kk * el).astype(BF16)
            al = lax.dot_general(ql, kl, (((1,), (1,)), ((), ())),
                                 preferred_element_type=F32)
            a = a + msk_ref[li] * al
        o = o + jnp.dot(a.astype(BF16), vb, preferred_element_type=F32)

        dsum = jnp.zeros((c, HEAD_W), F32)
        for d in range(HG_SUB):
            if d == 0:
                k_d, v_d = kk, v
            else:
                k_d, v_d = pltpu.roll(kk, d, 0), pltpu.roll(v, d, 0)
                dsum = dsum + (lg if d == 1 else pltpu.roll(lg, d - 1, 0))
            valid = (rows % HG_SUB) >= d
            pd = jnp.where(valid, q * k_d * jnp.exp(dsum), 0.0)
            o = o + jnp.sum(pd, axis=-1, keepdims=True) * v_d

        ms = jnp.mean(o * o, axis=-1, keepdims=True)
        on = o * lax.rsqrt(ms + RMS_EPS) * ng_ref[:, lo:lo + HEAD_W]
        y = on * _silu(g_ref[:, lo:lo + HEAD_W].astype(F32))
        y_ref[:, lo:lo + HEAD_W] = y.astype(y_ref.dtype)


def _hgrn(h, lb, norm_g, seq):
    t = h.shape[0]
    c = HG_CHUNK
    rng, masks = _hgrn_constants()

    def hblk(blk):
        return pl.BlockSpec((c, GROUP_W), lambda i, blk=blk: (i, blk))

    def full(a):
        nd = a.ndim
        return pl.BlockSpec(a.shape, lambda i, nd=nd: (0,) * nd)

    lb2 = lb.reshape(1, GROUP_W).astype(F32)
    ng2 = norm_g.reshape(1, GROUP_W).astype(F32)
    kern = functools.partial(_hgrn_kernel, seq=seq)
    return pl.pallas_call(
        kern,
        grid=(t // c,),
        in_specs=[hblk(B_Q), hblk(B_F), hblk(B_I), hblk(G_B),
                  full(lb2), full(ng2), full(rng), full(masks)],
        out_specs=pl.BlockSpec((c, GROUP_W), lambda i: (i, 0)),
        out_shape=jax.ShapeDtypeStruct((t, GROUP_W), BF16),
        scratch_shapes=[pltpu.VMEM((N_HEADS, HEAD_W, HEAD_W), F32)],
        compiler_params=pltpu.CompilerParams(
            dimension_semantics=("arbitrary",), vmem_limit_bytes=VMEM_LIMIT),
        name="hgrn2",
    )(h, h, h, h, lb2, ng2, rng, masks)


ATTN_ROWS = 64


def _diff_attn_kernel(lam_ref, q_ref, k_ref, v_ref, g_ref, ng_ref, o_ref,
                      qq_ref, sa_ref, sb_ref, pa_ref, pb_ref, pt_ref,
                      ma_ref, la_ref, aa_ref, mb_ref, lb_ref, ab_ref, mt_ref, lt_ref, at_ref,
                      acc_ref, *, tq, out_scale):
    i = pl.program_id(2)
    lam = lam_ref[0, 0]
    scale = DA_DQK ** -0.5
    tk = tq
    ncb = tk // HEAD_W
    st_a = (ma_ref, la_ref, aa_ref)
    st_b = (mb_ref, lb_ref, ab_ref)
    st_t = (mt_ref, lt_ref, at_ref)

    q = q_ref[...] * jnp.asarray(scale, BF16)
    lane = lax.broadcasted_iota(jnp.int32, q.shape, 1)
    zero = jnp.zeros_like(q)
    qq_ref[0:tq, :] = jnp.where(lane < DA_DQK, q, zero)
    qq_ref[tq:2 * tq, :] = jnp.where(lane >= DA_DQK, q, zero)

    def scores(j):
        kb = k_ref[pl.ds(pl.multiple_of(j * tk, tk), tk), :]
        return lax.dot_general(qq_ref[...], kb, (((1,), (1,)), ((), ())),
                               preferred_element_type=F32)

    def accumulate(j, p_ref, st):
        vb = v_ref[pl.ds(pl.multiple_of(j * tk, tk), tk), :]
        acc_ref[...] = st[2][...] * acc_ref[...] + jnp.dot(p_ref[...], vb,
                                                            preferred_element_type=F32)

    def softmax(s_ref, p_ref, st_in, st_out, masked):
        for c in range(2 * tq // ATTN_ROWS):
            r0 = c * ATTN_ROWS
            rs = slice(r0, r0 + ATTN_ROWS)
            row_lo = r0 % tq
            tiles = []
            for cb in range(ncb):
                if masked and cb * HEAD_W > row_lo + ATTN_ROWS - 1:
                    tiles.append(None)
                    continue
                s = s_ref[rs, cb * HEAD_W:(cb + 1) * HEAD_W]
                if masked and (cb + 1) * HEAD_W - 1 > row_lo:
                    r = row_lo + lax.broadcasted_iota(jnp.int32, s.shape, 0)
                    cidx = cb * HEAD_W + lax.broadcasted_iota(jnp.int32, s.shape, 1)
                    s = jnp.where(cidx <= r, s, MASK_VALUE)
                tiles.append(s)
            live = [s for s in tiles if s is not None]
            mx = functools.reduce(jnp.maximum, live)
            m_blk = jnp.broadcast_to(jnp.max(mx, axis=-1, keepdims=True), mx.shape)
            m_old = st_in[0][rs, :]
            m_new = jnp.maximum(m_old, m_blk)
            alpha = jnp.exp(m_old - m_new)
            psum = None
            for cb, s in enumerate(tiles):
                if s is None:
                    p_ref[rs, cb * HEAD_W:(cb + 1) * HEAD_W] = jnp.zeros((ATTN_ROWS, HEAD_W), BF16)
                    continue
                p = jnp.exp(s - m_new)
                psum = p if psum is None else psum + p
                p_ref[rs, cb * HEAD_W:(cb + 1) * HEAD_W] = p.astype(BF16)
            st_out[0][rs, :] = m_new
            st_out[1][rs, :] = alpha * st_in[1][rs, :] + psum
            st_out[2][rs, :] = alpha

    acc_ref[...] = jnp.zeros_like(acc_ref)
    mb_ref[...] = jnp.full_like(mb_ref, MASK_VALUE)
    lb_ref[...] = jnp.zeros_like(lb_ref)
    ab_ref[...] = jnp.ones_like(ab_ref)
    pb_ref[...] = jnp.zeros_like(pb_ref)
    sa_ref[...] = scores(0)

    def half_step(j, s_in, s_out, p_in, p_out, st_in, st_out):
        s_out[...] = scores(j + 1)
        accumulate(jnp.maximum(j - 1, 0), p_in, st_in)
        softmax(s_in, p_out, st_in, st_out, masked=False)

    def pair(jj, carry):
        half_step(2 * jj, sa_ref, sb_ref, pb_ref, pa_ref, st_b, st_a)
        half_step(2 * jj + 1, sb_ref, sa_ref, pa_ref, pb_ref, st_a, st_b)
        return carry

    lax.fori_loop(0, i // 2, pair, 0)

    def tail(s_in, p_in, p_out, st_in, st_out):
        accumulate(jnp.maximum(i - 1, 0), p_in, st_in)
        softmax(s_in, p_out, st_in, st_out, masked=True)
        accumulate(i, p_out, st_out)
        l = jnp.sum(st_out[1][...], axis=-1, keepdims=True)
        acc = acc_ref[...] / l
        o = acc[0:tq] - lam * acc[tq:2 * tq]
        ms = jnp.mean(o * o, axis=-1, keepdims=True)
        on = o * lax.rsqrt(ms + RMS_EPS) * ng_ref[...] * out_scale
        o_ref[...] = (on * _silu(g_ref[...].astype(F32))).astype(o_ref.dtype)

    @pl.when(i % 2 == 0)
    def _():
        tail(sa_ref, pb_ref, pt_ref, st_b, st_t)

    @pl.when(i % 2 == 1)
    def _():
        half_step(i - 1, sa_ref, sb_ref, pb_ref, pa_ref, st_b, st_a)
        tail(sb_ref, pa_ref, pt_ref, st_a, st_t)


def _diff_attn(h, lam, lam_init, norm_g, batch, seq, tq=512):
    t = h.shape[0]
    nq = seq // tq
    per = GROUP_W // HEAD_W
    ng2 = norm_g.reshape(1, GROUP_W).astype(F32)
    lam2 = jnp.reshape(lam, (1, 1)).astype(F32)
    kern = functools.partial(_diff_attn_kernel, tq=tq, out_scale=1.0 - lam_init)
    return pl.pallas_call(
        kern,
        grid=(batch, N_HEADS, nq),
        in_specs=[pl.BlockSpec(memory_space=pltpu.SMEM),
                  pl.BlockSpec((tq, HEAD_W), lambda b, hh, i: (b * nq + i, C_Q * per + hh)),
                  pl.BlockSpec((seq, HEAD_W), lambda b, hh, i: (b, C_K * per + hh)),
                  pl.BlockSpec((seq, HEAD_W), lambda b, hh, i: (b, C_V * per + hh)),
                  pl.BlockSpec((tq, HEAD_W), lambda b, hh, i: (b * nq + i, G_C * per + hh)),
                  pl.BlockSpec((1, HEAD_W), lambda b, hh, i: (0, hh))],
        out_specs=pl.BlockSpec((tq, HEAD_W), lambda b, hh, i: (b * nq + i, hh)),
        out_shape=jax.ShapeDtypeStruct((t, GROUP_W), BF16),
        scratch_shapes=[pltpu.VMEM((2 * tq, HEAD_W), BF16),
                        pltpu.VMEM((2 * tq, tq), F32), pltpu.VMEM((2 * tq, tq), F32),
                        *([pltpu.VMEM((2 * tq, tq), BF16)] * 3),
                        *([pltpu.VMEM((2 * tq, HEAD_W), F32)] * 10)],
        compiler_params=pltpu.CompilerParams(
            dimension_semantics=("arbitrary", "arbitrary", "arbitrary"),
            vmem_limit_bytes=VMEM_LIMIT),
        name="diff_attn",
    )(lam2, h, h, h, h, ng2)


def _out_kernel(ya_ref, yb_ref, yc_ref, yd_ref, x_ref, p_ref, wo_ref, wpg_ref, wpe_ref,
                lng_ref, lnb_ref, xo_ref, xb_ref, *, alpha):
    acc = alpha * x_ref[...]
    for g, y_ref in enumerate((ya_ref, yb_ref, yc_ref, yd_ref)):
        acc = acc + jnp.dot(y_ref[...], wo_ref[g * GROUP_W:(g + 1) * GROUP_W, :],
                            preferred_element_type=F32)
    mu = jnp.mean(acc, axis=-1, keepdims=True)
    xc = acc - mu
    var = jnp.mean(xc * xc, axis=-1, keepdims=True)
    xn = xc * lax.rsqrt(var + LN_EPS) * lng_ref[...] + lnb_ref[...]
    z = jnp.dot(xn.astype(BF16), wpg_ref[...], preferred_element_type=F32)
    pe = jnp.dot(p_ref[...].astype(BF16), wpe_ref[...], preferred_element_type=F32)
    out = xn + pe * _sigmoid(z)
    xo_ref[...] = out
    xb_ref[...] = out.astype(BF16)


def _out_block(ya, yb, yc, yd, x, p, wo, wpg, wpe, ln_g, ln_b, alpha, tm=256):
    t, d = x.shape
    pdim = p.shape[1]

    def rows(w):
        return pl.BlockSpec((tm, w), lambda i: (i, 0))

    def resident(a):
        return pl.BlockSpec(a.shape, lambda i: (0, 0), pipeline_mode=pl.Buffered(1))

    lng = ln_g.reshape(1, d).astype(F32)
    lnb = ln_b.reshape(1, d).astype(F32)
    kern = functools.partial(_out_kernel, alpha=alpha)
    return pl.pallas_call(
        kern,
        grid=(t // tm,),
        in_specs=[rows(GROUP_W), rows(GROUP_W), rows(GROUP_W), rows(GROUP_W),
                  rows(d), rows(pdim),
                  resident(wo), resident(wpg), resident(wpe), resident(lng), resident(lnb)],
        out_specs=[rows(d), rows(d)],
        out_shape=[jax.ShapeDtypeStruct((t, d), F32), jax.ShapeDtypeStruct((t, d), BF16)],
        compiler_params=pltpu.CompilerParams(
            dimension_semantics=("arbitrary",), vmem_limit_bytes=VMEM_LIMIT),
        name="out_block",
    )(ya, yb, yc, yd, x, p, wo, wpg, wpe, lng, lnb)


def kernel(x, p, w_in, conv_w, hgrn_lb, hgrn_norm_g, diff_lambda, diff_norm_g,
           sg_ln_g, sg_ln_b, sg_w, sg_b, w_out, ln_g, ln_b, w_pe, w_pg):
    batch, seq, d_model = x.shape
    depth = w_in.shape[0]
    t = batch * seq
    alpha = (2 * depth) ** 0.25

    lb_sm = jax.nn.softmax(hgrn_lb.astype(F32), axis=0)
    lower_bounds = jnp.cumsum(lb_sm, axis=0) - lb_sm[0]

    xf = x.reshape(t, d_model)
    xb = xf.astype(BF16)
    for i in range(depth):
        lam_init = 0.8 - 0.6 * math.exp(-0.3 * i)
        dl = diff_lambda[i].astype(F32)
        lam = (jnp.exp(jnp.sum(dl[0] * dl[1])) - jnp.exp(jnp.sum(dl[2] * dl[3])) + lam_init)

        h = _in_proj(xb, w_in[i].astype(BF16))
        ya, yd = _conv_sgu(h, conv_w[i], sg_ln_g[i], sg_ln_b[i], sg_w[i], sg_b[i], seq)
        yb = _hgrn(h, lower_bounds[i], hgrn_norm_g[i], seq)
        yc = _diff_attn(h, lam, lam_init, diff_norm_g[i], batch, seq)
        xf, xb = _out_block(ya, yb, yc, yd, xf, p[i].reshape(t, -1),
                            w_out[i].astype(BF16), w_pg[i].astype(BF16), w_pe[i].astype(BF16),
                            ln_g[i], ln_b[i], alpha)
    return xf.reshape(batch, seq, d_model)
```

```python
import functools
import math

import numpy as np
import jax
import jax.numpy as jnp
from jax import lax
from jax.experimental import pallas as pl
from jax.experimental.pallas import tpu as pltpu

F32 = jnp.float32
BF16 = jnp.bfloat16

GROUP_W = 512
HEAD_W = 128
N_HEADS = GROUP_W // HEAD_W
DA_DQK = 64
F_FLOOR = 1e-30
MASK_VALUE = -1e30
LN_EPS = 1e-5
RMS_EPS = 1e-6

A_B, A_C, A_X, B_Q, B_F, B_I, C_Q, C_K, C_V, D_U, D_V, G_A, G_B, G_C, G_D = range(15)

HG_CHUNK = 128
HG_LEVELS = (2, 4, 8, 16, 32, 64, 128)
SG_CHUNK = 128
VMEM_LIMIT = 56 * 1024 * 1024


def _sigmoid(x):
    return 1.0 / (1.0 + jnp.exp(-x))


def _silu(x):
    return x * _sigmoid(x)


def _gelu_tanh(x):
    c = math.sqrt(2.0 / math.pi)
    return 0.5 * x * (1.0 + jnp.tanh(c * (x + 0.044715 * (x * x * x))))


def _matmul_kernel(x_ref, w_ref, o_ref):
    o_ref[...] = jnp.dot(x_ref[...], w_ref[...].astype(BF16),
                         preferred_element_type=F32).astype(o_ref.dtype)


def _in_proj(xb, w_all, layer, tm=2048, tn=512):
    t, k = xb.shape
    n = w_all.shape[2]
    return pl.pallas_call(
        _matmul_kernel,
        grid=(t // tm, n // tn),
        in_specs=[pl.BlockSpec((tm, k), lambda i, j: (i, 0)),
                  pl.BlockSpec((None, k, tn), lambda i, j: (layer, 0, j))],
        out_specs=pl.BlockSpec((tm, tn), lambda i, j: (i, j)),
        out_shape=jax.ShapeDtypeStruct((t, n), BF16),
        compiler_params=pltpu.CompilerParams(
            dimension_semantics=("arbitrary", "arbitrary"),
            vmem_limit_bytes=VMEM_LIMIT),
        name="in_proj",
    )(xb, w_all)


def _conv_sgu_kernel(ab_ref, ac_ref, ax_ref, ga_ref, du_ref, dv_ref, gd_ref,
                     cw_ref, lng_ref, lnb_ref, ws_ref, bst_ref,
                     ya_ref, yd_ref, carry_ref, *, ts, seq):
    i = pl.program_id(0)

    @pl.when((i * ts) % seq == 0)
    def _():
        carry_ref[...] = jnp.zeros_like(carry_ref)

    z = ac_ref[...].astype(F32) * ax_ref[...].astype(F32)
    rows = lax.broadcasted_iota(jnp.int32, z.shape, 0)
    prev1 = carry_ref[7:8, :]
    prev2 = carry_ref[6:7, :]
    z1 = jnp.where(rows == 0, prev1, pltpu.roll(z, 1, 0))
    z2 = jnp.where(rows == 0, prev2, jnp.where(rows == 1, prev1, pltpu.roll(z, 2, 0)))
    cw = cw_ref[...]
    y = cw[0:1, :] * z2 + cw[1:2, :] * z1 + cw[2:3, :] * z
    ya = ab_ref[...].astype(F32) * y * _silu(ga_ref[...].astype(F32))
    ya_ref[...] = ya.astype(ya_ref.dtype)
    carry_ref[...] = z[ts - 8:ts, :]

    u = _gelu_tanh(du_ref[...].astype(F32))
    v = _gelu_tanh(dv_ref[...].astype(F32))
    mu = jnp.mean(v, axis=-1, keepdims=True)
    vc = v - mu
    var = jnp.mean(vc * vc, axis=-1, keepdims=True)
    vn = (vc * lax.rsqrt(var + LN_EPS) * lng_ref[...] + lnb_ref[...]).astype(BF16)
    gate = _silu(gd_ref[...].astype(F32))
    tri_r = lax.broadcasted_iota(jnp.int32, (SG_CHUNK, SG_CHUNK), 0)
    tri_c = lax.broadcasted_iota(jnp.int32, (SG_CHUNK, SG_CHUNK), 1)
    bst = bst_ref[...]
    for g in range(N_HEADS):
        w = jnp.where(tri_c <= tri_r, ws_ref[g], 0.0).astype(BF16)
        bias = bst[:, g:g + 1]
        lo = g * HEAD_W
        for c in range(ts // SG_CHUNK):
            r0 = c * SG_CHUNK
            sv = jnp.dot(w, vn[r0:r0 + SG_CHUNK, lo:lo + HEAD_W],
                         preferred_element_type=F32) + bias
            yd = u[r0:r0 + SG_CHUNK, lo:lo + HEAD_W] * sv * gate[r0:r0 + SG_CHUNK, lo:lo + HEAD_W]
            yd_ref[r0:r0 + SG_CHUNK, lo:lo + HEAD_W] = yd.astype(yd_ref.dtype)


def _conv_sgu(h, conv_w, sg_ln_g, sg_ln_b, sg_w, sg_b, seq, ts=256):
    t = h.shape[0]

    def hblk(blk):
        return pl.BlockSpec((ts, GROUP_W), lambda i, blk=blk: (i, blk))

    def full(a):
        nd = a.ndim
        return pl.BlockSpec(a.shape, lambda i, nd=nd: (0,) * nd)

    lng = sg_ln_g.reshape(1, GROUP_W)
    lnb = sg_ln_b.reshape(1, GROUP_W)
    bst = sg_b.T
    kern = functools.partial(_conv_sgu_kernel, ts=ts, seq=seq)
    return pl.pallas_call(
        kern,
        grid=(t // ts,),
        in_specs=[hblk(A_B), hblk(A_C), hblk(A_X), hblk(G_A), hblk(D_U), hblk(D_V), hblk(G_D),
                  full(conv_w), full(lng), full(lnb), full(sg_w), full(bst)],
        out_specs=[pl.BlockSpec((ts, GROUP_W), lambda i: (i, 0)),
                   pl.BlockSpec((ts, GROUP_W), lambda i: (i, 0))],
        out_shape=[jax.ShapeDtypeStruct((t, GROUP_W), BF16),
                   jax.ShapeDtypeStruct((t, GROUP_W), BF16)],
        scratch_shapes=[pltpu.VMEM((8, GROUP_W), F32)],
        compiler_params=pltpu.CompilerParams(
            dimension_semantics=("arbitrary",), vmem_limit_bytes=VMEM_LIMIT),
        name="conv_sgu",
    )(h, h, h, h, h, h, h, conv_w, lng, lnb, sg_w, bst)


def _hgrn_constants():
    c = HG_CHUNK
    t = np.arange(c)[:, None]
    s = np.arange(c)[None, :]
    mats = [(s <= t), (s > t)]
    for lv in HG_LEVELS:
        mid = (t // lv) * lv + lv // 2
        qside = t >= mid
        mats.append(np.where(qside, (s >= mid) & (s <= t), (s > t) & (s < mid)))
    rng = np.concatenate(mats, axis=0).astype(np.float32)
    masks = np.stack([t == s] + [(t // lv) == (s // lv) for lv in HG_LEVELS]).astype(np.float32)
    return jnp.asarray(rng, BF16), jnp.asarray(masks, F32)


def _hgrn_kernel(q_ref, f_ref, i_ref, g_ref, lb_ref, ng_ref, rng_ref, msk_ref,
                 y_ref, st_ref, *, seq, nsub):
    c = HG_CHUNK
    step = pl.program_id(0)

    @pl.when((step * nsub * c) % seq == 0)
    def _():
        st_ref[...] = jnp.zeros_like(st_ref)

    fz = f_ref[...].astype(F32)
    lb = lb_ref[...]
    e = jnp.exp(-jnp.abs(fz))
    r = 1.0 / (1.0 + e)
    pos = fz >= 0.0
    sig = jnp.where(pos, r, e * r)
    nsig = jnp.where(pos, e * r, r)
    f = lb + (1.0 - lb) * sig
    lg = jnp.log(jnp.maximum(f, F_FLOOR))
    kk_all = (1.0 - lb) * nsig

    g1 = lg.astype(BF16)
    g2 = (lg - g1.astype(F32)).astype(BF16)
    rng = rng_ref[...]
    exs = [jnp.dot(rng, g1[u * c:(u + 1) * c], preferred_element_type=F32)
           + jnp.dot(rng, g2[u * c:(u + 1) * c], preferred_element_type=F32)
           for u in range(nsub)]

    rows = lax.broadcasted_iota(jnp.int32, (c, HEAD_W), 0)
    for h in range(N_HEADS):
        lo = h * HEAD_W
        st = st_ref[h]
        for u in range(nsub):
            ex = exs[u]
            r0 = u * c
            q = q_ref[r0:r0 + c, lo:lo + HEAD_W].astype(F32)
            vb = i_ref[r0:r0 + c, lo:lo + HEAD_W]
            kk = kk_all[r0:r0 + c, lo:lo + HEAD_W]

            bcum = ex[0:c, lo:lo + HEAD_W]
            q_in = (q * jnp.exp(bcum)).astype(BF16)
            o = lax.dot_general(q_in, st.astype(BF16), (((1,), (1,)), ((), ())),
                                preferred_element_type=F32)
            k_out = (kk * jnp.exp(ex[c:2 * c, lo:lo + HEAD_W])).astype(BF16)
            e_last = jnp.exp(bcum[c - 1:c, :])
            st = e_last * st + jnp.dot(vb.astype(F32).T.astype(BF16), k_out,
                                       preferred_element_type=F32)

            a = msk_ref[0] * lax.dot_general(q.astype(BF16), kk.astype(BF16),
                                             (((1,), (1,)), ((), ())), preferred_element_type=F32)
            for li, lv in enumerate(HG_LEVELS):
                el = jnp.exp(ex[(2 + li) * c:(3 + li) * c, lo:lo + HEAD_W])
                qside = (rows & (lv - 1)) >= (lv // 2)
                ql = jnp.where(qside, q * el, 0.0).astype(BF16)
                kl = jnp.where(qside, 0.0, kk * el).astype(BF16)
                al = lax.dot_general(ql, kl, (((1,), (1,)), ((), ())),
                                     preferred_element_type=F32)
                a = a + msk_ref[1 + li] * al
            o = o + jnp.dot(a.astype(BF16), vb, preferred_element_type=F32)

            ms = jnp.mean(o * o, axis=-1, keepdims=True)
            on = o * lax.rsqrt(ms + RMS_EPS) * ng_ref[:, lo:lo + HEAD_W]
            y = on * _silu(g_ref[r0:r0 + c, lo:lo + HEAD_W].astype(F32))
            y_ref[r0:r0 + c, lo:lo + HEAD_W] = y.astype(y_ref.dtype)
        st_ref[h] = st


def _hgrn(h, lb, norm_g, seq, nsub=2):
    t = h.shape[0]
    c = nsub * HG_CHUNK
    rng, masks = _hgrn_constants()

    def hblk(blk):
        return pl.BlockSpec((c, GROUP_W), lambda i, blk=blk: (i, blk))

    def full(a):
        nd = a.ndim
        return pl.BlockSpec(a.shape, lambda i, nd=nd: (0,) * nd)

    lb2 = lb.reshape(1, GROUP_W).astype(F32)
    ng2 = norm_g.reshape(1, GROUP_W).astype(F32)
    kern = functools.partial(_hgrn_kernel, seq=seq, nsub=nsub)
    return pl.pallas_call(
        kern,
        grid=(t // c,),
        in_specs=[hblk(B_Q), hblk(B_F), hblk(B_I), hblk(G_B),
                  full(lb2), full(ng2), full(rng), full(masks)],
        out_specs=pl.BlockSpec((c, GROUP_W), lambda i: (i, 0)),
        out_shape=jax.ShapeDtypeStruct((t, GROUP_W), BF16),
        scratch_shapes=[pltpu.VMEM((N_HEADS, HEAD_W, HEAD_W), F32)],
        compiler_params=pltpu.CompilerParams(
            dimension_semantics=("arbitrary",), vmem_limit_bytes=VMEM_LIMIT),
        name="hgrn2",
    )(h, h, h, h, lb2, ng2, rng, masks)


ATTN_ROWS = 64


def _diff_attn_kernel(lam_ref, q_ref, k_ref, v_ref, g_ref, ng_ref, o_ref,
                      qq_ref, sa_ref, sb_ref, pa_ref, pb_ref, pt_ref,
                      ma_ref, la_ref, aa_ref, mb_ref, lb_ref, ab_ref, mt_ref, lt_ref, at_ref,
                      acc_ref, *, tq, out_scale):
    i = pl.program_id(2)
    lam = lam_ref[0, 0]
    scale = DA_DQK ** -0.5
    tk = tq
    ncb = tk // HEAD_W
    st_a = (ma_ref, la_ref, aa_ref)
    st_b = (mb_ref, lb_ref, ab_ref)
    st_t = (mt_ref, lt_ref, at_ref)

    q = (q_ref[...].astype(F32) * (scale * math.log2(math.e))).astype(BF16)
    lane = lax.broadcasted_iota(jnp.int32, q.shape, 1)
    zero = jnp.zeros_like(q)
    qq_ref[0:tq, :] = jnp.where(lane < DA_DQK, q, zero)
    qq_ref[tq:2 * tq, :] = jnp.where(lane >= DA_DQK, q, zero)

    def scores(j):
        kb = k_ref[pl.ds(pl.multiple_of(j * tk, tk), tk), :]
        return lax.dot_general(qq_ref[...], kb, (((1,), (1,)), ((), ())),
                               preferred_element_type=F32)

    def accumulate(j, p_ref, st):
        vb = v_ref[pl.ds(pl.multiple_of(j * tk, tk), tk), :]
        acc_ref[...] = st[2][...] * acc_ref[...] + jnp.dot(p_ref[...], vb,
                                                            preferred_element_type=F32)

    def softmax(s_ref, p_ref, st_in, st_out, masked):
        for c in range(2 * tq // ATTN_ROWS):
            r0 = c * ATTN_ROWS
            rs = slice(r0, r0 + ATTN_ROWS)
            row_lo = r0 % tq
            tiles = []
            for cb in range(ncb):
                if masked and cb * HEAD_W > row_lo + ATTN_ROWS - 1:
                    tiles.append(None)
                    continue
                s = s_ref[rs, cb * HEAD_W:(cb + 1) * HEAD_W]
                if masked and (cb + 1) * HEAD_W - 1 > row_lo:
                    r = row_lo + lax.broadcasted_iota(jnp.int32, s.shape, 0)
                    cidx = cb * HEAD_W + lax.broadcasted_iota(jnp.int32, s.shape, 1)
                    s = jnp.where(cidx <= r, s, MASK_VALUE)
                tiles.append(s)
            live = [s for s in tiles if s is not None]
            mx = functools.reduce(jnp.maximum, live)
            m_blk = jnp.broadcast_to(jnp.max(mx, axis=-1, keepdims=True), mx.shape)
            m_old = st_in[0][rs, :]
            m_new = jnp.maximum(m_old, m_blk)
            alpha = jnp.exp2(m_old - m_new)
            psum = None
            for cb, s in enumerate(tiles):
                if s is None:
                    p_ref[rs, cb * HEAD_W:(cb + 1) * HEAD_W] = jnp.zeros((ATTN_ROWS, HEAD_W), BF16)
                    continue
                p = jnp.exp2(s - m_new)
                psum = p if psum is None else psum + p
                p_ref[rs, cb * HEAD_W:(cb + 1) * HEAD_W] = p.astype(BF16)
            st_out[0][rs, :] = m_new
            st_out[1][rs, :] = alpha * st_in[1][rs, :] + psum
            st_out[2][rs, :] = alpha

    acc_ref[...] = jnp.zeros_like(acc_ref)
    mb_ref[...] = jnp.full_like(mb_ref, MASK_VALUE)
    lb_ref[...] = jnp.zeros_like(lb_ref)
    ab_ref[...] = jnp.ones_like(ab_ref)
    pb_ref[...] = jnp.zeros_like(pb_ref)
    sa_ref[...] = scores(0)

    def half_step(j, s_in, s_out, p_in, p_out, st_in, st_out):
        s_out[...] = scores(j + 1)
        accumulate(jnp.maximum(j - 1, 0), p_in, st_in)
        softmax(s_in, p_out, st_in, st_out, masked=False)

    def pair(jj, carry):
        half_step(2 * jj, sa_ref, sb_ref, pb_ref, pa_ref, st_b, st_a)
        half_step(2 * jj + 1, sb_ref, sa_ref, pa_ref, pb_ref, st_a, st_b)
        return carry

    lax.fori_loop(0, i // 2, pair, 0)

    def tail(s_in, p_in, p_out, st_in, st_out):
        accumulate(jnp.maximum(i - 1, 0), p_in, st_in)
        softmax(s_in, p_out, st_in, st_out, masked=True)
        accumulate(i, p_out, st_out)
        l = jnp.sum(st_out[1][...], axis=-1, keepdims=True)
        acc = acc_ref[...] / l
        o = acc[0:tq] - lam * acc[tq:2 * tq]
        ms = jnp.mean(o * o, axis=-1, keepdims=True)
        on = o * lax.rsqrt(ms + RMS_EPS) * ng_ref[...] * out_scale
        o_ref[...] = (on * _silu(g_ref[...].astype(F32))).astype(o_ref.dtype)

    @pl.when(i % 2 == 0)
    def _():
        tail(sa_ref, pb_ref, pt_ref, st_b, st_t)

    @pl.when(i % 2 == 1)
    def _():
        half_step(i - 1, sa_ref, sb_ref, pb_ref, pa_ref, st_b, st_a)
        tail(sb_ref, pa_ref, pt_ref, st_a, st_t)


def _diff_attn(h, lam, lam_init, norm_g, batch, seq, tq=512):
    t = h.shape[0]
    nq = seq // tq
    per = GROUP_W // HEAD_W
    ng2 = norm_g.reshape(1, GROUP_W).astype(F32)
    lam2 = jnp.reshape(lam, (1, 1)).astype(F32)
    kern = functools.partial(_diff_attn_kernel, tq=tq, out_scale=1.0 - lam_init)
    return pl.pallas_call(
        kern,
        grid=(batch, N_HEADS, nq),
        in_specs=[pl.BlockSpec(memory_space=pltpu.SMEM),
                  pl.BlockSpec((tq, HEAD_W), lambda b, hh, i: (b * nq + i, C_Q * per + hh)),
                  pl.BlockSpec((seq, HEAD_W), lambda b, hh, i: (b, C_K * per + hh)),
                  pl.BlockSpec((seq, HEAD_W), lambda b, hh, i: (b, C_V * per + hh)),
                  pl.BlockSpec((tq, HEAD_W), lambda b, hh, i: (b * nq + i, G_C * per + hh)),
                  pl.BlockSpec((1, HEAD_W), lambda b, hh, i: (0, hh))],
        out_specs=pl.BlockSpec((tq, HEAD_W), lambda b, hh, i: (b * nq + i, hh)),
        out_shape=jax.ShapeDtypeStruct((t, GROUP_W), BF16),
        scratch_shapes=[pltpu.VMEM((2 * tq, HEAD_W), BF16),
                        pltpu.VMEM((2 * tq, tq), F32), pltpu.VMEM((2 * tq, tq), F32),
                        *([pltpu.VMEM((2 * tq, tq), BF16)] * 3),
                        *([pltpu.VMEM((2 * tq, HEAD_W), F32)] * 10)],
        compiler_params=pltpu.CompilerParams(
            dimension_semantics=("arbitrary", "arbitrary", "arbitrary"),
            vmem_limit_bytes=VMEM_LIMIT),
        name="diff_attn",
    )(lam2, h, h, h, h, ng2)


def _out_kernel(ya_ref, yb_ref, yc_ref, yd_ref, x_ref, p_ref, wo_ref, wpg_ref, wpe_ref,
                lng_ref, lnb_ref, xo_ref, xb_ref, *, alpha):
    acc = alpha * x_ref[...]
    for g, y_ref in enumerate((ya_ref, yb_ref, yc_ref, yd_ref)):
        acc = acc + jnp.dot(y_ref[...], wo_ref[g * GROUP_W:(g + 1) * GROUP_W, :],
                            preferred_element_type=F32)
    mu = jnp.mean(acc, axis=-1, keepdims=True)
    xc = acc - mu
    var = jnp.mean(xc * xc, axis=-1, keepdims=True)
    xn = xc * lax.rsqrt(var + LN_EPS) * lng_ref[...] + lnb_ref[...]
    z = jnp.dot(xn.astype(BF16), wpg_ref[...], preferred_element_type=F32)
    pe = jnp.dot(p_ref[...].astype(BF16), wpe_ref[...], preferred_element_type=F32)
    out = xn + pe * _sigmoid(z)
    xo_ref[...] = out
    xb_ref[...] = out.astype(BF16)


def _out_block(ya, yb, yc, yd, x, p_all, wo_all, wpg_all, wpe_all, lng_all, lnb_all, layer, alpha,
               tm=256):
    t, d = x.shape
    pdim = p_all.shape[1]
    steps = t // tm

    def rows(w):
        return pl.BlockSpec((tm, w), lambda i: (i, 0))

    def resident(a):
        return pl.BlockSpec((None,) + a.shape[1:], lambda i: (layer, 0, 0),
                            pipeline_mode=pl.Buffered(1))

    kern = functools.partial(_out_kernel, alpha=alpha)
    return pl.pallas_call(
        kern,
        grid=(steps,),
        in_specs=[rows(GROUP_W), rows(GROUP_W), rows(GROUP_W), rows(GROUP_W),
                  rows(d), pl.BlockSpec((tm, pdim), lambda i: (layer * steps + i, 0)),
                  resident(wo_all), resident(wpg_all), resident(wpe_all),
                  resident(lng_all), resident(lnb_all)],
        out_specs=[rows(d), rows(d)],
        out_shape=[jax.ShapeDtypeStruct((t, d), F32), jax.ShapeDtypeStruct((t, d), BF16)],
        compiler_params=pltpu.CompilerParams(
            dimension_semantics=("arbitrary",), vmem_limit_bytes=VMEM_LIMIT),
        name="out_block",
    )(ya, yb, yc, yd, x, p_all, wo_all, wpg_all, wpe_all, lng_all, lnb_all)


def kernel(x, p, w_in, conv_w, hgrn_lb, hgrn_norm_g, diff_lambda, diff_norm_g,
           sg_ln_g, sg_ln_b, sg_w, sg_b, w_out, ln_g, ln_b, w_pe, w_pg):
    batch, seq, d_model = x.shape
    depth = w_in.shape[0]
    t = batch * seq
    alpha = (2 * depth) ** 0.25

    lb_sm = jax.nn.softmax(hgrn_lb.astype(F32), axis=0)
    lower_bounds = jnp.cumsum(lb_sm, axis=0) - lb_sm[0]

    xf = x.reshape(t, d_model)
    xb = xf.astype(BF16)
    p_all = p.reshape(depth * t, p.shape[-1])
    wo_all, wpg_all, wpe_all = w_out.astype(BF16), w_pg.astype(BF16), w_pe.astype(BF16)
    lng_all = ln_g.reshape(depth, 1, d_model).astype(F32)
    lnb_all = ln_b.reshape(depth, 1, d_model).astype(F32)
    for i in range(depth):
        lam_init = 0.8 - 0.6 * math.exp(-0.3 * i)
        dl = diff_lambda[i].astype(F32)
        lam = (jnp.exp(jnp.sum(dl[0] * dl[1])) - jnp.exp(jnp.sum(dl[2] * dl[3])) + lam_init)

        h = _in_proj(xb, w_in, i)
        ya, yd = _conv_sgu(h, conv_w[i], sg_ln_g[i], sg_ln_b[i], sg_w[i], sg_b[i], seq)
        yb = _hgrn(h, lower_bounds[i], hgrn_norm_g[i], seq)
        yc = _diff_attn(h, lam, lam_init, diff_norm_g[i], batch, seq)
        xf, xb = _out_block(ya, yb, yc, yd, xf, p_all, wo_all, wpg_all, wpe_all,
                            lng_all, lnb_all, i, alpha)
    return xf.reshape(batch, seq, d_model)
```

```python
import functools
import math

import numpy as np
import jax
import jax.numpy as jnp
from jax import lax
from jax.experimental import pallas as pl
from jax.experimental.pallas import tpu as pltpu

F32 = jnp.float32
BF16 = jnp.bfloat16

GROUP_W = 512
HEAD_W = 128
N_HEADS = GROUP_W // HEAD_W
DA_DQK = 64
F_FLOOR = 1e-30
MASK_VALUE = -1e30
LN_EPS = 1e-5
RMS_EPS = 1e-6

A_B, A_C, A_X, B_Q, B_F, B_I, C_Q, C_K, C_V, D_U, D_V, G_A, G_B, G_C, G_D = range(15)

HG_CHUNK = 128
HG_LEVELS = (2, 4, 8, 16, 32, 64, 128)
SG_CHUNK = 128
VMEM_LIMIT = 56 * 1024 * 1024


def _sigmoid(x):
    return 1.0 / (1.0 + jnp.exp(-x))


def _silu(x):
    return x * _sigmoid(x)


def _gelu_tanh(x):
    c = math.sqrt(2.0 / math.pi)
    return 0.5 * x * (1.0 + jnp.tanh(c * (x + 0.044715 * (x * x * x))))


def _matmul_kernel(x_ref, w_ref, o_ref):
    o_ref[...] = jnp.dot(x_ref[...], w_ref[...].astype(BF16),
                         preferred_element_type=F32).astype(o_ref.dtype)


def _in_proj(xb, w_all, layer, tm=2048, tn=512):
    t, k = xb.shape
    n = w_all.shape[2]
    return pl.pallas_call(
        _matmul_kernel,
        grid=(t // tm, n // tn),
        in_specs=[pl.BlockSpec((tm, k), lambda i, j: (i, 0)),
                  pl.BlockSpec((None, k, tn), lambda i, j: (layer, 0, j))],
        out_specs=pl.BlockSpec((tm, tn), lambda i, j: (i, j)),
        out_shape=jax.ShapeDtypeStruct((t, n), BF16),
        compiler_params=pltpu.CompilerParams(
            dimension_semantics=("arbitrary", "arbitrary"),
            vmem_limit_bytes=VMEM_LIMIT),
        name="in_proj",
    )(xb, w_all)


def _conv_sgu_kernel(ab_ref, ac_ref, ax_ref, ga_ref, du_ref, dv_ref, gd_ref,
                     cw_ref, lng_ref, lnb_ref, ws_ref, bst_ref,
                     ya_ref, yd_ref, carry_ref, *, ts, seq):
    i = pl.program_id(0)

    @pl.when((i * ts) % seq == 0)
    def _():
        carry_ref[...] = jnp.zeros_like(carry_ref)

    z = ac_ref[...].astype(F32) * ax_ref[...].astype(F32)
    rows = lax.broadcasted_iota(jnp.int32, z.shape, 0)
    prev1 = carry_ref[7:8, :]
    prev2 = carry_ref[6:7, :]
    z1 = jnp.where(rows == 0, prev1, pltpu.roll(z, 1, 0))
    z2 = jnp.where(rows == 0, prev2, jnp.where(rows == 1, prev1, pltpu.roll(z, 2, 0)))
    cw = cw_ref[...]
    y = cw[0:1, :] * z2 + cw[1:2, :] * z1 + cw[2:3, :] * z
    ya = ab_ref[...].astype(F32) * y * _silu(ga_ref[...].astype(F32))
    ya_ref[...] = ya.astype(ya_ref.dtype)
    carry_ref[...] = z[ts - 8:ts, :]

    u = _gelu_tanh(du_ref[...].astype(F32))
    v = _gelu_tanh(dv_ref[...].astype(F32))
    mu = jnp.mean(v, axis=-1, keepdims=True)
    vc = v - mu
    var = jnp.mean(vc * vc, axis=-1, keepdims=True)
    vn = (vc * lax.rsqrt(var + LN_EPS) * lng_ref[...] + lnb_ref[...]).astype(BF16)
    gate = _silu(gd_ref[...].astype(F32))
    tri_r = lax.broadcasted_iota(jnp.int32, (SG_CHUNK, SG_CHUNK), 0)
    tri_c = lax.broadcasted_iota(jnp.int32, (SG_CHUNK, SG_CHUNK), 1)
    bst = bst_ref[...]
    for g in range(N_HEADS):
        w = jnp.where(tri_c <= tri_r, ws_ref[g], 0.0).astype(BF16)
        bias = bst[:, g:g + 1]
        lo = g * HEAD_W
        for c in range(ts // SG_CHUNK):
            r0 = c * SG_CHUNK
            sv = jnp.dot(w, vn[r0:r0 + SG_CHUNK, lo:lo + HEAD_W],
                         preferred_element_type=F32) + bias
            yd = u[r0:r0 + SG_CHUNK, lo:lo + HEAD_W] * sv * gate[r0:r0 + SG_CHUNK, lo:lo + HEAD_W]
            yd_ref[r0:r0 + SG_CHUNK, lo:lo + HEAD_W] = yd.astype(yd_ref.dtype)


def _conv_sgu(h, conv_w, sg_ln_g, sg_ln_b, sg_w, sg_b, seq, ts=256):
    t = h.shape[0]

    def hblk(blk):
        return pl.BlockSpec((ts, GROUP_W), lambda i, blk=blk: (i, blk))

    def full(a):
        nd = a.ndim
        return pl.BlockSpec(a.shape, lambda i, nd=nd: (0,) * nd)

    lng = sg_ln_g.reshape(1, GROUP_W)
    lnb = sg_ln_b.reshape(1, GROUP_W)
    bst = sg_b.T
    kern = functools.partial(_conv_sgu_kernel, ts=ts, seq=seq)
    return pl.pallas_call(
        kern,
        grid=(t // ts,),
        in_specs=[hblk(A_B), hblk(A_C), hblk(A_X), hblk(G_A), hblk(D_U), hblk(D_V), hblk(G_D),
                  full(conv_w), full(lng), full(lnb), full(sg_w), full(bst)],
        out_specs=[pl.BlockSpec((ts, GROUP_W), lambda i: (i, 0)),
                   pl.BlockSpec((ts, GROUP_W), lambda i: (i, 0))],
        out_shape=[jax.ShapeDtypeStruct((t, GROUP_W), BF16),
                   jax.ShapeDtypeStruct((t, GROUP_W), BF16)],
        scratch_shapes=[pltpu.VMEM((8, GROUP_W), F32)],
        compiler_params=pltpu.CompilerParams(
            dimension_semantics=("arbitrary",), vmem_limit_bytes=VMEM_LIMIT),
        name="conv_sgu",
    )(h, h, h, h, h, h, h, conv_w, lng, lnb, sg_w, bst)


def _hgrn_constants():
    c = HG_CHUNK
    t = np.arange(c)[:, None]
    s = np.arange(c)[None, :]
    mats = [(s <= t), (s > t)]
    for lv in HG_LEVELS:
        mid = (t // lv) * lv + lv // 2
        qside = t >= mid
        mats.append(np.where(qside, (s >= mid) & (s <= t), (s > t) & (s < mid)))
    rng = np.concatenate(mats, axis=0).astype(np.float32)
    rng = np.concatenate([rng, rng], axis=1)
    masks = np.stack([t == s] + [(t // lv) == (s // lv) for lv in HG_LEVELS]).astype(np.float32)
    masks = np.concatenate([masks, masks], axis=2)
    return jnp.asarray(rng, BF16), jnp.asarray(masks, F32)


def _hgrn_kernel(q_ref, f_ref, i_ref, g_ref, lb_ref, ng_ref, rng_ref, msk_ref,
                 y_ref, st_ref, *, seq, nsub):
    c = HG_CHUNK
    step = pl.program_id(0)

    @pl.when((step * nsub * c) % seq == 0)
    def _():
        st_ref[...] = jnp.zeros_like(st_ref)

    fz = f_ref[...].astype(F32)
    lb = lb_ref[...]
    e = jnp.exp(-jnp.abs(fz))
    r = 1.0 / (1.0 + e)
    pos = fz >= 0.0
    sig = jnp.where(pos, r, e * r)
    nsig = jnp.where(pos, e * r, r)
    f = lb + (1.0 - lb) * sig
    lg = jnp.log(jnp.maximum(f, F_FLOOR))
    kk_all = (1.0 - lb) * nsig

    g1 = lg.astype(BF16)
    g2 = (lg - g1.astype(F32)).astype(BF16)
    rng = rng_ref[...]
    exs = [jnp.dot(rng, jnp.concatenate([g1[u * c:(u + 1) * c], g2[u * c:(u + 1) * c]], axis=0),
                   preferred_element_type=F32)
           for u in range(nsub)]

    rows = lax.broadcasted_iota(jnp.int32, (c, HEAD_W), 0)
    zblk = jnp.zeros((c, HEAD_W), BF16)
    nt = (((1,), (1,)), ((), ()))

    def pair_rows(x0, x1):
        return jnp.concatenate([jnp.concatenate([x0, zblk], axis=1),
                                jnp.concatenate([zblk, x1], axis=1)], axis=0)

    for hp in range(N_HEADS // 2):
        los = (2 * hp * HEAD_W, (2 * hp + 1) * HEAD_W)
        sts = [st_ref[2 * hp], st_ref[2 * hp + 1]]
        for u in range(nsub):
            ex = exs[u]
            r0 = u * c
            qs, kks, vbs, outs = [], [], [], []
            for n, lo in enumerate(los):
                q = q_ref[r0:r0 + c, lo:lo + HEAD_W].astype(F32)
                vb = i_ref[r0:r0 + c, lo:lo + HEAD_W]
                kk = kk_all[r0:r0 + c, lo:lo + HEAD_W]
                bcum = ex[0:c, lo:lo + HEAD_W]
                q_in = (q * jnp.exp(bcum)).astype(BF16)
                outs.append(lax.dot_general(q_in, sts[n].astype(BF16), nt,
                                            preferred_element_type=F32))
                k_out = (kk * jnp.exp(ex[c:2 * c, lo:lo + HEAD_W])).astype(BF16)
                e_last = jnp.exp(bcum[c - 1:c, :])
                sts[n] = e_last * sts[n] + jnp.dot(vb.astype(F32).T.astype(BF16), k_out,
                                                   preferred_element_type=F32)
                qs.append(q)
                kks.append(kk)
                vbs.append(vb)

            a = msk_ref[0] * lax.dot_general(
                jnp.concatenate([qs[0].astype(BF16), qs[1].astype(BF16)], axis=1),
                pair_rows(kks[0].astype(BF16), kks[1].astype(BF16)), nt,
                preferred_element_type=F32)
            for li, lv in enumerate(HG_LEVELS):
                qside = (rows & (lv - 1)) >= (lv // 2)
                qls, kls = [], []
                for n, lo in enumerate(los):
                    el = jnp.exp(ex[(2 + li) * c:(3 + li) * c, lo:lo + HEAD_W])
                    qls.append(jnp.where(qside, qs[n] * el, 0.0).astype(BF16))
                    kls.append(jnp.where(qside, 0.0, kks[n] * el).astype(BF16))
                al = lax.dot_general(jnp.concatenate(qls, axis=1), pair_rows(kls[0], kls[1]), nt,
                                     preferred_element_type=F32)
                a = a + msk_ref[1 + li] * al
            o2 = jnp.dot(a.astype(BF16), pair_rows(vbs[0], vbs[1]),
                         preferred_element_type=F32)

            for n, lo in enumerate(los):
                o = outs[n] + o2[:, n * HEAD_W:(n + 1) * HEAD_W]
                ms = jnp.mean(o * o, axis=-1, keepdims=True)
                on = o * lax.rsqrt(ms + RMS_EPS) * ng_ref[:, lo:lo + HEAD_W]
                y = on * _silu(g_ref[r0:r0 + c, lo:lo + HEAD_W].astype(F32))
                y_ref[r0:r0 + c, lo:lo + HEAD_W] = y.astype(y_ref.dtype)
        st_ref[2 * hp] = sts[0]
        st_ref[2 * hp + 1] = sts[1]


def _hgrn(h, lb, norm_g, seq, nsub=2):
    t = h.shape[0]
    c = nsub * HG_CHUNK
    rng, masks = _hgrn_constants()

    def hblk(blk):
        return pl.BlockSpec((c, GROUP_W), lambda i, blk=blk: (i, blk))

    def full(a):
        nd = a.ndim
        return pl.BlockSpec(a.shape, lambda i, nd=nd: (0,) * nd)

    lb2 = lb.reshape(1, GROUP_W).astype(F32)
    ng2 = norm_g.reshape(1, GROUP_W).astype(F32)
    kern = functools.partial(_hgrn_kernel, seq=seq, nsub=nsub)
    return pl.pallas_call(
        kern,
        grid=(t // c,),
        in_specs=[hblk(B_Q), hblk(B_F), hblk(B_I), hblk(G_B),
                  full(lb2), full(ng2), full(rng), full(masks)],
        out_specs=pl.BlockSpec((c, GROUP_W), lambda i: (i, 0)),
        out_shape=jax.ShapeDtypeStruct((t, GROUP_W), BF16),
        scratch_shapes=[pltpu.VMEM((N_HEADS, HEAD_W, HEAD_W), F32)],
        compiler_params=pltpu.CompilerParams(
            dimension_semantics=("arbitrary",), vmem_limit_bytes=VMEM_LIMIT),
        name="hgrn2",
    )(h, h, h, h, lb2, ng2, rng, masks)


ATTN_ROWS = 64


def _diff_attn_kernel(lam_ref, q_ref, k_ref, v_ref, g_ref, ng_ref, o_ref,
                      qq_ref, pa_ref, pb_ref, pt_ref,
                      ma_ref, la_ref, aa_ref, mb_ref, lb_ref, ab_ref, mt_ref, lt_ref, at_ref,
                      acc_ref, *, tq, out_scale):
    i = pl.program_id(2)
    lam = lam_ref[0, 0]
    scale = DA_DQK ** -0.5
    tk = tq
    ncb = tk // HEAD_W
    st_a = (ma_ref, la_ref, aa_ref)
    st_b = (mb_ref, lb_ref, ab_ref)
    st_t = (mt_ref, lt_ref, at_ref)

    q = (q_ref[...].astype(F32) * (scale * math.log2(math.e))).astype(BF16)
    lane = lax.broadcasted_iota(jnp.int32, q.shape, 1)
    zero = jnp.zeros_like(q)
    qq_ref[0:tq, :] = jnp.where(lane < DA_DQK, q, zero)
    qq_ref[tq:2 * tq, :] = jnp.where(lane >= DA_DQK, q, zero)

    def scores(j):
        kb = k_ref[pl.ds(pl.multiple_of(j * tk, tk), tk), :]
        return lax.dot_general(qq_ref[...], kb, (((1,), (1,)), ((), ())),
                               preferred_element_type=F32)

    def accumulate(j, p_ref, st):
        vb = v_ref[pl.ds(pl.multiple_of(j * tk, tk), tk), :]
        acc_ref[...] = st[2][...] * acc_ref[...] + jnp.dot(p_ref[...], vb,
                                                            preferred_element_type=F32)

    def softmax(s_all, p_ref, st_in, st_out, masked):
        for c in range(2 * tq // ATTN_ROWS):
            r0 = c * ATTN_ROWS
            rs = slice(r0, r0 + ATTN_ROWS)
            row_lo = r0 % tq
            tiles = []
            for cb in range(ncb):
                if masked and cb * HEAD_W > row_lo + ATTN_ROWS - 1:
                    tiles.append(None)
                    continue
                s = s_all[rs, cb * HEAD_W:(cb + 1) * HEAD_W]
                if masked and (cb + 1) * HEAD_W - 1 > row_lo:
                    r = row_lo + lax.broadcasted_iota(jnp.int32, s.shape, 0)
                    cidx = cb * HEAD_W + lax.broadcasted_iota(jnp.int32, s.shape, 1)
                    s = jnp.where(cidx <= r, s, MASK_VALUE)
                tiles.append(s)
            live = [s for s in tiles if s is not None]
            mx = functools.reduce(jnp.maximum, live)
            m_blk = jnp.broadcast_to(jnp.max(mx, axis=-1, keepdims=True), mx.shape)
            m_old = st_in[0][rs, :]
            m_new = jnp.maximum(m_old, m_blk)
            alpha = jnp.exp2(m_old - m_new)
            psum = None
            for cb, s in enumerate(tiles):
                if s is None:
                    p_ref[rs, cb * HEAD_W:(cb + 1) * HEAD_W] = jnp.zeros((ATTN_ROWS, HEAD_W), BF16)
                    continue
                p = jnp.exp2(s - m_new)
                psum = p if psum is None else psum + p
                p_ref[rs, cb * HEAD_W:(cb + 1) * HEAD_W] = p.astype(BF16)
            st_out[0][rs, :] = m_new
            st_out[1][rs, :] = alpha * st_in[1][rs, :] + psum
            st_out[2][rs, :] = alpha

    acc_ref[...] = jnp.zeros_like(acc_ref)
    mb_ref[...] = jnp.full_like(mb_ref, MASK_VALUE)
    lb_ref[...] = jnp.zeros_like(lb_ref)
    ab_ref[...] = jnp.ones_like(ab_ref)
    pb_ref[...] = jnp.zeros_like(pb_ref)

    def step(j, p_in, p_out, st_in, st_out, masked=False):
        s = scores(j)
        accumulate(jnp.maximum(j - 1, 0), p_in, st_in)
        softmax(s, p_out, st_in, st_out, masked)

    def pair(jj, carry):
        step(2 * jj, pb_ref, pa_ref, st_b, st_a)
        step(2 * jj + 1, pa_ref, pb_ref, st_a, st_b)
        return carry

    lax.fori_loop(0, i // 2, pair, 0)

    def tail(p_in, st_in):
        step(i, p_in, pt_ref, st_in, st_t, masked=True)
        accumulate(i, pt_ref, st_t)
        l = jnp.sum(st_t[1][...], axis=-1, keepdims=True)
        acc = acc_ref[...] / l
        o = acc[0:tq] - lam * acc[tq:2 * tq]
        ms = jnp.mean(o * o, axis=-1, keepdims=True)
        on = o * lax.rsqrt(ms + RMS_EPS) * ng_ref[...] * out_scale
        o_ref[...] = (on * _silu(g_ref[...].astype(F32))).astype(o_ref.dtype)

    @pl.when(i % 2 == 0)
    def _():
        tail(pb_ref, st_b)

    @pl.when(i % 2 == 1)
    def _():
        step(i - 1, pb_ref, pa_ref, st_b, st_a)
        tail(pa_ref, st_a)


def _diff_attn(h, lam, lam_init, norm_g, batch, seq, tq=512):
    t = h.shape[0]
    nq = seq // tq
    per = GROUP_W // HEAD_W
    ng2 = norm_g.reshape(1, GROUP_W).astype(F32)
    lam2 = jnp.reshape(lam, (1, 1)).astype(F32)
    kern = functools.partial(_diff_attn_kernel, tq=tq, out_scale=1.0 - lam_init)
    return pl.pallas_call(
        kern,
        grid=(batch, N_HEADS, nq),
        in_specs=[pl.BlockSpec(memory_space=pltpu.SMEM),
                  pl.BlockSpec((tq, HEAD_W), lambda b, hh, i: (b * nq + i, C_Q * per + hh)),
                  pl.BlockSpec((seq, HEAD_W), lambda b, hh, i: (b, C_K * per + hh)),
                  pl.BlockSpec((seq, HEAD_W), lambda b, hh, i: (b, C_V * per + hh)),
                  pl.BlockSpec((tq, HEAD_W), lambda b, hh, i: (b * nq + i, G_C * per + hh)),
                  pl.BlockSpec((1, HEAD_W), lambda b, hh, i: (0, hh))],
        out_specs=pl.BlockSpec((tq, HEAD_W), lambda b, hh, i: (b * nq + i, hh)),
        out_shape=jax.ShapeDtypeStruct((t, GROUP_W), BF16),
        scratch_shapes=[pltpu.VMEM((2 * tq, HEAD_W), BF16),
                        *([pltpu.VMEM((2 * tq, tq), BF16)] * 3),
                        *([pltpu.VMEM((2 * tq, HEAD_W), F32)] * 10)],
        compiler_params=pltpu.CompilerParams(
            dimension_semantics=("arbitrary", "arbitrary", "arbitrary"),
            vmem_limit_bytes=VMEM_LIMIT),
        name="diff_attn",
    )(lam2, h, h, h, h, ng2)


def _out_kernel(ya_ref, yb_ref, yc_ref, yd_ref, x_ref, p_ref, wo_ref, wpg_ref, wpe_ref,
                lng_ref, lnb_ref, xo_ref, xb_ref, *, alpha):
    acc = alpha * x_ref[...]
    for g, y_ref in enumerate((ya_ref, yb_ref, yc_ref, yd_ref)):
        acc = acc + jnp.dot(y_ref[...], wo_ref[g * GROUP_W:(g + 1) * GROUP_W, :],
                            preferred_element_type=F32)
    mu = jnp.mean(acc, axis=-1, keepdims=True)
    xc = acc - mu
    var = jnp.mean(xc * xc, axis=-1, keepdims=True)
    xn = xc * lax.rsqrt(var + LN_EPS) * lng_ref[...] + lnb_ref[...]
    z = jnp.dot(xn.astype(BF16), wpg_ref[...], preferred_element_type=F32)
    pe = jnp.dot(p_ref[...].astype(BF16), wpe_ref[...], preferred_element_type=F32)
    out = xn + pe * _sigmoid(z)
    xo_ref[...] = out
    xb_ref[...] = out.astype(BF16)


def _out_block(ya, yb, yc, yd, x, p_all, wo_all, wpg_all, wpe_all, lng_all, lnb_all, layer, alpha,
               tm=256):
    t, d = x.shape
    pdim = p_all.shape[1]
    steps = t // tm

    def rows(w):
        return pl.BlockSpec((tm, w), lambda i: (i, 0))

    def resident(a):
        return pl.BlockSpec((None,) + a.shape[1:], lambda i: (layer, 0, 0),
                            pipeline_mode=pl.Buffered(1))

    kern = functools.partial(_out_kernel, alpha=alpha)
    return pl.pallas_call(
        kern,
        grid=(steps,),
        in_specs=[rows(GROUP_W), rows(GROUP_W), rows(GROUP_W), rows(GROUP_W),
                  rows(d), pl.BlockSpec((tm, pdim), lambda i: (layer * steps + i, 0)),
                  resident(wo_all), resident(wpg_all), resident(wpe_all),
                  resident(lng_all), resident(lnb_all)],
        out_specs=[rows(d), rows(d)],
        out_shape=[jax.ShapeDtypeStruct((t, d), F32), jax.ShapeDtypeStruct((t, d), BF16)],
        compiler_params=pltpu.CompilerParams(
            dimension_semantics=("arbitrary",), vmem_limit_bytes=VMEM_LIMIT),
        name="out_block",
    )(ya, yb, yc, yd, x, p_all, wo_all, wpg_all, wpe_all, lng_all, lnb_all)


def kernel(x, p, w_in, conv_w, hgrn_lb, hgrn_norm_g, diff_lambda, diff_norm_g,
           sg_ln_g, sg_ln_b, sg_w, sg_b, w_out, ln_g, ln_b, w_pe, w_pg):
    batch, seq, d_model = x.shape
    depth = w_in.shape[0]
    t = batch * seq
    alpha = (2 * depth) ** 0.25

    lb_sm = jax.nn.softmax(hgrn_lb.astype(F32), axis=0)
    lower_bounds = jnp.cumsum(lb_sm, axis=0) - lb_sm[0]

    xf = x.reshape(t, d_model)
    xb = xf.astype(BF16)
    p_all = p.reshape(depth * t, p.shape[-1])
    wo_all, wpg_all, wpe_all = w_out.astype(BF16), w_pg.astype(BF16), w_pe.astype(BF16)
    lng_all = ln_g.reshape(depth, 1, d_model).astype(F32)
    lnb_all = ln_b.reshape(depth, 1, d_model).astype(F32)
    for i in range(depth):
        lam_init = 0.8 - 0.6 * math.exp(-0.3 * i)
        dl = diff_lambda[i].astype(F32)
        lam = (jnp.exp(jnp.sum(dl[0] * dl[1])) - jnp.exp(jnp.sum(dl[2] * dl[3])) + lam_init)

        h = _in_proj(xb, w_in, i)
        ya, yd = _conv_sgu(h, conv_w[i], sg_ln_g[i], sg_ln_b[i], sg_w[i], sg_b[i], seq)
        yb = _hgrn(h, lower_bounds[i], hgrn_norm_g[i], seq)
        yc = _diff_attn(h, lam, lam_init, diff_norm_g[i], batch, seq)
        xf, xb = _out_block(ya, yb, yc, yd, xf, p_all, wo_all, wpg_all, wpe_all,
                            lng_all, lnb_all, i, alpha)
    return xf.reshape(batch, seq, d_model)
```

```python
import functools
import math

import numpy as np
import jax
import jax.numpy as jnp
from jax import lax
from jax.experimental import pallas as pl
from jax.experimental.pallas import tpu as pltpu

F32 = jnp.float32
BF16 = jnp.bfloat16

GROUP_W = 512
HEAD_W = 128
N_HEADS = GROUP_W // HEAD_W
DA_DQK = 64
F_FLOOR = 1e-30
MASK_VALUE = -1e30
LN_EPS = 1e-5
RMS_EPS = 1e-6

A_B, A_C, A_X, B_Q, B_F, B_I, C_Q, C_K, C_V, D_U, D_V, G_A, G_B, G_C, G_D = range(15)

HG_CHUNK = 128
HG_LEVELS = (2, 4, 8, 16, 32, 64, 128)
SG_CHUNK = 128
VMEM_LIMIT = 56 * 1024 * 1024


def _sigmoid(x):
    return 1.0 / (1.0 + jnp.exp(-x))


def _silu(x):
    return x * _sigmoid(x)


def _gelu_tanh(x):
    c = math.sqrt(2.0 / math.pi)
    return 0.5 * x * (1.0 + jnp.tanh(c * (x + 0.044715 * (x * x * x))))


def _matmul_kernel(x_ref, w_ref, o_ref):
    o_ref[...] = jnp.dot(x_ref[...], w_ref[...].astype(BF16),
                         preferred_element_type=F32).astype(o_ref.dtype)


def _in_proj(xb, w_all, layer, tm=2048, tn=512):
    t, k = xb.shape
    n = w_all.shape[2]
    return pl.pallas_call(
        _matmul_kernel,
        grid=(t // tm, n // tn),
        in_specs=[pl.BlockSpec((tm, k), lambda i, j: (i, 0)),
                  pl.BlockSpec((None, k, tn), lambda i, j: (layer, 0, j))],
        out_specs=pl.BlockSpec((tm, tn), lambda i, j: (i, j)),
        out_shape=jax.ShapeDtypeStruct((t, n), BF16),
        compiler_params=pltpu.CompilerParams(
            dimension_semantics=("arbitrary", "arbitrary"),
            vmem_limit_bytes=VMEM_LIMIT),
        name="in_proj",
    )(xb, w_all)


def _conv_sgu_body(ab_ref, ac_ref, ax_ref, ga_ref, du_ref, dv_ref, gd_ref,
                   cw_ref, lng_ref, lnb_ref, ws_ref, bst_ref,
                   ya_ref, yd_ref, carry_ref, *, ts):
    z = ac_ref[...].astype(F32) * ax_ref[...].astype(F32)
    rows = lax.broadcasted_iota(jnp.int32, z.shape, 0)
    prev1 = carry_ref[7:8, :]
    prev2 = carry_ref[6:7, :]
    z1 = jnp.where(rows == 0, prev1, pltpu.roll(z, 1, 0))
    z2 = jnp.where(rows == 0, prev2, jnp.where(rows == 1, prev1, pltpu.roll(z, 2, 0)))
    cw = cw_ref[...]
    y = cw[0:1, :] * z2 + cw[1:2, :] * z1 + cw[2:3, :] * z
    ya = ab_ref[...].astype(F32) * y * _silu(ga_ref[...].astype(F32))
    ya_ref[...] = ya.astype(ya_ref.dtype)
    carry_ref[...] = z[ts - 8:ts, :]
    yield

    u = _gelu_tanh(du_ref[...].astype(F32))
    v = _gelu_tanh(dv_ref[...].astype(F32))
    mu = jnp.mean(v, axis=-1, keepdims=True)
    vc = v - mu
    var = jnp.mean(vc * vc, axis=-1, keepdims=True)
    vn = (vc * lax.rsqrt(var + LN_EPS) * lng_ref[...] + lnb_ref[...]).astype(BF16)
    gate = _silu(gd_ref[...].astype(F32))
    yield
    tri_r = lax.broadcasted_iota(jnp.int32, (SG_CHUNK, SG_CHUNK), 0)
    tri_c = lax.broadcasted_iota(jnp.int32, (SG_CHUNK, SG_CHUNK), 1)
    bst = bst_ref[...]
    for g in range(N_HEADS):
        w = jnp.where(tri_c <= tri_r, ws_ref[g], 0.0).astype(BF16)
        bias = bst[:, g:g + 1]
        lo = g * HEAD_W
        for c in range(ts // SG_CHUNK):
            r0 = c * SG_CHUNK
            sv = jnp.dot(w, vn[r0:r0 + SG_CHUNK, lo:lo + HEAD_W],
                         preferred_element_type=F32) + bias
            yd = u[r0:r0 + SG_CHUNK, lo:lo + HEAD_W] * sv * gate[r0:r0 + SG_CHUNK, lo:lo + HEAD_W]
            yd_ref[r0:r0 + SG_CHUNK, lo:lo + HEAD_W] = yd.astype(yd_ref.dtype)
        if g % 2 == 1:
            yield


def _hgrn_constants():
    c = HG_CHUNK
    t = np.arange(c)[:, None]
    s = np.arange(c)[None, :]
    mats = [(s <= t), (s > t)]
    for lv in HG_LEVELS:
        mid = (t // lv) * lv + lv // 2
        qside = t >= mid
        mats.append(np.where(qside, (s >= mid) & (s <= t), (s > t) & (s < mid)))
    rng = np.concatenate(mats, axis=0).astype(np.float32)
    rng = np.concatenate([rng, rng], axis=1)
    masks = np.stack([t == s] + [(t // lv) == (s // lv) for lv in HG_LEVELS]).astype(np.float32)
    masks = np.concatenate([masks, masks], axis=2)
    return jnp.asarray(rng, BF16), jnp.asarray(masks, F32)


def _hgrn_body(q_ref, f_ref, i_ref, g_ref, lb_ref, ng_ref, rng_ref, msk_ref,
               y_ref, st_ref, *, nsub):
    c = HG_CHUNK
    fz = f_ref[...].astype(F32)
    lb = lb_ref[...]
    e = jnp.exp(-jnp.abs(fz))
    r = 1.0 / (1.0 + e)
    pos = fz >= 0.0
    sig = jnp.where(pos, r, e * r)
    nsig = jnp.where(pos, e * r, r)
    f = lb + (1.0 - lb) * sig
    lg = jnp.log(jnp.maximum(f, F_FLOOR))
    kk_all = (1.0 - lb) * nsig

    g1 = lg.astype(BF16)
    g2 = (lg - g1.astype(F32)).astype(BF16)
    rng = rng_ref[...]
    exs = [jnp.dot(rng, jnp.concatenate([g1[u * c:(u + 1) * c], g2[u * c:(u + 1) * c]], axis=0),
                   preferred_element_type=F32)
           for u in range(nsub)]
    yield

    rows = lax.broadcasted_iota(jnp.int32, (c, HEAD_W), 0)
    zblk = jnp.zeros((c, HEAD_W), BF16)
    nt = (((1,), (1,)), ((), ()))

    def pair_rows(x0, x1):
        return jnp.concatenate([jnp.concatenate([x0, zblk], axis=1),
                                jnp.concatenate([zblk, x1], axis=1)], axis=0)

    for hp in range(N_HEADS // 2):
        los = (2 * hp * HEAD_W, (2 * hp + 1) * HEAD_W)
        sts = [st_ref[2 * hp], st_ref[2 * hp + 1]]
        for u in range(nsub):
            ex = exs[u]
            r0 = u * c
            qs, kks, vbs, outs = [], [], [], []
            for n, lo in enumerate(los):
                q = q_ref[r0:r0 + c, lo:lo + HEAD_W].astype(F32)
                vb = i_ref[r0:r0 + c, lo:lo + HEAD_W]
                kk = kk_all[r0:r0 + c, lo:lo + HEAD_W]
                bcum = ex[0:c, lo:lo + HEAD_W]
                q_in = (q * jnp.exp(bcum)).astype(BF16)
                outs.append(lax.dot_general(q_in, sts[n].astype(BF16), nt,
                                            preferred_element_type=F32))
                k_out = (kk * jnp.exp(ex[c:2 * c, lo:lo + HEAD_W])).astype(BF16)
                e_last = jnp.exp(bcum[c - 1:c, :])
                sts[n] = e_last * sts[n] + jnp.dot(vb.astype(F32).T.astype(BF16), k_out,
                                                   preferred_element_type=F32)
                qs.append(q)
                kks.append(kk)
                vbs.append(vb)
            yield

            a = msk_ref[0] * lax.dot_general(
                jnp.concatenate([qs[0].astype(BF16), qs[1].astype(BF16)], axis=1),
                pair_rows(kks[0].astype(BF16), kks[1].astype(BF16)), nt,
                preferred_element_type=F32)
            for li, lv in enumerate(HG_LEVELS):
                qside = (rows & (lv - 1)) >= (lv // 2)
                qls, kls = [], []
                for n, lo in enumerate(los):
                    el = jnp.exp(ex[(2 + li) * c:(3 + li) * c, lo:lo + HEAD_W])
                    qls.append(jnp.where(qside, qs[n] * el, 0.0).astype(BF16))
                    kls.append(jnp.where(qside, 0.0, kks[n] * el).astype(BF16))
                al = lax.dot_general(jnp.concatenate(qls, axis=1), pair_rows(kls[0], kls[1]), nt,
                                     preferred_element_type=F32)
                a = a + msk_ref[1 + li] * al
                if li % 2 == 0:
                    yield
            o2 = jnp.dot(a.astype(BF16), pair_rows(vbs[0], vbs[1]),
                         preferred_element_type=F32)

            for n, lo in enumerate(los):
                o = outs[n] + o2[:, n * HEAD_W:(n + 1) * HEAD_W]
                ms = jnp.mean(o * o, axis=-1, keepdims=True)
                on = o * lax.rsqrt(ms + RMS_EPS) * ng_ref[:, lo:lo + HEAD_W]
                y = on * _silu(g_ref[r0:r0 + c, lo:lo + HEAD_W].astype(F32))
                y_ref[r0:r0 + c, lo:lo + HEAD_W] = y.astype(y_ref.dtype)
            yield
        st_ref[2 * hp] = sts[0]
        st_ref[2 * hp + 1] = sts[1]


ATTN_ROWS = 64


def _diff_attn_kernel(lam_ref, q_ref, k_ref, v_ref, g_ref, ng_ref, o_ref,
                      qq_ref, pa_ref, pb_ref, pt_ref,
                      ma_ref, la_ref, aa_ref, mb_ref, lb_ref, ab_ref, mt_ref, lt_ref, at_ref,
                      acc_ref, *, tq, out_scale):
    i = pl.program_id(2)
    lam = lam_ref[0, 0]
    scale = DA_DQK ** -0.5
    tk = tq
    ncb = tk // HEAD_W
    st_a = (ma_ref, la_ref, aa_ref)
    st_b = (mb_ref, lb_ref, ab_ref)
    st_t = (mt_ref, lt_ref, at_ref)

    q = (q_ref[...].astype(F32) * (scale * math.log2(math.e))).astype(BF16)
    lane = lax.broadcasted_iota(jnp.int32, q.shape, 1)
    zero = jnp.zeros_like(q)
    qq_ref[0:tq, :] = jnp.where(lane < DA_DQK, q, zero)
    qq_ref[tq:2 * tq, :] = jnp.where(lane >= DA_DQK, q, zero)

    def scores(j):
        kb = k_ref[pl.ds(pl.multiple_of(j * tk, tk), tk), :]
        return lax.dot_general(qq_ref[...], kb, (((1,), (1,)), ((), ())),
                               preferred_element_type=F32)

    def accumulate(j, p_ref, st):
        vb = v_ref[pl.ds(pl.multiple_of(j * tk, tk), tk), :]
        acc_ref[...] = st[2][...] * acc_ref[...] + jnp.dot(p_ref[...], vb,
                                                            preferred_element_type=F32)

    def softmax(s_all, p_ref, st_in, st_out, masked):
        for c in range(2 * tq // ATTN_ROWS):
            r0 = c * ATTN_ROWS
            rs = slice(r0, r0 + ATTN_ROWS)
            row_lo = r0 % tq
            tiles = []
            for cb in range(ncb):
                if masked and cb * HEAD_W > row_lo + ATTN_ROWS - 1:
                    tiles.append(None)
                    continue
                s = s_all[rs, cb * HEAD_W:(cb + 1) * HEAD_W]
                if masked and (cb + 1) * HEAD_W - 1 > row_lo:
                    r = row_lo + lax.broadcasted_iota(jnp.int32, s.shape, 0)
                    cidx = cb * HEAD_W + lax.broadcasted_iota(jnp.int32, s.shape, 1)
                    s = jnp.where(cidx <= r, s, MASK_VALUE)
                tiles.append(s)
            live = [s for s in tiles if s is not None]
            mx = functools.reduce(jnp.maximum, live)
            m_blk = jnp.broadcast_to(jnp.max(mx, axis=-1, keepdims=True), mx.shape)
            m_old = st_in[0][rs, :]
            m_new = jnp.maximum(m_old, m_blk)
            alpha = jnp.exp2(m_old - m_new)
            psum = None
            for cb, s in enumerate(tiles):
                if s is None:
                    p_ref[rs, cb * HEAD_W:(cb + 1) * HEAD_W] = jnp.zeros((ATTN_ROWS, HEAD_W), BF16)
                    continue
                p = jnp.exp2(s - m_new)
                psum = p if psum is None else psum + p
                p_ref[rs, cb * HEAD_W:(cb + 1) * HEAD_W] = p.astype(BF16)
            st_out[0][rs, :] = m_new
            st_out[1][rs, :] = alpha * st_in[1][rs, :] + psum
            st_out[2][rs, :] = alpha

    acc_ref[...] = jnp.zeros_like(acc_ref)
    mb_ref[...] = jnp.full_like(mb_ref, MASK_VALUE)
    lb_ref[...] = jnp.zeros_like(lb_ref)
    ab_ref[...] = jnp.ones_like(ab_ref)
    pb_ref[...] = jnp.zeros_like(pb_ref)

    def step(j, p_in, p_out, st_in, st_out, masked=False):
        s = scores(j)
        accumulate(jnp.maximum(j - 1, 0), p_in, st_in)
        softmax(s, p_out, st_in, st_out, masked)

    def pair(jj, carry):
        step(2 * jj, pb_ref, pa_ref, st_b, st_a)
        step(2 * jj + 1, pa_ref, pb_ref, st_a, st_b)
        return carry

    lax.fori_loop(0, i // 2, pair, 0)

    def tail(p_in, st_in):
        step(i, p_in, pt_ref, st_in, st_t, masked=True)
        accumulate(i, pt_ref, st_t)
        l = jnp.sum(st_t[1][...], axis=-1, keepdims=True)
        acc = acc_ref[...] / l
        o = acc[0:tq] - lam * acc[tq:2 * tq]
        ms = jnp.mean(o * o, axis=-1, keepdims=True)
        on = o * lax.rsqrt(ms + RMS_EPS) * ng_ref[...] * out_scale
        o_ref[...] = (on * _silu(g_ref[...].astype(F32))).astype(o_ref.dtype)

    @pl.when(i % 2 == 0)
    def _():
        tail(pb_ref, st_b)

    @pl.when(i % 2 == 1)
    def _():
        step(i - 1, pb_ref, pa_ref, st_b, st_a)
        tail(pa_ref, st_a)


def _diff_attn(h, lam, lam_init, norm_g, batch, seq, tq=512):
    t = h.shape[0]
    nq = seq // tq
    per = GROUP_W // HEAD_W
    ng2 = norm_g.reshape(1, GROUP_W).astype(F32)
    lam2 = jnp.reshape(lam, (1, 1)).astype(F32)
    kern = functools.partial(_diff_attn_kernel, tq=tq, out_scale=1.0 - lam_init)
    return pl.pallas_call(
        kern,
        grid=(batch, N_HEADS, nq),
        in_specs=[pl.BlockSpec(memory_space=pltpu.SMEM),
                  pl.BlockSpec((tq, HEAD_W), lambda b, hh, i: (b * nq + i, C_Q * per + hh)),
                  pl.BlockSpec((seq, HEAD_W), lambda b, hh, i: (b, C_K * per + hh)),
                  pl.BlockSpec((seq, HEAD_W), lambda b, hh, i: (b, C_V * per + hh)),
                  pl.BlockSpec((tq, HEAD_W), lambda b, hh, i: (b * nq + i, G_C * per + hh)),
                  pl.BlockSpec((1, HEAD_W), lambda b, hh, i: (0, hh))],
        out_specs=pl.BlockSpec((tq, HEAD_W), lambda b, hh, i: (b * nq + i, hh)),
        out_shape=jax.ShapeDtypeStruct((t, GROUP_W), BF16),
        scratch_shapes=[pltpu.VMEM((2 * tq, HEAD_W), BF16),
                        *([pltpu.VMEM((2 * tq, tq), BF16)] * 3),
                        *([pltpu.VMEM((2 * tq, HEAD_W), F32)] * 10)],
        compiler_params=pltpu.CompilerParams(
            dimension_semantics=("arbitrary", "arbitrary", "arbitrary"),
            vmem_limit_bytes=VMEM_LIMIT),
        name="diff_attn",
    )(lam2, h, h, h, h, ng2)


TAIL_ROWS = 256


def _tail_kernel(ab_ref, ac_ref, ax_ref, ga_ref, du_ref, dv_ref, gd_ref,
                 bq_ref, bf_ref, bi_ref, gb_ref, yc_ref, x_ref, p_ref,
                 cw_ref, sglng_ref, sglnb_ref, ws_ref, bst_ref,
                 hlb_ref, hng_ref, rng_ref, msk_ref,
                 wo_ref, wpg_ref, wpe_ref, lng_ref, lnb_ref,
                 xo_ref, xb_ref,
                 ya_ref, yb_ref, yd_ref, carry_ref, st_ref, *, seq, alpha):
    n = pl.program_id(0)

    @pl.when(n == 0)
    def _():
        ya_ref[...] = jnp.zeros_like(ya_ref)
        yb_ref[...] = jnp.zeros_like(yb_ref)
        yd_ref[...] = jnp.zeros_like(yd_ref)

    @pl.when((n * TAIL_ROWS) % seq == 0)
    def _():
        carry_ref[...] = jnp.zeros_like(carry_ref)
        st_ref[...] = jnp.zeros_like(st_ref)

    wslot = n % 2
    rslot = 1 - wslot

    def mixers():
        yield from _conv_sgu_body(ab_ref, ac_ref, ax_ref, ga_ref, du_ref, dv_ref, gd_ref,
                                  cw_ref, sglng_ref, sglnb_ref, ws_ref, bst_ref,
                                  ya_ref.at[wslot], yd_ref.at[wslot], carry_ref, ts=TAIL_ROWS)
        yield from _hgrn_body(bq_ref, bf_ref, bi_ref, gb_ref, hlb_ref, hng_ref, rng_ref, msk_ref,
                              yb_ref.at[wslot], st_ref, nsub=TAIL_ROWS // HG_CHUNK)

    def projections():
        d = x_ref.shape[-1]
        acc = alpha * x_ref[...]
        for y, g in ((yc_ref, 2), (ya_ref.at[rslot], 0), (yd_ref.at[rslot], 3),
                     (yb_ref.at[rslot], 1)):
            acc = acc + jnp.dot(y[...], wo_ref[g * GROUP_W:(g + 1) * GROUP_W, :],
                                preferred_element_type=F32)
            yield
        pe = jnp.dot(p_ref[...].astype(BF16), wpe_ref[...], preferred_element_type=F32)
        mu = jnp.mean(acc, axis=-1, keepdims=True)
        xc = acc - mu
        var = jnp.mean(xc * xc, axis=-1, keepdims=True)
        xn = xc * lax.rsqrt(var + LN_EPS) * lng_ref[...] + lnb_ref[...]
        xnb = xn.astype(BF16)
        yield
        for n0 in range(0, d, GROUP_W):
            cols = slice(n0, n0 + GROUP_W)
            z = jnp.dot(xnb, wpg_ref[:, cols], preferred_element_type=F32)
            out = xn[:, cols] + pe[:, cols] * _sigmoid(z)
            xo_ref[:, cols] = out
            xb_ref[:, cols] = out.astype(BF16)
            yield

    major, minor = mixers(), projections()
    live_major = live_minor = True
    while live_major or live_minor:
        for _ in range(2):
            live_major = live_major and next(major, False) is not False
        live_minor = live_minor and next(minor, False) is not False


def _tail_block(h, yc, x, p_all, small, wo_all, wpg_all, wpe_all, lng_all, lnb_all,
                layer, seq, alpha):
    t, d = x.shape
    tm = TAIL_ROWS
    pdim = p_all.shape[1]
    steps = t // tm

    def hblk(blk):
        return pl.BlockSpec((tm, GROUP_W), lambda i, blk=blk: (jnp.minimum(i, steps - 1), blk))

    def rows(w):
        return pl.BlockSpec((tm, w), lambda i: (jnp.maximum(i - 1, 0), 0))

    def full(a):
        nd = a.ndim
        return pl.BlockSpec(a.shape, lambda i, nd=nd: (0,) * nd)

    def resident(a):
        return pl.BlockSpec((None,) + a.shape[1:], lambda i: (layer, 0, 0),
                            pipeline_mode=pl.Buffered(1))

    h_blocks = (A_B, A_C, A_X, G_A, D_U, D_V, G_D, B_Q, B_F, B_I, G_B)
    kern = functools.partial(_tail_kernel, seq=seq, alpha=alpha)
    return pl.pallas_call(
        kern,
        grid=(steps + 1,),
        in_specs=[hblk(b) for b in h_blocks]
                 + [rows(GROUP_W), rows(d),
                    pl.BlockSpec((tm, pdim),
                                 lambda i: (layer * steps + jnp.maximum(i - 1, 0), 0))]
                 + [full(a) for a in small]
                 + [resident(wo_all), resident(wpg_all), resident(wpe_all),
                    resident(lng_all), resident(lnb_all)],
        out_specs=[rows(d), rows(d)],
        out_shape=[jax.ShapeDtypeStruct((t, d), F32), jax.ShapeDtypeStruct((t, d), BF16)],
        scratch_shapes=[pltpu.VMEM((2, tm, GROUP_W), BF16), pltpu.VMEM((2, tm, GROUP_W), BF16),
                        pltpu.VMEM((2, tm, GROUP_W), BF16),
                        pltpu.VMEM((8, GROUP_W), F32),
                        pltpu.VMEM((N_HEADS, HEAD_W, HEAD_W), F32)],
        compiler_params=pltpu.CompilerParams(
            dimension_semantics=("arbitrary",), vmem_limit_bytes=VMEM_LIMIT),
        name="mix_out",
    )(*([h] * len(h_blocks)), yc, x, p_all, *small,
      wo_all, wpg_all, wpe_all, lng_all, lnb_all)


def kernel(x, p, w_in, conv_w, hgrn_lb, hgrn_norm_g, diff_lambda, diff_norm_g,
           sg_ln_g, sg_ln_b, sg_w, sg_b, w_out, ln_g, ln_b, w_pe, w_pg):
    batch, seq, d_model = x.shape
    depth = w_in.shape[0]
    t = batch * seq
    alpha = (2 * depth) ** 0.25

    lb_sm = jax.nn.softmax(hgrn_lb.astype(F32), axis=0)
    lower_bounds = jnp.cumsum(lb_sm, axis=0) - lb_sm[0]

    xf = x.reshape(t, d_model)
    xb = xf.astype(BF16)
    p_all = p.reshape(depth * t, p.shape[-1])
    wo_all, wpg_all, wpe_all = w_out.astype(BF16), w_pg.astype(BF16), w_pe.astype(BF16)
    lng_all = ln_g.reshape(depth, 1, d_model).astype(F32)
    lnb_all = ln_b.reshape(depth, 1, d_model).astype(F32)
    rng, masks = _hgrn_constants()
    for i in range(depth):
        lam_init = 0.8 - 0.6 * math.exp(-0.3 * i)
        dl = diff_lambda[i].astype(F32)
        lam = (jnp.exp(jnp.sum(dl[0] * dl[1])) - jnp.exp(jnp.sum(dl[2] * dl[3])) + lam_init)

        h = _in_proj(xb, w_in, i)
        yc = _diff_attn(h, lam, lam_init, diff_norm_g[i], batch, seq)
        small = (conv_w[i].astype(F32),
                 sg_ln_g[i].reshape(1, GROUP_W).astype(F32), sg_ln_b[i].reshape(1, GROUP_W).astype(F32),
                 sg_w[i].astype(F32), sg_b[i].T.astype(F32),
                 lower_bounds[i].reshape(1, GROUP_W), hgrn_norm_g[i].reshape(1, GROUP_W).astype(F32),
                 rng, masks)
        xf, xb = _tail_block(h, yc, xf, p_all, small, wo_all, wpg_all, wpe_all,
                             lng_all, lnb_all, i, seq, alpha)
    return xf.reshape(batch, seq, d_model)
```

```python
import functools
import math

import numpy as np
import jax
import jax.numpy as jnp
from jax import lax
from jax.experimental import pallas as pl
from jax.experimental.pallas import tpu as pltpu

F32 = jnp.float32
BF16 = jnp.bfloat16

GROUP_W = 512
HEAD_W = 128
N_HEADS = GROUP_W // HEAD_W
DA_DQK = 64
F_FLOOR = 1e-30
MASK_VALUE = -1e30
LN_EPS = 1e-5
RMS_EPS = 1e-6

A_B, A_C, A_X, B_Q, B_F, B_I, C_Q, C_K, C_V, D_U, D_V, G_A, G_B, G_C, G_D = range(15)

HG_CHUNK = 128
HG_LEVELS = (2, 4, 8, 16, 32, 64, 128)
SG_CHUNK = 128
VMEM_LIMIT = 56 * 1024 * 1024


def _sigmoid(x):
    return 1.0 / (1.0 + jnp.exp(-x))


def _silu(x):
    return x * _sigmoid(x)


def _gelu_tanh(x):
    c = math.sqrt(2.0 / math.pi)
    return 0.5 * x * (1.0 + jnp.tanh(c * (x + 0.044715 * (x * x * x))))


def _matmul_kernel(x_ref, w_ref, o_ref):
    o_ref[...] = jnp.dot(x_ref[...].astype(BF16), w_ref[...].astype(BF16),
                         preferred_element_type=F32).astype(o_ref.dtype)


def _in_proj(xb, w_all, layer, tm=2048, tn=512):
    t, k = xb.shape
    n = w_all.shape[2]
    return pl.pallas_call(
        _matmul_kernel,
        grid=(t // tm, n // tn),
        in_specs=[pl.BlockSpec((tm, k), lambda i, j: (i, 0)),
                  pl.BlockSpec((None, k, tn), lambda i, j: (layer, 0, j))],
        out_specs=pl.BlockSpec((tm, tn), lambda i, j: (i, j)),
        out_shape=jax.ShapeDtypeStruct((t, n), BF16),
        compiler_params=pltpu.CompilerParams(
            dimension_semantics=("arbitrary", "arbitrary"),
            vmem_limit_bytes=VMEM_LIMIT),
        name="in_proj",
    )(xb, w_all)


def _conv_sgu_body(ab_ref, ac_ref, ax_ref, ga_ref, du_ref, dv_ref, gd_ref,
                   cw_ref, lng_ref, lnb_ref, ws_ref, bst_ref,
                   ya_ref, yd_ref, carry_ref, *, ts):
    z = ac_ref[...].astype(F32) * ax_ref[...].astype(F32)
    rows = lax.broadcasted_iota(jnp.int32, z.shape, 0)
    prev1 = carry_ref[7:8, :]
    prev2 = carry_ref[6:7, :]
    z1 = jnp.where(rows == 0, prev1, pltpu.roll(z, 1, 0))
    z2 = jnp.where(rows == 0, prev2, jnp.where(rows == 1, prev1, pltpu.roll(z, 2, 0)))
    cw = cw_ref[...]
    y = cw[0:1, :] * z2 + cw[1:2, :] * z1 + cw[2:3, :] * z
    ya = ab_ref[...].astype(F32) * y * _silu(ga_ref[...].astype(F32))
    ya_ref[...] = ya.astype(ya_ref.dtype)
    carry_ref[...] = z[ts - 8:ts, :]
    yield

    u = _gelu_tanh(du_ref[...].astype(F32))
    v = _gelu_tanh(dv_ref[...].astype(F32))
    mu = jnp.mean(v, axis=-1, keepdims=True)
    vc = v - mu
    var = jnp.mean(vc * vc, axis=-1, keepdims=True)
    vn = (vc * lax.rsqrt(var + LN_EPS) * lng_ref[...] + lnb_ref[...]).astype(BF16)
    gate = _silu(gd_ref[...].astype(F32))
    yield
    tri_r = lax.broadcasted_iota(jnp.int32, (SG_CHUNK, SG_CHUNK), 0)
    tri_c = lax.broadcasted_iota(jnp.int32, (SG_CHUNK, SG_CHUNK), 1)
    bst = bst_ref[...]
    for g in range(N_HEADS):
        w = jnp.where(tri_c <= tri_r, ws_ref[g], 0.0).astype(BF16)
        bias = bst[:, g:g + 1]
        lo = g * HEAD_W
        for c in range(ts // SG_CHUNK):
            r0 = c * SG_CHUNK
            sv = jnp.dot(w, vn[r0:r0 + SG_CHUNK, lo:lo + HEAD_W],
                         preferred_element_type=F32) + bias
            yd = u[r0:r0 + SG_CHUNK, lo:lo + HEAD_W] * sv * gate[r0:r0 + SG_CHUNK, lo:lo + HEAD_W]
            yd_ref[r0:r0 + SG_CHUNK, lo:lo + HEAD_W] = yd.astype(yd_ref.dtype)
        if g % 2 == 1:
            yield


def _hgrn_constants():
    c = HG_CHUNK
    t = np.arange(c)[:, None]
    s = np.arange(c)[None, :]
    mats = [(s <= t), (s > t)]
    for lv in HG_LEVELS:
        mid = (t // lv) * lv + lv // 2
        qside = t >= mid
        mats.append(np.where(qside, (s >= mid) & (s <= t), (s > t) & (s < mid)))
    rng = np.concatenate(mats, axis=0).astype(np.float32)
    rng = np.concatenate([rng, rng], axis=1)
    masks = np.stack([t == s] + [(t // lv) == (s // lv) for lv in HG_LEVELS]).astype(np.float32)
    masks = np.concatenate([masks, masks], axis=2)
    return jnp.asarray(rng, BF16), jnp.asarray(masks, F32)


def _hgrn_body(q_ref, f_ref, i_ref, g_ref, lb_ref, ng_ref, rng_ref, msk_ref,
               y_ref, st_ref, *, nsub):
    c = HG_CHUNK
    fz = f_ref[...].astype(F32)
    lb = lb_ref[...]
    e = jnp.exp(-jnp.abs(fz))
    r = 1.0 / (1.0 + e)
    pos = fz >= 0.0
    sig = jnp.where(pos, r, e * r)
    nsig = jnp.where(pos, e * r, r)
    f = lb + (1.0 - lb) * sig
    lg = jnp.log(jnp.maximum(f, F_FLOOR))
    kk_all = (1.0 - lb) * nsig

    g1 = lg.astype(BF16)
    g2 = (lg - g1.astype(F32)).astype(BF16)
    rng = rng_ref[...]
    exs = [jnp.dot(rng, jnp.concatenate([g1[u * c:(u + 1) * c], g2[u * c:(u + 1) * c]], axis=0),
                   preferred_element_type=F32)
           for u in range(nsub)]
    yield

    rows = lax.broadcasted_iota(jnp.int32, (c, HEAD_W), 0)
    zblk = jnp.zeros((c, HEAD_W), BF16)
    nt = (((1,), (1,)), ((), ()))

    def pair_rows(x0, x1):
        return jnp.concatenate([jnp.concatenate([x0, zblk], axis=1),
                                jnp.concatenate([zblk, x1], axis=1)], axis=0)

    for hp in range(N_HEADS // 2):
        los = (2 * hp * HEAD_W, (2 * hp + 1) * HEAD_W)
        sts = [st_ref[2 * hp], st_ref[2 * hp + 1]]
        for u in range(nsub):
            ex = exs[u]
            r0 = u * c
            qs, kks, vbs, outs = [], [], [], []
            for n, lo in enumerate(los):
                q = q_ref[r0:r0 + c, lo:lo + HEAD_W].astype(F32)
                vb = i_ref[r0:r0 + c, lo:lo + HEAD_W]
                kk = kk_all[r0:r0 + c, lo:lo + HEAD_W]
                bcum = ex[0:c, lo:lo + HEAD_W]
                q_in = (q * jnp.exp(bcum)).astype(BF16)
                outs.append(lax.dot_general(q_in, sts[n].astype(BF16), nt,
                                            preferred_element_type=F32))
                k_out = (kk * jnp.exp(ex[c:2 * c, lo:lo + HEAD_W])).astype(BF16)
                e_last = jnp.exp(bcum[c - 1:c, :])
                sts[n] = e_last * sts[n] + jnp.dot(vb.astype(F32).T.astype(BF16), k_out,
                                                   preferred_element_type=F32)
                qs.append(q)
                kks.append(kk)
                vbs.append(vb)
            yield

            a = msk_ref[0] * lax.dot_general(
                jnp.concatenate([qs[0].astype(BF16), qs[1].astype(BF16)], axis=1),
                pair_rows(kks[0].astype(BF16), kks[1].astype(BF16)), nt,
                preferred_element_type=F32)
            for li, lv in enumerate(HG_LEVELS):
                qside = (rows & (lv - 1)) >= (lv // 2)
                qls, kls = [], []
                for n, lo in enumerate(los):
                    el = jnp.exp(ex[(2 + li) * c:(3 + li) * c, lo:lo + HEAD_W])
                    qls.append(jnp.where(qside, qs[n] * el, 0.0).astype(BF16))
                    kls.append(jnp.where(qside, 0.0, kks[n] * el).astype(BF16))
                al = lax.dot_general(jnp.concatenate(qls, axis=1), pair_rows(kls[0], kls[1]), nt,
                                     preferred_element_type=F32)
                a = a + msk_ref[1 + li] * al
                if li % 2 == 0:
                    yield
            o2 = jnp.dot(a.astype(BF16), pair_rows(vbs[0], vbs[1]),
                         preferred_element_type=F32)

            for n, lo in enumerate(los):
                o = outs[n] + o2[:, n * HEAD_W:(n + 1) * HEAD_W]
                ms = jnp.mean(o * o, axis=-1, keepdims=True)
                on = o * lax.rsqrt(ms + RMS_EPS) * ng_ref[:, lo:lo + HEAD_W]
                y = on * _silu(g_ref[r0:r0 + c, lo:lo + HEAD_W].astype(F32))
                y_ref[r0:r0 + c, lo:lo + HEAD_W] = y.astype(y_ref.dtype)
            yield
        st_ref[2 * hp] = sts[0]
        st_ref[2 * hp + 1] = sts[1]


ATTN_ROWS = 64


def _diff_attn_kernel(lam_ref, q_ref, k_ref, v_ref, g_ref, ng_ref, o_ref,
                      qq_ref, pa_ref, pb_ref, pt_ref, pu_ref,
                      ma_ref, la_ref, aa_ref, mb_ref, lb_ref, ab_ref,
                      mt_ref, lt_ref, at_ref, mu_ref, lu_ref, au_ref,
                      acc_ref, *, tk, nq, out_scale):
    lam = lam_ref[0, 0]
    scale = DA_DQK ** -0.5
    tq = 2 * tk
    nrow = 2 * tq
    ncb = tk // HEAD_W
    st_a = (ma_ref, la_ref, aa_ref)
    st_b = (mb_ref, lb_ref, ab_ref)
    st_t = (mt_ref, lt_ref, at_ref)
    st_u = (mu_ref, lu_ref, au_ref)

    def scores(j, ra=0, rb=nrow):
        kb = k_ref[pl.ds(pl.multiple_of(j * tk, tk), tk), :]
        return lax.dot_general(qq_ref[ra:rb, :], kb, (((1,), (1,)), ((), ())),
                               preferred_element_type=F32)

    def accumulate(j, p_ref, st, ra=0, rb=nrow):
        vb = v_ref[pl.ds(pl.multiple_of(j * tk, tk), tk), :]
        acc_ref[ra:rb, :] = (st[2][ra:rb, :] * acc_ref[ra:rb, :]
                             + jnp.dot(p_ref[ra:rb, :], vb, preferred_element_type=F32))

    def softmax(s_all, p_ref, st_in, st_out, diag=None, ra=0, rb=nrow):
        masked = diag is not None
        for r0 in range(ra, rb, ATTN_ROWS):
            rs = slice(r0, r0 + ATTN_ROWS)
            row_lo = (r0 // (2 * tk)) * tk + r0 % tk
            key_lo = diag * tk if masked else 0
            tiles = []
            for cb in range(ncb):
                k0 = key_lo + cb * HEAD_W
                if masked and k0 > row_lo + ATTN_ROWS - 1:
                    tiles.append(None)
                    continue
                s = s_all[r0 - ra:r0 - ra + ATTN_ROWS, cb * HEAD_W:(cb + 1) * HEAD_W]
                if masked and k0 + HEAD_W - 1 > row_lo:
                    r = row_lo + lax.broadcasted_iota(jnp.int32, s.shape, 0)
                    cidx = k0 + lax.broadcasted_iota(jnp.int32, s.shape, 1)
                    s = jnp.where(cidx <= r, s, MASK_VALUE)
                tiles.append(s)
            live = [s for s in tiles if s is not None]
            mx = functools.reduce(jnp.maximum, live)
            m_blk = jnp.broadcast_to(jnp.max(mx, axis=-1, keepdims=True), mx.shape)
            m_old = st_in[0][rs, :]
            m_new = jnp.maximum(m_old, m_blk)
            alpha = jnp.exp2(m_old - m_new)
            psum = None
            for cb, s in enumerate(tiles):
                if s is None:
                    p_ref[rs, cb * HEAD_W:(cb + 1) * HEAD_W] = jnp.zeros((ATTN_ROWS, HEAD_W), BF16)
                    continue
                p = jnp.exp2(s - m_new)
                psum = p if psum is None else psum + p
                p_ref[rs, cb * HEAD_W:(cb + 1) * HEAD_W] = p.astype(BF16)
            st_out[0][rs, :] = m_new
            st_out[1][rs, :] = alpha * st_in[1][rs, :] + psum
            st_out[2][rs, :] = alpha

    def step(j, p_in, p_out, st_in, st_out):
        s = scores(j)
        accumulate(jnp.maximum(j - 1, 0), p_in, st_in)
        softmax(s, p_out, st_in, st_out)

    def pair(jj, carry):
        step(2 * jj, pb_ref, pa_ref, st_b, st_a)
        step(2 * jj + 1, pa_ref, pb_ref, st_a, st_b)
        return carry

    def q_tile(i, carry):
        for half in range(2):
            rows = pl.ds(pl.multiple_of(i * tq + half * tk, tk), tk)
            q = (q_ref[rows, :].astype(F32) * (scale * math.log2(math.e))).astype(BF16)
            lane = lax.broadcasted_iota(jnp.int32, q.shape, 1)
            zero = jnp.zeros_like(q)
            qq_ref[2 * half * tk:(2 * half + 1) * tk, :] = jnp.where(lane < DA_DQK, q, zero)
            qq_ref[(2 * half + 1) * tk:(2 * half + 2) * tk, :] = jnp.where(lane >= DA_DQK, q, zero)
        acc_ref[...] = jnp.zeros_like(acc_ref)
        mb_ref[...] = jnp.full_like(mb_ref, MASK_VALUE)
        lb_ref[...] = jnp.zeros_like(lb_ref)
        ab_ref[...] = jnp.ones_like(ab_ref)
        pb_ref[...] = jnp.zeros_like(pb_ref)

        lax.fori_loop(0, i, pair, 0)

        j0 = 2 * i
        s = scores(j0)
        accumulate(jnp.maximum(j0 - 1, 0), pb_ref, st_b)
        softmax(s, pt_ref, st_b, st_t, diag=0)
        s = scores(j0 + 1, tq, nrow)
        accumulate(j0, pt_ref, st_t)
        softmax(s, pu_ref, st_t, st_u, diag=1, ra=tq, rb=nrow)
        accumulate(j0 + 1, pu_ref, st_u, tq, nrow)

        for half, st in enumerate((st_t, st_u)):
            r0 = 2 * half * tk
            l = jnp.sum(st[1][r0:r0 + 2 * tk, :], axis=-1, keepdims=True)
            acc = acc_ref[r0:r0 + 2 * tk, :] / l
            o = acc[0:tk] - lam * acc[tk:2 * tk]
            ms = jnp.mean(o * o, axis=-1, keepdims=True)
            on = o * lax.rsqrt(ms + RMS_EPS) * ng_ref[...] * out_scale
            rows = pl.ds(pl.multiple_of(i * tq + half * tk, tk), tk)
            o_ref[rows, :] = (on * _silu(g_ref[rows, :].astype(F32))).astype(o_ref.dtype)
        return carry

    lax.fori_loop(0, nq, q_tile, 0)


def _diff_attn(h, lam, lam_init, norm_g, batch, seq, tk=512):
    t = h.shape[0]
    tq = 2 * tk
    nq = seq // tq
    per = GROUP_W // HEAD_W
    ng2 = norm_g.reshape(1, GROUP_W).astype(F32)
    lam2 = jnp.reshape(lam, (1, 1)).astype(F32)
    kern = functools.partial(_diff_attn_kernel, tk=tk, nq=nq, out_scale=1.0 - lam_init)

    def head_cols(blk):
        return pl.BlockSpec((seq, HEAD_W), lambda b, hh, blk=blk: (b, blk * per + hh))

    return pl.pallas_call(
        kern,
        grid=(batch, N_HEADS),
        in_specs=[pl.BlockSpec(memory_space=pltpu.SMEM),
                  head_cols(C_Q), head_cols(C_K), head_cols(C_V), head_cols(G_C),
                  pl.BlockSpec((1, HEAD_W), lambda b, hh: (0, hh))],
        out_specs=pl.BlockSpec((seq, HEAD_W), lambda b, hh: (b, hh)),
        out_shape=jax.ShapeDtypeStruct((t, GROUP_W), BF16),
        scratch_shapes=[pltpu.VMEM((2 * tq, HEAD_W), BF16),
                        *([pltpu.VMEM((2 * tq, tk), BF16)] * 4),
                        *([pltpu.VMEM((2 * tq, HEAD_W), F32)] * 13)],
        compiler_params=pltpu.CompilerParams(
            dimension_semantics=("arbitrary", "arbitrary"),
            vmem_limit_bytes=VMEM_LIMIT),
        name="diff_attn",
    )(lam2, h, h, h, h, ng2)


TAIL_ROWS = 256


def _tail_kernel(ab_ref, ac_ref, ax_ref, ga_ref, du_ref, dv_ref, gd_ref,
                 bq_ref, bf_ref, bi_ref, gb_ref, yc_ref, x_ref, p_ref,
                 cw_ref, sglng_ref, sglnb_ref, ws_ref, bst_ref,
                 hlb_ref, hng_ref, rng_ref, msk_ref,
                 wo_ref, wpg_ref, wpe_ref, lng_ref, lnb_ref,
                 xo_ref, xb_ref,
                 ya_ref, yb_ref, yd_ref, carry_ref, st_ref, *, seq, alpha):
    n = pl.program_id(0)

    @pl.when(n == 0)
    def _():
        ya_ref[...] = jnp.zeros_like(ya_ref)
        yb_ref[...] = jnp.zeros_like(yb_ref)
        yd_ref[...] = jnp.zeros_like(yd_ref)

    @pl.when((n * TAIL_ROWS) % seq == 0)
    def _():
        carry_ref[...] = jnp.zeros_like(carry_ref)
        st_ref[...] = jnp.zeros_like(st_ref)

    wslot = n % 2
    rslot = 1 - wslot

    def mixers():
        yield from _conv_sgu_body(ab_ref, ac_ref, ax_ref, ga_ref, du_ref, dv_ref, gd_ref,
                                  cw_ref, sglng_ref, sglnb_ref, ws_ref, bst_ref,
                                  ya_ref.at[wslot], yd_ref.at[wslot], carry_ref, ts=TAIL_ROWS)
        yield from _hgrn_body(bq_ref, bf_ref, bi_ref, gb_ref, hlb_ref, hng_ref, rng_ref, msk_ref,
                              yb_ref.at[wslot], st_ref, nsub=TAIL_ROWS // HG_CHUNK)

    def projections():
        d = x_ref.shape[-1]
        acc = alpha * x_ref[...]
        for y, g in ((yc_ref, 2), (ya_ref.at[rslot], 0), (yd_ref.at[rslot], 3),
                     (yb_ref.at[rslot], 1)):
            acc = acc + jnp.dot(y[...], wo_ref[g * GROUP_W:(g + 1) * GROUP_W, :],
                                preferred_element_type=F32)
            yield
        pe = jnp.dot(p_ref[...].astype(BF16), wpe_ref[...], preferred_element_type=F32)
        mu = jnp.mean(acc, axis=-1, keepdims=True)
        xc = acc - mu
        var = jnp.mean(xc * xc, axis=-1, keepdims=True)
        xn = xc * lax.rsqrt(var + LN_EPS) * lng_ref[...] + lnb_ref[...]
        xnb = xn.astype(BF16)
        yield
        for n0 in range(0, d, GROUP_W):
            cols = slice(n0, n0 + GROUP_W)
            z = jnp.dot(xnb, wpg_ref[:, cols], preferred_element_type=F32)
            out = xn[:, cols] + pe[:, cols] * _sigmoid(z)
            xo_ref[:, cols] = out
            xb_ref[:, cols] = out.astype(BF16)
            yield

    major, minor = mixers(), projections()
    live_major = live_minor = True
    while live_major or live_minor:
        for _ in range(2):
            live_major = live_major and next(major, False) is not False
        live_minor = live_minor and next(minor, False) is not False


def _tail_block(h, yc, x, p_all, small, wo_all, wpg_all, wpe_all, lng_all, lnb_all,
                layer, seq, alpha):
    t, d = x.shape
    tm = TAIL_ROWS
    pdim = p_all.shape[1]
    steps = t // tm

    def hblk(blk):
        return pl.BlockSpec((tm, GROUP_W), lambda i, blk=blk: (jnp.minimum(i, steps - 1), blk))

    def rows(w):
        return pl.BlockSpec((tm, w), lambda i: (jnp.maximum(i - 1, 0), 0))

    def full(a):
        nd = a.ndim
        return pl.BlockSpec(a.shape, lambda i, nd=nd: (0,) * nd)

    def resident(a):
        return pl.BlockSpec((None,) + a.shape[1:], lambda i: (layer, 0, 0),
                            pipeline_mode=pl.Buffered(1))

    h_blocks = (A_B, A_C, A_X, G_A, D_U, D_V, G_D, B_Q, B_F, B_I, G_B)
    kern = functools.partial(_tail_kernel, seq=seq, alpha=alpha)
    return pl.pallas_call(
        kern,
        grid=(steps + 1,),
        in_specs=[hblk(b) for b in h_blocks]
                 + [rows(GROUP_W), rows(d),
                    pl.BlockSpec((tm, pdim),
                                 lambda i: (layer * steps + jnp.maximum(i - 1, 0), 0))]
                 + [full(a) for a in small]
                 + [resident(wo_all), resident(wpg_all), resident(wpe_all),
                    resident(lng_all), resident(lnb_all)],
        out_specs=[rows(d), rows(d)],
        out_shape=[jax.ShapeDtypeStruct((t, d), F32), jax.ShapeDtypeStruct((t, d), BF16)],
        scratch_shapes=[pltpu.VMEM((2, tm, GROUP_W), BF16), pltpu.VMEM((2, tm, GROUP_W), BF16),
                        pltpu.VMEM((2, tm, GROUP_W), BF16),
                        pltpu.VMEM((8, GROUP_W), F32),
                        pltpu.VMEM((N_HEADS, HEAD_W, HEAD_W), F32)],
        compiler_params=pltpu.CompilerParams(
            dimension_semantics=("arbitrary",), vmem_limit_bytes=VMEM_LIMIT),
        name="mix_out",
    )(*([h] * len(h_blocks)), yc, x, p_all, *small,
      wo_all, wpg_all, wpe_all, lng_all, lnb_all)


def kernel(x, p, w_in, conv_w, hgrn_lb, hgrn_norm_g, diff_lambda, diff_norm_g,
           sg_ln_g, sg_ln_b, sg_w, sg_b, w_out, ln_g, ln_b, w_pe, w_pg):
    batch, seq, d_model = x.shape
    depth = w_in.shape[0]
    t = batch * seq
    alpha = (2 * depth) ** 0.25

    lb_sm = jax.nn.softmax(hgrn_lb.astype(F32), axis=0)
    lower_bounds = jnp.cumsum(lb_sm, axis=0) - lb_sm[0]

    xf = x.reshape(t, d_model)
    xb = xf
    p_all = p.reshape(depth * t, p.shape[-1])
    wo_all, wpg_all, wpe_all = w_out.astype(BF16), w_pg.astype(BF16), w_pe.astype(BF16)
    lng_all = ln_g.reshape(depth, 1, d_model).astype(F32)
    lnb_all = ln_b.reshape(depth, 1, d_model).astype(F32)
    rng, masks = _hgrn_constants()
    for i in range(depth):
        lam_init = 0.8 - 0.6 * math.exp(-0.3 * i)
        dl = diff_lambda[i].astype(F32)
        lam = (jnp.exp(jnp.sum(dl[0] * dl[1])) - jnp.exp(jnp.sum(dl[2] * dl[3])) + lam_init)

        h = _in_proj(xb, w_in, i)
        yc = _diff_attn(h, lam, lam_init, diff_norm_g[i], batch, seq)
        small = (conv_w[i].astype(F32),
                 sg_ln_g[i].reshape(1, GROUP_W).astype(F32), sg_ln_b[i].reshape(1, GROUP_W).astype(F32),
                 sg_w[i].astype(F32), sg_b[i].T.astype(F32),
                 lower_bounds[i].reshape(1, GROUP_W), hgrn_norm_g[i].reshape(1, GROUP_W).astype(F32),
                 rng, masks)
        xf, xb = _tail_block(h, yc, xf, p_all, small, wo_all, wpg_all, wpe_all,
                             lng_all, lnb_all, i, seq, alpha)
    return xf.reshape(batch, seq, d_model)
```

```python
import functools
import math

import numpy as np
import jax
import jax.numpy as jnp
from jax import lax
from jax.experimental import pallas as pl
from jax.experimental.pallas import tpu as pltpu

F32 = jnp.float32
BF16 = jnp.bfloat16

GROUP_W = 512
HEAD_W = 128
N_HEADS = GROUP_W // HEAD_W
DA_DQK = 64
F_FLOOR = 1e-30
MASK_VALUE = -1e30
LN_EPS = 1e-5
RMS_EPS = 1e-6

A_B, A_C, A_X, B_Q, B_F, B_I, C_Q, C_K, C_V, D_U, D_V, G_A, G_B, G_C, G_D = range(15)

HG_CHUNK = 128
HG_LEVELS = (2, 4, 8, 16, 32, 64, 128)
SG_CHUNK = 128
VMEM_LIMIT = 56 * 1024 * 1024


def _sigmoid(x):
    return 1.0 / (1.0 + jnp.exp(-x))


def _silu(x):
    return x * _sigmoid(x)


def _gelu_tanh(x):
    c = math.sqrt(2.0 / math.pi)
    return 0.5 * x * (1.0 + jnp.tanh(c * (x + 0.044715 * (x * x * x))))


def _matmul_kernel(x_ref, w_ref, o_ref):
    o_ref[...] = jnp.dot(x_ref[...].astype(BF16), w_ref[...].astype(BF16),
                         preferred_element_type=F32).astype(o_ref.dtype)


def _in_proj(xb, w_all, layer, tm=2048, tn=512):
    t, k = xb.shape
    n = w_all.shape[2]
    return pl.pallas_call(
        _matmul_kernel,
        grid=(t // tm, n // tn),
        in_specs=[pl.BlockSpec((tm, k), lambda i, j: (i, 0)),
                  pl.BlockSpec((None, k, tn), lambda i, j: (layer, 0, j))],
        out_specs=pl.BlockSpec((tm, tn), lambda i, j: (i, j)),
        out_shape=jax.ShapeDtypeStruct((t, n), BF16),
        compiler_params=pltpu.CompilerParams(
            dimension_semantics=("arbitrary", "arbitrary"),
            vmem_limit_bytes=VMEM_LIMIT),
        name="in_proj",
    )(xb, w_all)


def _conv_sgu_body(ab_ref, ac_ref, ax_ref, ga_ref, du_ref, dv_ref, gd_ref,
                   cw_ref, lng_ref, lnb_ref, ws_ref, bst_ref,
                   ya_ref, yd_ref, carry_ref, *, ts):
    z = ac_ref[...].astype(F32) * ax_ref[...].astype(F32)
    rows = lax.broadcasted_iota(jnp.int32, z.shape, 0)
    prev1 = carry_ref[7:8, :]
    prev2 = carry_ref[6:7, :]
    z1 = jnp.where(rows == 0, prev1, pltpu.roll(z, 1, 0))
    z2 = jnp.where(rows == 0, prev2, jnp.where(rows == 1, prev1, pltpu.roll(z, 2, 0)))
    cw = cw_ref[...]
    y = cw[0:1, :] * z2 + cw[1:2, :] * z1 + cw[2:3, :] * z
    ya = ab_ref[...].astype(F32) * y * _silu(ga_ref[...].astype(F32))
    ya_ref[...] = ya.astype(ya_ref.dtype)
    carry_ref[...] = z[ts - 8:ts, :]
    yield

    u = _gelu_tanh(du_ref[...].astype(F32))
    v = _gelu_tanh(dv_ref[...].astype(F32))
    mu = jnp.mean(v, axis=-1, keepdims=True)
    vc = v - mu
    var = jnp.mean(vc * vc, axis=-1, keepdims=True)
    vn = (vc * lax.rsqrt(var + LN_EPS) * lng_ref[...] + lnb_ref[...]).astype(BF16)
    gate = _silu(gd_ref[...].astype(F32))
    yield
    tri_r = lax.broadcasted_iota(jnp.int32, (SG_CHUNK, SG_CHUNK), 0)
    tri_c = lax.broadcasted_iota(jnp.int32, (SG_CHUNK, SG_CHUNK), 1)
    bst = bst_ref[...]
    for g in range(N_HEADS):
        w = jnp.where(tri_c <= tri_r, ws_ref[g], 0.0).astype(BF16)
        bias = bst[:, g:g + 1]
        lo = g * HEAD_W
        for c in range(ts // SG_CHUNK):
            r0 = c * SG_CHUNK
            sv = jnp.dot(w, vn[r0:r0 + SG_CHUNK, lo:lo + HEAD_W],
                         preferred_element_type=F32) + bias
            yd = u[r0:r0 + SG_CHUNK, lo:lo + HEAD_W] * sv * gate[r0:r0 + SG_CHUNK, lo:lo + HEAD_W]
            yd_ref[r0:r0 + SG_CHUNK, lo:lo + HEAD_W] = yd.astype(yd_ref.dtype)
        if g % 2 == 1:
            yield


def _hgrn_constants():
    c = HG_CHUNK
    t = np.arange(c)[:, None]
    s = np.arange(c)[None, :]
    mats = [(s <= t), (s > t)]
    for lv in HG_LEVELS:
        mid = (t // lv) * lv + lv // 2
        qside = t >= mid
        mats.append(np.where(qside, (s >= mid) & (s <= t), (s > t) & (s < mid)))
    rng = np.concatenate(mats, axis=0).astype(np.float32)
    rng = np.concatenate([rng, rng], axis=1)
    masks = np.stack([t == s] + [(t // lv) == (s // lv) for lv in HG_LEVELS]).astype(np.float32)
    masks = np.concatenate([masks, masks], axis=2)
    return jnp.asarray(rng, BF16), jnp.asarray(masks, F32)


def _hgrn_body(q_ref, f_ref, i_ref, g_ref, lb_ref, ng_ref, rng_ref, msk_ref,
               y_ref, st_ref, *, nsub):
    c = HG_CHUNK
    fz = f_ref[...].astype(F32)
    lb = lb_ref[...]
    e = jnp.exp(-jnp.abs(fz))
    r = 1.0 / (1.0 + e)
    pos = fz >= 0.0
    sig = jnp.where(pos, r, e * r)
    nsig = jnp.where(pos, e * r, r)
    f = lb + (1.0 - lb) * sig
    lg = jnp.log(jnp.maximum(f, F_FLOOR))
    kk_all = (1.0 - lb) * nsig

    g1 = lg.astype(BF16)
    g2 = (lg - g1.astype(F32)).astype(BF16)
    rng = rng_ref[...]
    exs = [jnp.dot(rng, jnp.concatenate([g1[u * c:(u + 1) * c], g2[u * c:(u + 1) * c]], axis=0),
                   preferred_element_type=F32)
           for u in range(nsub)]
    yield

    rows = lax.broadcasted_iota(jnp.int32, (c, HEAD_W), 0)
    zblk = jnp.zeros((c, HEAD_W), BF16)
    nt = (((1,), (1,)), ((), ()))

    def pair_rows(x0, x1):
        return jnp.concatenate([jnp.concatenate([x0, zblk], axis=1),
                                jnp.concatenate([zblk, x1], axis=1)], axis=0)

    for hp in range(N_HEADS // 2):
        los = (2 * hp * HEAD_W, (2 * hp + 1) * HEAD_W)
        sts = [st_ref[2 * hp], st_ref[2 * hp + 1]]
        for u in range(nsub):
            ex = exs[u]
            r0 = u * c
            qs, kks, vbs, outs = [], [], [], []
            for n, lo in enumerate(los):
                q = q_ref[r0:r0 + c, lo:lo + HEAD_W].astype(F32)
                vb = i_ref[r0:r0 + c, lo:lo + HEAD_W]
                kk = kk_all[r0:r0 + c, lo:lo + HEAD_W]
                bcum = ex[0:c, lo:lo + HEAD_W]
                q_in = (q * jnp.exp(bcum)).astype(BF16)
                outs.append(lax.dot_general(q_in, sts[n].astype(BF16), nt,
                                            preferred_element_type=F32))
                k_out = (kk * jnp.exp(ex[c:2 * c, lo:lo + HEAD_W])).astype(BF16)
                e_last = jnp.exp(bcum[c - 1:c, :])
                sts[n] = e_last * sts[n] + jnp.dot(vb.astype(F32).T.astype(BF16), k_out,
                                                   preferred_element_type=F32)
                qs.append(q)
                kks.append(kk)
                vbs.append(vb)
            yield

            a = msk_ref[0] * lax.dot_general(
                jnp.concatenate([qs[0].astype(BF16), qs[1].astype(BF16)], axis=1),
                pair_rows(kks[0].astype(BF16), kks[1].astype(BF16)), nt,
                preferred_element_type=F32)
            for li, lv in enumerate(HG_LEVELS):
                qside = (rows & (lv - 1)) >= (lv // 2)
                qls, kls = [], []
                for n, lo in enumerate(los):
                    el = jnp.exp(ex[(2 + li) * c:(3 + li) * c, lo:lo + HEAD_W])
                    qls.append(jnp.where(qside, qs[n] * el, 0.0).astype(BF16))
                    kls.append(jnp.where(qside, 0.0, kks[n] * el).astype(BF16))
                al = lax.dot_general(jnp.concatenate(qls, axis=1), pair_rows(kls[0], kls[1]), nt,
                                     preferred_element_type=F32)
                a = a + msk_ref[1 + li] * al
                if li % 2 == 0:
                    yield
            o2 = jnp.dot(a.astype(BF16), pair_rows(vbs[0], vbs[1]),
                         preferred_element_type=F32)

            for n, lo in enumerate(los):
                o = outs[n] + o2[:, n * HEAD_W:(n + 1) * HEAD_W]
                ms = jnp.mean(o * o, axis=-1, keepdims=True)
                on = o * lax.rsqrt(ms + RMS_EPS) * ng_ref[:, lo:lo + HEAD_W]
                y = on * _silu(g_ref[r0:r0 + c, lo:lo + HEAD_W].astype(F32))
                y_ref[r0:r0 + c, lo:lo + HEAD_W] = y.astype(y_ref.dtype)
            yield
        st_ref[2 * hp] = sts[0]
        st_ref[2 * hp + 1] = sts[1]


ATTN_ROWS = 64


def _diff_attn_kernel(lam_ref, q_ref, k_ref, v_ref, g_ref, ng_ref, o_ref,
                      qq_ref, pa_ref, pb_ref, pt_ref, pu_ref,
                      ma_ref, la_ref, aa_ref, mb_ref, lb_ref, ab_ref,
                      mt_ref, lt_ref, at_ref, mu_ref, lu_ref, au_ref,
                      acc_ref, *, tk, nq, out_scale):
    lam = lam_ref[0, 0]
    scale = DA_DQK ** -0.5
    tq = 2 * tk
    nrow = 2 * tq
    ncb = tk // HEAD_W
    st_a = (ma_ref, la_ref, aa_ref)
    st_b = (mb_ref, lb_ref, ab_ref)
    st_t = (mt_ref, lt_ref, at_ref)
    st_u = (mu_ref, lu_ref, au_ref)

    def scores(j, ra=0, rb=nrow):
        kb = k_ref[pl.ds(pl.multiple_of(j * tk, tk), tk), :]
        return lax.dot_general(qq_ref[ra:rb, :], kb, (((1,), (1,)), ((), ())),
                               preferred_element_type=F32)

    def accumulate(j, p_ref, st, ra=0, rb=nrow):
        vb = v_ref[pl.ds(pl.multiple_of(j * tk, tk), tk), :]
        acc_ref[ra:rb, :] = (st[2][ra:rb, :] * acc_ref[ra:rb, :]
                             + jnp.dot(p_ref[ra:rb, :], vb, preferred_element_type=F32))

    def softmax(s_all, p_ref, st_in, st_out, diag=None, ra=0, rb=nrow):
        masked = diag is not None
        for r0 in range(ra, rb, ATTN_ROWS):
            rs = slice(r0, r0 + ATTN_ROWS)
            row_lo = (r0 // (2 * tk)) * tk + r0 % tk
            key_lo = diag * tk if masked else 0
            tiles = []
            for cb in range(ncb):
                k0 = key_lo + cb * HEAD_W
                if masked and k0 > row_lo + ATTN_ROWS - 1:
                    tiles.append(None)
                    continue
                s = s_all[r0 - ra:r0 - ra + ATTN_ROWS, cb * HEAD_W:(cb + 1) * HEAD_W]
                if masked and k0 + HEAD_W - 1 > row_lo:
                    r = row_lo + lax.broadcasted_iota(jnp.int32, s.shape, 0)
                    cidx = k0 + lax.broadcasted_iota(jnp.int32, s.shape, 1)
                    s = jnp.where(cidx <= r, s, MASK_VALUE)
                tiles.append(s)
            live = [s for s in tiles if s is not None]
            mx = functools.reduce(jnp.maximum, live)
            m_blk = jnp.broadcast_to(jnp.max(mx, axis=-1, keepdims=True), mx.shape)
            m_old = st_in[0][rs, :]
            m_new = jnp.maximum(m_old, m_blk)
            alpha = jnp.exp2(m_old - m_new)
            psum = None
            for cb, s in enumerate(tiles):
                if s is None:
                    p_ref[rs, cb * HEAD_W:(cb + 1) * HEAD_W] = jnp.zeros((ATTN_ROWS, HEAD_W), BF16)
                    continue
                p = jnp.exp2(s - m_new)
                psum = p if psum is None else psum + p
                p_ref[rs, cb * HEAD_W:(cb + 1) * HEAD_W] = p.astype(BF16)
            st_out[0][rs, :] = m_new
            st_out[1][rs, :] = alpha * st_in[1][rs, :] + psum
            st_out[2][rs, :] = alpha

    def step(j, p_in, p_out, st_in, st_out):
        s = scores(j)
        accumulate(jnp.maximum(j - 1, 0), p_in, st_in)
        softmax(s, p_out, st_in, st_out)

    def pair(jj, carry):
        step(2 * jj, pb_ref, pa_ref, st_b, st_a)
        step(2 * jj + 1, pa_ref, pb_ref, st_a, st_b)
        return carry

    def q_tile(i, carry):
        for half in range(2):
            rows = pl.ds(pl.multiple_of(i * tq + half * tk, tk), tk)
            q = (q_ref[rows, :].astype(F32) * (scale * math.log2(math.e))).astype(BF16)
            lane = lax.broadcasted_iota(jnp.int32, q.shape, 1)
            zero = jnp.zeros_like(q)
            qq_ref[2 * half * tk:(2 * half + 1) * tk, :] = jnp.where(lane < DA_DQK, q, zero)
            qq_ref[(2 * half + 1) * tk:(2 * half + 2) * tk, :] = jnp.where(lane >= DA_DQK, q, zero)
        acc_ref[...] = jnp.zeros_like(acc_ref)
        mb_ref[...] = jnp.full_like(mb_ref, MASK_VALUE)
        lb_ref[...] = jnp.zeros_like(lb_ref)
        ab_ref[...] = jnp.ones_like(ab_ref)
        pb_ref[...] = jnp.zeros_like(pb_ref)

        lax.fori_loop(0, i, pair, 0)

        j0 = 2 * i
        s = scores(j0)
        accumulate(jnp.maximum(j0 - 1, 0), pb_ref, st_b)
        softmax(s, pt_ref, st_b, st_t, diag=0)
        s = scores(j0 + 1, tq, nrow)
        accumulate(j0, pt_ref, st_t)
        softmax(s, pu_ref, st_t, st_u, diag=1, ra=tq, rb=nrow)
        accumulate(j0 + 1, pu_ref, st_u, tq, nrow)

        for half, st in enumerate((st_t, st_u)):
            r0 = 2 * half * tk
            l = jnp.sum(st[1][r0:r0 + 2 * tk, :], axis=-1, keepdims=True)
            acc = acc_ref[r0:r0 + 2 * tk, :] / l
            o = acc[0:tk] - lam * acc[tk:2 * tk]
            ms = jnp.mean(o * o, axis=-1, keepdims=True)
            on = o * lax.rsqrt(ms + RMS_EPS) * ng_ref[...] * out_scale
            rows = pl.ds(pl.multiple_of(i * tq + half * tk, tk), tk)
            o_ref[rows, :] = (on * _silu(g_ref[rows, :].astype(F32))).astype(o_ref.dtype)
        return carry

    lax.fori_loop(0, nq, q_tile, 0)


def _diff_attn(h, lam, lam_init, norm_g, batch, seq, tk=512):
    t = h.shape[0]
    tq = 2 * tk
    nq = seq // tq
    per = GROUP_W // HEAD_W
    ng2 = norm_g.reshape(1, GROUP_W).astype(F32)
    lam2 = jnp.reshape(lam, (1, 1)).astype(F32)
    kern = functools.partial(_diff_attn_kernel, tk=tk, nq=nq, out_scale=1.0 - lam_init)

    def head_cols(blk):
        return pl.BlockSpec((seq, HEAD_W), lambda b, hh, blk=blk: (b, blk * per + hh))

    return pl.pallas_call(
        kern,
        grid=(batch, N_HEADS),
        in_specs=[pl.BlockSpec(memory_space=pltpu.SMEM),
                  head_cols(C_Q), head_cols(C_K), head_cols(C_V), head_cols(G_C),
                  pl.BlockSpec((1, HEAD_W), lambda b, hh: (0, hh))],
        out_specs=pl.BlockSpec((seq, HEAD_W), lambda b, hh: (b, hh)),
        out_shape=jax.ShapeDtypeStruct((t, GROUP_W), BF16),
        scratch_shapes=[pltpu.VMEM((2 * tq, HEAD_W), BF16),
                        *([pltpu.VMEM((2 * tq, tk), BF16)] * 4),
                        *([pltpu.VMEM((2 * tq, HEAD_W), F32)] * 13)],
        compiler_params=pltpu.CompilerParams(
            dimension_semantics=("arbitrary", "arbitrary"),
            vmem_limit_bytes=VMEM_LIMIT),
        name="diff_attn",
    )(lam2, h, h, h, h, ng2)


TAIL_ROWS = 256


def _tail_kernel(ab_ref, ac_ref, ax_ref, ga_ref, du_ref, dv_ref, gd_ref,
                 bq_ref, bf_ref, bi_ref, gb_ref, yc_ref, x_ref, p_ref,
                 cw_ref, sglng_ref, sglnb_ref, ws_ref, bst_ref,
                 hlb_ref, hng_ref, rng_ref, msk_ref,
                 wo_ref, wpg_ref, wpe_ref, lng_ref, lnb_ref,
                 xo_ref, xb_ref,
                 ya_ref, yb_ref, yd_ref, carry_ref, st_ref, *, seq, alpha):
    n = pl.program_id(0)

    @pl.when(n == 0)
    def _():
        ya_ref[...] = jnp.zeros_like(ya_ref)
        yb_ref[...] = jnp.zeros_like(yb_ref)
        yd_ref[...] = jnp.zeros_like(yd_ref)

    @pl.when((n * TAIL_ROWS) % seq == 0)
    def _():
        carry_ref[...] = jnp.zeros_like(carry_ref)
        st_ref[...] = jnp.zeros_like(st_ref)

    wslot = n % 2
    rslot = 1 - wslot

    def mixers():
        yield from _conv_sgu_body(ab_ref, ac_ref, ax_ref, ga_ref, du_ref, dv_ref, gd_ref,
                                  cw_ref, sglng_ref, sglnb_ref, ws_ref, bst_ref,
                                  ya_ref.at[wslot], yd_ref.at[wslot], carry_ref, ts=TAIL_ROWS)
        yield from _hgrn_body(bq_ref, bf_ref, bi_ref, gb_ref, hlb_ref, hng_ref, rng_ref, msk_ref,
                              yb_ref.at[wslot], st_ref, nsub=TAIL_ROWS // HG_CHUNK)

    def projections():
        d = x_ref.shape[-1]
        blocks = [slice(n0, n0 + GROUP_W) for n0 in range(0, d, GROUP_W)]
        ys = ((yc_ref, 2), (ya_ref.at[rslot], 0), (yd_ref.at[rslot], 3), (yb_ref.at[rslot], 1))
        accs = []
        for cols in blocks:
            acc = alpha * x_ref[:, cols]
            for y, g in ys:
                acc = acc + jnp.dot(y[...], wo_ref[g * GROUP_W:(g + 1) * GROUP_W, cols],
                                    preferred_element_type=F32)
                yield
            accs.append(acc)
        pe = jnp.dot(p_ref[...].astype(BF16), wpe_ref[...], preferred_element_type=F32)
        mu = sum(jnp.sum(a, axis=-1, keepdims=True) for a in accs) * (1.0 / d)
        xcs = [a - mu for a in accs]
        var = sum(jnp.sum(xc * xc, axis=-1, keepdims=True) for xc in xcs) * (1.0 / d)
        rstd = lax.rsqrt(var + LN_EPS)
        xns = [xc * rstd * lng_ref[:, cols] + lnb_ref[:, cols] for xc, cols in zip(xcs, blocks)]
        xnbs = [xn.astype(BF16) for xn in xns]
        yield
        for xn, cols in zip(xns, blocks):
            z = None
            for xnb, rows in zip(xnbs, blocks):
                part = jnp.dot(xnb, wpg_ref[rows, cols], preferred_element_type=F32)
                z = part if z is None else z + part
                yield
            out = xn + pe[:, cols] * _sigmoid(z)
            xo_ref[:, cols] = out
            xb_ref[:, cols] = out.astype(BF16)

    major, minor = mixers(), projections()
    live_major = live_minor = True
    while live_major or live_minor:
        live_major = live_major and next(major, False) is not False
        live_minor = live_minor and next(minor, False) is not False


def _tail_block(h, yc, x, p_all, small, wo_all, wpg_all, wpe_all, lng_all, lnb_all,
                layer, seq, alpha):
    t, d = x.shape
    tm = TAIL_ROWS
    pdim = p_all.shape[1]
    steps = t // tm

    def hblk(blk):
        return pl.BlockSpec((tm, GROUP_W), lambda i, blk=blk: (jnp.minimum(i, steps - 1), blk))

    def rows(w):
        return pl.BlockSpec((tm, w), lambda i: (jnp.maximum(i - 1, 0), 0))

    def full(a):
        nd = a.ndim
        return pl.BlockSpec(a.shape, lambda i, nd=nd: (0,) * nd)

    def resident(a):
        return pl.BlockSpec((None,) + a.shape[1:], lambda i: (layer, 0, 0),
                            pipeline_mode=pl.Buffered(1))

    h_blocks = (A_B, A_C, A_X, G_A, D_U, D_V, G_D, B_Q, B_F, B_I, G_B)
    kern = functools.partial(_tail_kernel, seq=seq, alpha=alpha)
    return pl.pallas_call(
        kern,
        grid=(steps + 1,),
        in_specs=[hblk(b) for b in h_blocks]
                 + [rows(GROUP_W), rows(d),
                    pl.BlockSpec((tm, pdim),
                                 lambda i: (layer * steps + jnp.maximum(i - 1, 0), 0))]
                 + [full(a) for a in small]
                 + [resident(wo_all), resident(wpg_all), resident(wpe_all),
                    resident(lng_all), resident(lnb_all)],
        out_specs=[rows(d), rows(d)],
        out_shape=[jax.ShapeDtypeStruct((t, d), F32), jax.ShapeDtypeStruct((t, d), BF16)],
        scratch_shapes=[pltpu.VMEM((2, tm, GROUP_W), BF16), pltpu.VMEM((2, tm, GROUP_W), BF16),
                        pltpu.VMEM((2, tm, GROUP_W), BF16),
                        pltpu.VMEM((8, GROUP_W), F32),
                        pltpu.VMEM((N_HEADS, HEAD_W, HEAD_W), F32)],
        compiler_params=pltpu.CompilerParams(
            dimension_semantics=("arbitrary",), vmem_limit_bytes=VMEM_LIMIT),
        name="mix_out",
    )(*([h] * len(h_blocks)), yc, x, p_all, *small,
      wo_all, wpg_all, wpe_all, lng_all, lnb_all)


def kernel(x, p, w_in, conv_w, hgrn_lb, hgrn_norm_g, diff_lambda, diff_norm_g,
           sg_ln_g, sg_ln_b, sg_w, sg_b, w_out, ln_g, ln_b, w_pe, w_pg):
    batch, seq, d_model = x.shape
    depth = w_in.shape[0]
    t = batch * seq
    alpha = (2 * depth) ** 0.25

    lb_sm = jax.nn.softmax(hgrn_lb.astype(F32), axis=0)
    lower_bounds = jnp.cumsum(lb_sm, axis=0) - lb_sm[0]

    xf = x.reshape(t, d_model)
    xb = xf
    p_all = p.reshape(depth * t, p.shape[-1])
    wo_all, wpg_all, wpe_all = w_out.astype(BF16), w_pg.astype(BF16), w_pe.astype(BF16)
    lng_all = ln_g.reshape(depth, 1, d_model).astype(F32)
    lnb_all = ln_b.reshape(depth, 1, d_model).astype(F32)
    rng, masks = _hgrn_constants()
    for i in range(depth):
        lam_init = 0.8 - 0.6 * math.exp(-0.3 * i)
        dl = diff_lambda[i].astype(F32)
        lam = (jnp.exp(jnp.sum(dl[0] * dl[1])) - jnp.exp(jnp.sum(dl[2] * dl[3])) + lam_init)

        h = _in_proj(xb, w_in, i)
        yc = _diff_attn(h, lam, lam_init, diff_norm_g[i], batch, seq)
        small = (conv_w[i].astype(F32),
                 sg_ln_g[i].reshape(1, GROUP_W).astype(F32), sg_ln_b[i].reshape(1, GROUP_W).astype(F32),
                 sg_w[i].astype(F32), sg_b[i].T.astype(F32),
                 lower_bounds[i].reshape(1, GROUP_W), hgrn_norm_g[i].reshape(1, GROUP_W).astype(F32),
                 rng, masks)
        xf, xb = _tail_block(h, yc, xf, p_all, small, wo_all, wpg_all, wpe_all,
                             lng_all, lnb_all, i, seq, alpha)
    return xf.reshape(batch, seq, d_model)
```

```python
import functools
import math

import numpy as np
import jax
import jax.numpy as jnp
from jax import lax
from jax.experimental import pallas as pl
from jax.experimental.pallas import tpu as pltpu

F32 = jnp.float32
BF16 = jnp.bfloat16

GROUP_W = 512
HEAD_W = 128
N_HEADS = GROUP_W // HEAD_W
DA_DQK = 64
F_FLOOR = 1e-30
MASK_VALUE = -1e30
LN_EPS = 1e-5
RMS_EPS = 1e-6

A_B, A_C, A_X, B_Q, B_F, B_I, C_Q, C_K, C_V, D_U, D_V, G_A, G_B, G_C, G_D = range(15)

HG_CHUNK = 128
HG_LEVELS = (2, 4, 8, 16, 32, 64, 128)
SG_CHUNK = 128
VMEM_LIMIT = 56 * 1024 * 1024


def _sigmoid(x):
    return 1.0 / (1.0 + jnp.exp(-x))


def _silu(x):
    return x * _sigmoid(x)


def _gelu_tanh(x):
    c = math.sqrt(2.0 / math.pi)
    return 0.5 * x * (1.0 + jnp.tanh(c * (x + 0.044715 * (x * x * x))))


def _matmul_kernel(x_ref, w_ref, o_ref):
    o_ref[...] = jnp.dot(x_ref[...].astype(BF16), w_ref[...].astype(BF16),
                         preferred_element_type=F32).astype(o_ref.dtype)


def _in_proj(xb, w_all, layer, tm=2048, tn=512):
    t, k = xb.shape
    n = w_all.shape[2]
    return pl.pallas_call(
        _matmul_kernel,
        grid=(t // tm, n // tn),
        in_specs=[pl.BlockSpec((tm, k), lambda i, j: (i, 0)),
                  pl.BlockSpec((None, k, tn), lambda i, j: (layer, 0, j))],
        out_specs=pl.BlockSpec((tm, tn), lambda i, j: (i, j)),
        out_shape=jax.ShapeDtypeStruct((t, n), BF16),
        compiler_params=pltpu.CompilerParams(
            dimension_semantics=("arbitrary", "arbitrary"),
            vmem_limit_bytes=VMEM_LIMIT),
        name="in_proj",
    )(xb, w_all)


def _conv_sgu_body(ab_ref, ac_ref, ax_ref, ga_ref, du_ref, dv_ref, gd_ref,
                   cw_ref, lng_ref, lnb_ref, ws_ref, bst_ref,
                   ya_ref, yd_ref, carry_ref, *, ts):
    z = ac_ref[...].astype(F32) * ax_ref[...].astype(F32)
    rows = lax.broadcasted_iota(jnp.int32, z.shape, 0)
    prev1 = carry_ref[7:8, :]
    prev2 = carry_ref[6:7, :]
    z1 = jnp.where(rows == 0, prev1, pltpu.roll(z, 1, 0))
    z2 = jnp.where(rows == 0, prev2, jnp.where(rows == 1, prev1, pltpu.roll(z, 2, 0)))
    cw = cw_ref[...]
    y = cw[0:1, :] * z2 + cw[1:2, :] * z1 + cw[2:3, :] * z
    ya = ab_ref[...].astype(F32) * y * _silu(ga_ref[...].astype(F32))
    ya_ref[...] = ya.astype(ya_ref.dtype)
    carry_ref[...] = z[ts - 8:ts, :]
    yield

    u = _gelu_tanh(du_ref[...].astype(F32))
    v = _gelu_tanh(dv_ref[...].astype(F32))
    mu = jnp.mean(v, axis=-1, keepdims=True)
    vc = v - mu
    var = jnp.mean(vc * vc, axis=-1, keepdims=True)
    vn = (vc * lax.rsqrt(var + LN_EPS) * lng_ref[...] + lnb_ref[...]).astype(BF16)
    gate = _silu(gd_ref[...].astype(F32))
    yield
    tri_r = lax.broadcasted_iota(jnp.int32, (SG_CHUNK, SG_CHUNK), 0)
    tri_c = lax.broadcasted_iota(jnp.int32, (SG_CHUNK, SG_CHUNK), 1)
    bst = bst_ref[...]
    for g in range(N_HEADS):
        w = jnp.where(tri_c <= tri_r, ws_ref[g], 0.0).astype(BF16)
        bias = bst[:, g:g + 1]
        lo = g * HEAD_W
        for c in range(ts // SG_CHUNK):
            r0 = c * SG_CHUNK
            sv = jnp.dot(w, vn[r0:r0 + SG_CHUNK, lo:lo + HEAD_W],
                         preferred_element_type=F32) + bias
            yd = u[r0:r0 + SG_CHUNK, lo:lo + HEAD_W] * sv * gate[r0:r0 + SG_CHUNK, lo:lo + HEAD_W]
            yd_ref[r0:r0 + SG_CHUNK, lo:lo + HEAD_W] = yd.astype(yd_ref.dtype)
        if g % 2 == 1:
            yield


def _hgrn_constants():
    c = HG_CHUNK
    t = np.arange(c)[:, None]
    s = np.arange(c)[None, :]
    mats = [(s <= t), (s > t)]
    for lv in HG_LEVELS:
        mid = (t // lv) * lv + lv // 2
        qside = t >= mid
        mats.append(np.where(qside, (s >= mid) & (s <= t), (s > t) & (s < mid)))
    rng = np.concatenate(mats, axis=0).astype(np.float32)
    rng = np.concatenate([rng, rng], axis=1)
    masks = np.stack([t == s] + [(t // lv) == (s // lv) for lv in HG_LEVELS]).astype(np.float32)
    masks = np.concatenate([masks, masks], axis=2)
    return jnp.asarray(rng, BF16), jnp.asarray(masks, F32)


def _hgrn_body(q_ref, f_ref, i_ref, g_ref, lb_ref, ng_ref, rng_ref, msk_ref,
               y_ref, st_ref, *, nsub):
    c = HG_CHUNK
    fz = f_ref[...].astype(F32)
    lb = lb_ref[...]
    e = jnp.exp(-jnp.abs(fz))
    r = 1.0 / (1.0 + e)
    pos = fz >= 0.0
    sig = jnp.where(pos, r, e * r)
    nsig = jnp.where(pos, e * r, r)
    f = lb + (1.0 - lb) * sig
    lg = jnp.log(jnp.maximum(f, F_FLOOR))
    kk_all = (1.0 - lb) * nsig

    g1 = lg.astype(BF16)
    g2 = (lg - g1.astype(F32)).astype(BF16)
    rng = rng_ref[...]
    exs = [jnp.dot(rng, jnp.concatenate([g1[u * c:(u + 1) * c], g2[u * c:(u + 1) * c]], axis=0),
                   preferred_element_type=F32)
           for u in range(nsub)]
    yield

    rows = lax.broadcasted_iota(jnp.int32, (c, HEAD_W), 0)
    zblk = jnp.zeros((c, HEAD_W), BF16)
    nt = (((1,), (1,)), ((), ()))

    def pair_rows(x0, x1):
        return jnp.concatenate([jnp.concatenate([x0, zblk], axis=1),
                                jnp.concatenate([zblk, x1], axis=1)], axis=0)

    for hp in range(N_HEADS // 2):
        los = (2 * hp * HEAD_W, (2 * hp + 1) * HEAD_W)
        sts = [st_ref[2 * hp], st_ref[2 * hp + 1]]
        for u in range(nsub):
            ex = exs[u]
            r0 = u * c
            qs, kks, vbs, outs = [], [], [], []
            for n, lo in enumerate(los):
                q = q_ref[r0:r0 + c, lo:lo + HEAD_W].astype(F32)
                vb = i_ref[r0:r0 + c, lo:lo + HEAD_W]
                kk = kk_all[r0:r0 + c, lo:lo + HEAD_W]
                bcum = ex[0:c, lo:lo + HEAD_W]
                q_in = (q * jnp.exp(bcum)).astype(BF16)
                outs.append(lax.dot_general(q_in, sts[n].astype(BF16), nt,
                                            preferred_element_type=F32))
                k_out = (kk * jnp.exp(ex[c:2 * c, lo:lo + HEAD_W])).astype(BF16)
                e_last = jnp.exp(bcum[c - 1:c, :])
                sts[n] = e_last * sts[n] + jnp.dot(vb.astype(F32).T.astype(BF16), k_out,
                                                   preferred_element_type=F32)
                qs.append(q)
                kks.append(kk)
                vbs.append(vb)
            yield

            a = msk_ref[0] * lax.dot_general(
                jnp.concatenate([qs[0].astype(BF16), qs[1].astype(BF16)], axis=1),
                pair_rows(kks[0].astype(BF16), kks[1].astype(BF16)), nt,
                preferred_element_type=F32)
            for li, lv in enumerate(HG_LEVELS):
                qside = (rows & (lv - 1)) >= (lv // 2)
                qls, kls = [], []
                for n, lo in enumerate(los):
                    el = jnp.exp(ex[(2 + li) * c:(3 + li) * c, lo:lo + HEAD_W])
                    qls.append(jnp.where(qside, qs[n] * el, 0.0).astype(BF16))
                    kls.append(jnp.where(qside, 0.0, kks[n] * el).astype(BF16))
                al = lax.dot_general(jnp.concatenate(qls, axis=1), pair_rows(kls[0], kls[1]), nt,
                                     preferred_element_type=F32)
                a = a + msk_ref[1 + li] * al
                if li % 2 == 0:
                    yield
            o2 = jnp.dot(a.astype(BF16), pair_rows(vbs[0], vbs[1]),
                         preferred_element_type=F32)

            for n, lo in enumerate(los):
                o = outs[n] + o2[:, n * HEAD_W:(n + 1) * HEAD_W]
                ms = jnp.mean(o * o, axis=-1, keepdims=True)
                on = o * lax.rsqrt(ms + RMS_EPS) * ng_ref[:, lo:lo + HEAD_W]
                y = on * _silu(g_ref[r0:r0 + c, lo:lo + HEAD_W].astype(F32))
                y_ref[r0:r0 + c, lo:lo + HEAD_W] = y.astype(y_ref.dtype)
            yield
        st_ref[2 * hp] = sts[0]
        st_ref[2 * hp + 1] = sts[1]


ATTN_LANES = 128
SUBLANES = 8


def _diff_attn_kernel(lam_ref, q_ref, k_ref, v_ref, g_ref, ngt_ref, o_ref,
                      vt_ref, qqt_ref, pa_ref, pb_ref, pt_ref, pu_ref,
                      sta_ref, stb_ref, stt_ref, stu_ref, acc_ref, *, tk, nq, out_scale):
    lam = lam_ref[0, 0]
    scale = DA_DQK ** -0.5
    tq = 2 * tk
    nl = 2 * tq
    nkv = vt_ref.shape[0]
    ngrp = tk // SUBLANES

    def vt_block(j, carry):
        for sub in range(tk // HEAD_W):
            rows = pl.ds(pl.multiple_of(j * tk + sub * HEAD_W, HEAD_W), HEAD_W)
            vt_ref[j, :, sub * HEAD_W:(sub + 1) * HEAD_W] = (
                v_ref[rows, :].astype(F32).T.astype(BF16))
        return carry

    lax.fori_loop(0, nkv, vt_block, 0)

    def scores(j, la=0, lb=nl):
        kb = k_ref[pl.ds(pl.multiple_of(j * tk, tk), tk), :]
        return jnp.dot(kb, qqt_ref[:, la:lb], preferred_element_type=F32)

    def accumulate(j, p_ref, st, la=0, lb=nl):
        alpha = jnp.concatenate([st[2, :, la:lb]] * (HEAD_W // SUBLANES), axis=0)
        acc_ref[:, la:lb] = (alpha * acc_ref[:, la:lb]
                             + jnp.dot(vt_ref[j], p_ref[:, la:lb], preferred_element_type=F32))

    def softmax(s_all, p_ref, st_in, st_out, key_lo=None, la=0, lb=nl):
        for c0 in range(la, lb, ATTN_LANES):
            cs = slice(c0, c0 + ATTN_LANES)
            s = s_all[:, c0 - la:c0 - la + ATTN_LANES]
            if key_lo is not None:
                q_lo = (c0 // (2 * tk)) * tk + c0 % tk
                if key_lo + tk - 1 > q_lo:
                    kidx = key_lo + lax.broadcasted_iota(jnp.int32, s.shape, 0)
                    qidx = q_lo + lax.broadcasted_iota(jnp.int32, s.shape, 1)
                    s = jnp.where(kidx <= qidx, s, MASK_VALUE)
            grp = [s[r * SUBLANES:(r + 1) * SUBLANES, :] for r in range(ngrp)]
            mx = functools.reduce(jnp.maximum, grp)
            m_blk = jnp.broadcast_to(jnp.max(mx, axis=0, keepdims=True), mx.shape)
            m_old = st_in[0, :, cs]
            m_new = jnp.maximum(m_old, m_blk)
            alpha = jnp.exp2(m_old - m_new)
            psum = None
            for r in range(0, ngrp, 2):
                p0 = jnp.exp2(grp[r] - m_new)
                p1 = jnp.exp2(grp[r + 1] - m_new)
                psum = p0 + p1 if psum is None else psum + p0 + p1
                p_ref[r * SUBLANES:(r + 2) * SUBLANES, cs] = (
                    jnp.concatenate([p0, p1], axis=0).astype(BF16))
            st_out[0, :, cs] = m_new
            st_out[1, :, cs] = alpha * st_in[1, :, cs] + psum
            st_out[2, :, cs] = alpha

    def step(j, p_in, p_out, st_in, st_out):
        s = scores(j)
        accumulate(jnp.maximum(j - 1, 0), p_in, st_in)
        softmax(s, p_out, st_in, st_out)

    def pair(jj, carry):
        step(2 * jj, pb_ref, pa_ref, stb_ref, sta_ref)
        step(2 * jj + 1, pa_ref, pb_ref, sta_ref, stb_ref)
        return carry

    def q_tile(i, carry):
        for half in range(2):
            for sub in range(tk // HEAD_W):
                rows = pl.ds(pl.multiple_of(i * tq + half * tk + sub * HEAD_W, HEAD_W), HEAD_W)
                qt = (q_ref[rows, :].astype(F32) * (scale * math.log2(math.e))).T
                d = lax.broadcasted_iota(jnp.int32, qt.shape, 0)
                zero = jnp.zeros_like(qt)
                l0 = 2 * half * tk + sub * HEAD_W
                qqt_ref[:, l0:l0 + HEAD_W] = jnp.where(d < DA_DQK, qt, zero).astype(BF16)
                qqt_ref[:, l0 + tk:l0 + tk + HEAD_W] = jnp.where(d >= DA_DQK, qt, zero).astype(BF16)
        acc_ref[...] = jnp.zeros_like(acc_ref)
        stb_ref[0] = jnp.full(stb_ref.shape[1:], MASK_VALUE, F32)
        stb_ref[1] = jnp.zeros(stb_ref.shape[1:], F32)
        stb_ref[2] = jnp.ones(stb_ref.shape[1:], F32)
        pb_ref[...] = jnp.zeros_like(pb_ref)

        lax.fori_loop(0, i, pair, 0)

        j0 = 2 * i
        s = scores(j0)
        accumulate(jnp.maximum(j0 - 1, 0), pb_ref, stb_ref)
        softmax(s, pt_ref, stb_ref, stt_ref, key_lo=0)
        s = scores(j0 + 1, tq, nl)
        accumulate(j0, pt_ref, stt_ref)
        softmax(s, pu_ref, stt_ref, stu_ref, key_lo=tk, la=tq, lb=nl)
        accumulate(j0 + 1, pu_ref, stu_ref, tq, nl)

        ngt = jnp.concatenate([ngt_ref[...]] * (tk // HEAD_W), axis=1)
        for half, st in enumerate((stt_ref, stu_ref)):
            l0 = 2 * half * tk
            l = jnp.sum(st[1, :, l0:l0 + 2 * tk], axis=0, keepdims=True)
            acc = acc_ref[:, l0:l0 + 2 * tk] / l
            o = acc[:, 0:tk] - lam * acc[:, tk:2 * tk]
            ms = jnp.mean(o * o, axis=0, keepdims=True)
            on = o * lax.rsqrt(ms + RMS_EPS) * ngt * out_scale
            for sub in range(tk // HEAD_W):
                rows = pl.ds(pl.multiple_of(i * tq + half * tk + sub * HEAD_W, HEAD_W), HEAD_W)
                blk = on[:, sub * HEAD_W:(sub + 1) * HEAD_W].T
                o_ref[rows, :] = (blk * _silu(g_ref[rows, :].astype(F32))).astype(o_ref.dtype)
        return carry

    lax.fori_loop(0, nq, q_tile, 0)


def _diff_attn(h, lam, lam_init, norm_g, batch, seq, tk=512):
    t = h.shape[0]
    tq = 2 * tk
    nq = seq // tq
    nkv = seq // tk
    per = GROUP_W // HEAD_W
    ngt = jnp.broadcast_to(norm_g.astype(F32).reshape(N_HEADS, HEAD_W, 1),
                           (N_HEADS, HEAD_W, HEAD_W))
    lam2 = jnp.reshape(lam, (1, 1)).astype(F32)
    kern = functools.partial(_diff_attn_kernel, tk=tk, nq=nq, out_scale=1.0 - lam_init)

    def head_cols(blk):
        return pl.BlockSpec((seq, HEAD_W), lambda b, hh, blk=blk: (b, blk * per + hh))

    return pl.pallas_call(
        kern,
        grid=(batch, N_HEADS),
        in_specs=[pl.BlockSpec(memory_space=pltpu.SMEM),
                  head_cols(C_Q), head_cols(C_K), head_cols(C_V), head_cols(G_C),
                  pl.BlockSpec((None, HEAD_W, HEAD_W), lambda b, hh: (hh, 0, 0))],
        out_specs=pl.BlockSpec((seq, HEAD_W), lambda b, hh: (b, hh)),
        out_shape=jax.ShapeDtypeStruct((t, GROUP_W), BF16),
        scratch_shapes=[pltpu.VMEM((nkv, HEAD_W, tk), BF16),
                        pltpu.VMEM((HEAD_W, 2 * tq), BF16),
                        *([pltpu.VMEM((tk, 2 * tq), BF16)] * 4),
                        *([pltpu.VMEM((3, SUBLANES, 2 * tq), F32)] * 4),
                        pltpu.VMEM((HEAD_W, 2 * tq), F32)],
        compiler_params=pltpu.CompilerParams(
            dimension_semantics=("arbitrary", "arbitrary"),
            vmem_limit_bytes=VMEM_LIMIT),
        name="diff_attn",
    )(lam2, h, h, h, h, ngt)


TAIL_ROWS = 256


def _tail_kernel(ab_ref, ac_ref, ax_ref, ga_ref, du_ref, dv_ref, gd_ref,
                 bq_ref, bf_ref, bi_ref, gb_ref, yc_ref, x_ref, p_ref,
                 cw_ref, sglng_ref, sglnb_ref, ws_ref, bst_ref,
                 hlb_ref, hng_ref, rng_ref, msk_ref,
                 wo_ref, wpg_ref, wpe_ref, lng_ref, lnb_ref,
                 xo_ref, xb_ref,
                 ya_ref, yb_ref, yd_ref, carry_ref, st_ref, *, seq, alpha):
    n = pl.program_id(0)

    @pl.when(n == 0)
    def _():
        ya_ref[...] = jnp.zeros_like(ya_ref)
        yb_ref[...] = jnp.zeros_like(yb_ref)
        yd_ref[...] = jnp.zeros_like(yd_ref)

    @pl.when((n * TAIL_ROWS) % seq == 0)
    def _():
        carry_ref[...] = jnp.zeros_like(carry_ref)
        st_ref[...] = jnp.zeros_like(st_ref)

    wslot = n % 2
    rslot = 1 - wslot

    def mixers():
        yield from _conv_sgu_body(ab_ref, ac_ref, ax_ref, ga_ref, du_ref, dv_ref, gd_ref,
                                  cw_ref, sglng_ref, sglnb_ref, ws_ref, bst_ref,
                                  ya_ref.at[wslot], yd_ref.at[wslot], carry_ref, ts=TAIL_ROWS)
        yield from _hgrn_body(bq_ref, bf_ref, bi_ref, gb_ref, hlb_ref, hng_ref, rng_ref, msk_ref,
                              yb_ref.at[wslot], st_ref, nsub=TAIL_ROWS // HG_CHUNK)

    def projections():
        d = x_ref.shape[-1]
        blocks = [slice(n0, n0 + GROUP_W) for n0 in range(0, d, GROUP_W)]
        ys = ((yc_ref, 2), (ya_ref.at[rslot], 0), (yd_ref.at[rslot], 3), (yb_ref.at[rslot], 1))
        accs = []
        for cols in blocks:
            acc = alpha * x_ref[:, cols]
            for y, g in ys:
                acc = acc + jnp.dot(y[...], wo_ref[g * GROUP_W:(g + 1) * GROUP_W, cols],
                                    preferred_element_type=F32)
                yield
            accs.append(acc)
        pe = jnp.dot(p_ref[...].astype(BF16), wpe_ref[...], preferred_element_type=F32)
        mu = sum(jnp.sum(a, axis=-1, keepdims=True) for a in accs) * (1.0 / d)
        xcs = [a - mu for a in accs]
        var = sum(jnp.sum(xc * xc, axis=-1, keepdims=True) for xc in xcs) * (1.0 / d)
        rstd = lax.rsqrt(var + LN_EPS)
        xns = [xc * rstd * lng_ref[:, cols] + lnb_ref[:, cols] for xc, cols in zip(xcs, blocks)]
        xnbs = [xn.astype(BF16) for xn in xns]
        yield
        for xn, cols in zip(xns, blocks):
            z = None
            for xnb, rows in zip(xnbs, blocks):
                part = jnp.dot(xnb, wpg_ref[rows, cols], preferred_element_type=F32)
                z = part if z is None else z + part
                yield
            out = xn + pe[:, cols] * _sigmoid(z)
            xo_ref[:, cols] = out
            xb_ref[:, cols] = out.astype(BF16)

    major, minor = mixers(), projections()
    live_major = live_minor = True
    while live_major or live_minor:
        live_major = live_major and next(major, False) is not False
        live_minor = live_minor and next(minor, False) is not False


def _tail_block(h, yc, x, p_all, small, wo_all, wpg_all, wpe_all, lng_all, lnb_all,
                layer, seq, alpha):
    t, d = x.shape
    tm = TAIL_ROWS
    pdim = p_all.shape[1]
    steps = t // tm

    def hblk(blk):
        return pl.BlockSpec((tm, GROUP_W), lambda i, blk=blk: (jnp.minimum(i, steps - 1), blk))

    def rows(w):
        return pl.BlockSpec((tm, w), lambda i: (jnp.maximum(i - 1, 0), 0))

    def full(a):
        nd = a.ndim
        return pl.BlockSpec(a.shape, lambda i, nd=nd: (0,) * nd)

    def resident(a):
        return pl.BlockSpec((None,) + a.shape[1:], lambda i: (layer, 0, 0),
                            pipeline_mode=pl.Buffered(1))

    h_blocks = (A_B, A_C, A_X, G_A, D_U, D_V, G_D, B_Q, B_F, B_I, G_B)
    kern = functools.partial(_tail_kernel, seq=seq, alpha=alpha)
    return pl.pallas_call(
        kern,
        grid=(steps + 1,),
        in_specs=[hblk(b) for b in h_blocks]
                 + [rows(GROUP_W), rows(d),
                    pl.BlockSpec((tm, pdim),
                                 lambda i: (layer * steps + jnp.maximum(i - 1, 0), 0))]
                 + [full(a) for a in small]
                 + [resident(wo_all), resident(wpg_all), resident(wpe_all),
                    resident(lng_all), resident(lnb_all)],
        out_specs=[rows(d), rows(d)],
        out_shape=[jax.ShapeDtypeStruct((t, d), F32), jax.ShapeDtypeStruct((t, d), BF16)],
        scratch_shapes=[pltpu.VMEM((2, tm, GROUP_W), BF16), pltpu.VMEM((2, tm, GROUP_W), BF16),
                        pltpu.VMEM((2, tm, GROUP_W), BF16),
                        pltpu.VMEM((8, GROUP_W), F32),
                        pltpu.VMEM((N_HEADS, HEAD_W, HEAD_W), F32)],
        compiler_params=pltpu.CompilerParams(
            dimension_semantics=("arbitrary",), vmem_limit_bytes=VMEM_LIMIT),
        name="mix_out",
    )(*([h] * len(h_blocks)), yc, x, p_all, *small,
      wo_all, wpg_all, wpe_all, lng_all, lnb_all)


def kernel(x, p, w_in, conv_w, hgrn_lb, hgrn_norm_g, diff_lambda, diff_norm_g,
           sg_ln_g, sg_ln_b, sg_w, sg_b, w_out, ln_g, ln_b, w_pe, w_pg):
    batch, seq, d_model = x.shape
    depth = w_in.shape[0]
    t = batch * seq
    alpha = (2 * depth) ** 0.25

    lb_sm = jax.nn.softmax(hgrn_lb.astype(F32), axis=0)
    lower_bounds = jnp.cumsum(lb_sm, axis=0) - lb_sm[0]

    xf = x.reshape(t, d_model)
    xb = xf
    p_all = p.reshape(depth * t, p.shape[-1])
    wo_all, wpg_all, wpe_all = w_out.astype(BF16), w_pg.astype(BF16), w_pe.astype(BF16)
    lng_all = ln_g.reshape(depth, 1, d_model).astype(F32)
    lnb_all = ln_b.reshape(depth, 1, d_model).astype(F32)
    rng, masks = _hgrn_constants()
    for i in range(depth):
        lam_init = 0.8 - 0.6 * math.exp(-0.3 * i)
        dl = diff_lambda[i].astype(F32)
        lam = (jnp.exp(jnp.sum(dl[0] * dl[1])) - jnp.exp(jnp.sum(dl[2] * dl[3])) + lam_init)

        h = _in_proj(xb, w_in, i)
        yc = _diff_attn(h, lam, lam_init, diff_norm_g[i], batch, seq)
        small = (conv_w[i].astype(F32),
                 sg_ln_g[i].reshape(1, GROUP_W).astype(F32), sg_ln_b[i].reshape(1, GROUP_W).astype(F32),
                 sg_w[i].astype(F32), sg_b[i].T.astype(F32),
                 lower_bounds[i].reshape(1, GROUP_W), hgrn_norm_g[i].reshape(1, GROUP_W).astype(F32),
                 rng, masks)
        xf, xb = _tail_block(h, yc, xf, p_all, small, wo_all, wpg_all, wpe_all,
                             lng_all, lnb_all, i, seq, alpha)
    return xf.reshape(batch, seq, d_model)
```

```python
import functools
import math

import numpy as np
import jax
import jax.numpy as jnp
from jax import lax
from jax.experimental import pallas as pl
from jax.experimental.pallas import tpu as pltpu

F32 = jnp.float32
BF16 = jnp.bfloat16

GROUP_W = 512
HEAD_W = 128
N_HEADS = GROUP_W // HEAD_W
DA_DQK = 64
F_FLOOR = 1e-30
MASK_VALUE = -1e30
LN_EPS = 1e-5
RMS_EPS = 1e-6

A_B, A_C, A_X, B_Q, B_F, B_I, C_Q, C_K, C_V, D_U, D_V, G_A, G_B, G_C, G_D = range(15)

HG_CHUNK = 128
HG_LEVELS = (2, 4, 8, 16, 32, 64, 128)
SG_CHUNK = 128
VMEM_LIMIT = 56 * 1024 * 1024


def _sigmoid(x):
    return 1.0 / (1.0 + jnp.exp(-x))


def _silu(x):
    return x * _sigmoid(x)


def _gelu_tanh(x):
    c = math.sqrt(2.0 / math.pi)
    return 0.5 * x * (1.0 + jnp.tanh(c * (x + 0.044715 * (x * x * x))))


CAST_COLS = 128


def _matmul_kernel(x_ref, w_ref, wo_ref, wpg_ref, wpe_ref, o_ref, wo_o, wpg_o, wpe_o):
    o_ref[...] = jnp.dot(x_ref[...].astype(BF16), w_ref[...].astype(BF16),
                         preferred_element_type=F32).astype(o_ref.dtype)
    wo_o[...] = wo_ref[...].astype(BF16)
    wpg_o[...] = wpg_ref[...].astype(BF16)
    wpe_o[...] = wpe_ref[...].astype(BF16)


def _in_proj(xb, w_all, wo_all, wpg_all, wpe_all, layer, tm=2048, tn=512):
    t, k = xb.shape
    n = w_all.shape[2]
    d = wo_all.shape[2]
    nj = n // tn
    last = d // CAST_COLS - 1
    assert (t // tm) * nj > last, "too few grid steps to cast every weight column block"

    def cast_blk(i, j):
        return jnp.minimum(i * nj + j, last)

    def w_in_spec(a):
        return pl.BlockSpec((None, a.shape[1], CAST_COLS), lambda i, j: (layer, 0, cast_blk(i, j)))

    def w_out_spec(a):
        return pl.BlockSpec((a.shape[1], CAST_COLS), lambda i, j: (0, cast_blk(i, j)))

    return pl.pallas_call(
        _matmul_kernel,
        grid=(t // tm, nj),
        in_specs=[pl.BlockSpec((tm, k), lambda i, j: (i, 0)),
                  pl.BlockSpec((None, k, tn), lambda i, j: (layer, 0, j)),
                  w_in_spec(wo_all), w_in_spec(wpg_all), w_in_spec(wpe_all)],
        out_specs=[pl.BlockSpec((tm, tn), lambda i, j: (i, j)),
                   w_out_spec(wo_all), w_out_spec(wpg_all), w_out_spec(wpe_all)],
        out_shape=[jax.ShapeDtypeStruct((t, n), BF16)]
                  + [jax.ShapeDtypeStruct(a.shape[1:], BF16) for a in (wo_all, wpg_all, wpe_all)],
        compiler_params=pltpu.CompilerParams(
            dimension_semantics=("arbitrary", "arbitrary"),
            vmem_limit_bytes=VMEM_LIMIT),
        name="in_proj",
    )(xb, w_all, wo_all, wpg_all, wpe_all)


def _conv_sgu_body(ab_ref, ac_ref, ax_ref, ga_ref, du_ref, dv_ref, gd_ref,
                   cw_ref, lng_ref, lnb_ref, ws_ref, bst_ref,
                   ya_ref, yd_ref, carry_ref, *, ts):
    z = ac_ref[...].astype(F32) * ax_ref[...].astype(F32)
    rows = lax.broadcasted_iota(jnp.int32, z.shape, 0)
    prev1 = carry_ref[7:8, :]
    prev2 = carry_ref[6:7, :]
    z1 = jnp.where(rows == 0, prev1, pltpu.roll(z, 1, 0))
    z2 = jnp.where(rows == 0, prev2, jnp.where(rows == 1, prev1, pltpu.roll(z, 2, 0)))
    cw = cw_ref[...]
    y = cw[0:1, :] * z2 + cw[1:2, :] * z1 + cw[2:3, :] * z
    ya = ab_ref[...].astype(F32) * y * _silu(ga_ref[...].astype(F32))
    ya_ref[...] = ya.astype(ya_ref.dtype)
    carry_ref[...] = z[ts - 8:ts, :]
    yield

    u = _gelu_tanh(du_ref[...].astype(F32))
    v = _gelu_tanh(dv_ref[...].astype(F32))
    mu = jnp.mean(v, axis=-1, keepdims=True)
    vc = v - mu
    var = jnp.mean(vc * vc, axis=-1, keepdims=True)
    vn = (vc * lax.rsqrt(var + LN_EPS) * lng_ref[...] + lnb_ref[...]).astype(BF16)
    gate = _silu(gd_ref[...].astype(F32))
    yield
    tri_r = lax.broadcasted_iota(jnp.int32, (SG_CHUNK, SG_CHUNK), 0)
    tri_c = lax.broadcasted_iota(jnp.int32, (SG_CHUNK, SG_CHUNK), 1)
    bst = bst_ref[...]
    for g in range(N_HEADS):
        w = jnp.where(tri_c <= tri_r, ws_ref[g], 0.0).astype(BF16)
        bias = bst[:, g:g + 1]
        lo = g * HEAD_W
        for c in range(ts // SG_CHUNK):
            r0 = c * SG_CHUNK
            sv = jnp.dot(w, vn[r0:r0 + SG_CHUNK, lo:lo + HEAD_W],
                         preferred_element_type=F32) + bias
            yd = u[r0:r0 + SG_CHUNK, lo:lo + HEAD_W] * sv * gate[r0:r0 + SG_CHUNK, lo:lo + HEAD_W]
            yd_ref[r0:r0 + SG_CHUNK, lo:lo + HEAD_W] = yd.astype(yd_ref.dtype)
        if g % 2 == 1:
            yield


def _hgrn_constants():
    c = HG_CHUNK
    t = np.arange(c)[:, None]
    s = np.arange(c)[None, :]
    mats = [(s <= t), (s > t)]
    for lv in HG_LEVELS:
        mid = (t // lv) * lv + lv // 2
        qside = t >= mid
        mats.append(np.where(qside, (s >= mid) & (s <= t), (s > t) & (s < mid)))
    rng = np.concatenate(mats, axis=0).astype(np.float32)
    rng = np.concatenate([rng, rng], axis=1)
    masks = np.stack([t == s] + [(t // lv) == (s // lv) for lv in HG_LEVELS]).astype(np.float32)
    masks = np.concatenate([masks, masks], axis=2)
    return jnp.asarray(rng, BF16), jnp.asarray(masks, F32)


def _hgrn_body(q_ref, f_ref, i_ref, g_ref, lb_ref, ng_ref, rng_ref, msk_ref,
               y_ref, st_ref, *, nsub):
    c = HG_CHUNK
    fz = f_ref[...].astype(F32)
    lb = lb_ref[...]
    e = jnp.exp(-jnp.abs(fz))
    r = 1.0 / (1.0 + e)
    pos = fz >= 0.0
    sig = jnp.where(pos, r, e * r)
    nsig = jnp.where(pos, e * r, r)
    f = lb + (1.0 - lb) * sig
    lg = jnp.log(jnp.maximum(f, F_FLOOR))
    kk_all = (1.0 - lb) * nsig

    g1 = lg.astype(BF16)
    g2 = (lg - g1.astype(F32)).astype(BF16)
    rng = rng_ref[...]
    exs = [jnp.dot(rng, jnp.concatenate([g1[u * c:(u + 1) * c], g2[u * c:(u + 1) * c]], axis=0),
                   preferred_element_type=F32)
           for u in range(nsub)]
    yield

    rows = lax.broadcasted_iota(jnp.int32, (c, HEAD_W), 0)
    zblk = jnp.zeros((c, HEAD_W), BF16)
    nt = (((1,), (1,)), ((), ()))

    def pair_rows(x0, x1):
        return jnp.concatenate([jnp.concatenate([x0, zblk], axis=1),
                                jnp.concatenate([zblk, x1], axis=1)], axis=0)

    for hp in range(N_HEADS // 2):
        los = (2 * hp * HEAD_W, (2 * hp + 1) * HEAD_W)
        sts = [st_ref[2 * hp], st_ref[2 * hp + 1]]
        for u in range(nsub):
            ex = exs[u]
            r0 = u * c
            qs, kks, vbs, outs = [], [], [], []
            for n, lo in enumerate(los):
                q = q_ref[r0:r0 + c, lo:lo + HEAD_W].astype(F32)
                vb = i_ref[r0:r0 + c, lo:lo + HEAD_W]
                kk = kk_all[r0:r0 + c, lo:lo + HEAD_W]
                bcum = ex[0:c, lo:lo + HEAD_W]
                q_in = (q * jnp.exp(bcum)).astype(BF16)
                outs.append(lax.dot_general(q_in, sts[n].astype(BF16), nt,
                                            preferred_element_type=F32))
                k_out = (kk * jnp.exp(ex[c:2 * c, lo:lo + HEAD_W])).astype(BF16)
                e_last = jnp.exp(bcum[c - 1:c, :])
                sts[n] = e_last * sts[n] + jnp.dot(vb.astype(F32).T.astype(BF16), k_out,
                                                   preferred_element_type=F32)
                qs.append(q)
                kks.append(kk)
                vbs.append(vb)
            yield

            a = msk_ref[0] * lax.dot_general(
                jnp.concatenate([qs[0].astype(BF16), qs[1].astype(BF16)], axis=1),
                pair_rows(kks[0].astype(BF16), kks[1].astype(BF16)), nt,
                preferred_element_type=F32)
            for li, lv in enumerate(HG_LEVELS):
                qside = (rows & (lv - 1)) >= (lv // 2)
                qls, kls = [], []
                for n, lo in enumerate(los):
                    el = jnp.exp(ex[(2 + li) * c:(3 + li) * c, lo:lo + HEAD_W])
                    qls.append(jnp.where(qside, qs[n] * el, 0.0).astype(BF16))
                    kls.append(jnp.where(qside, 0.0, kks[n] * el).astype(BF16))
                al = lax.dot_general(jnp.concatenate(qls, axis=1), pair_rows(kls[0], kls[1]), nt,
                                     preferred_element_type=F32)
                a = a + msk_ref[1 + li] * al
                if li % 2 == 0:
                    yield
            o2 = jnp.dot(a.astype(BF16), pair_rows(vbs[0], vbs[1]),
                         preferred_element_type=F32)

            for n, lo in enumerate(los):
                o = outs[n] + o2[:, n * HEAD_W:(n + 1) * HEAD_W]
                ms = jnp.mean(o * o, axis=-1, keepdims=True)
                on = o * lax.rsqrt(ms + RMS_EPS) * ng_ref[:, lo:lo + HEAD_W]
                y = on * _silu(g_ref[r0:r0 + c, lo:lo + HEAD_W].astype(F32))
                y_ref[r0:r0 + c, lo:lo + HEAD_W] = y.astype(y_ref.dtype)
            yield
        st_ref[2 * hp] = sts[0]
        st_ref[2 * hp + 1] = sts[1]


ATTN_ROWS = 64


def _diff_attn_kernel(lam_ref, q_ref, k_ref, v_ref, g_ref, ng_ref, o_ref,
                      qq_ref, pa_ref, pb_ref, pt_ref, pu_ref,
                      ma_ref, la_ref, aa_ref, mb_ref, lb_ref, ab_ref,
                      mt_ref, lt_ref, at_ref, mu_ref, lu_ref, au_ref,
                      acc_ref, *, tk, nq, out_scale):
    lam = lam_ref[0, 0]
    scale = DA_DQK ** -0.5
    tq = 2 * tk
    nrow = 2 * tq
    ncb = tk // HEAD_W
    st_a = (ma_ref, la_ref, aa_ref)
    st_b = (mb_ref, lb_ref, ab_ref)
    st_t = (mt_ref, lt_ref, at_ref)
    st_u = (mu_ref, lu_ref, au_ref)

    def scores(j, ra=0, rb=nrow):
        kb = k_ref[pl.ds(pl.multiple_of(j * tk, tk), tk), :]
        return lax.dot_general(qq_ref[ra:rb, :], kb, (((1,), (1,)), ((), ())),
                               preferred_element_type=F32)

    def accumulate(j, p_ref, st, ra=0, rb=nrow):
        vb = v_ref[pl.ds(pl.multiple_of(j * tk, tk), tk), :]
        acc_ref[ra:rb, :] = (st[2][ra:rb, :] * acc_ref[ra:rb, :]
                             + jnp.dot(p_ref[ra:rb, :], vb, preferred_element_type=F32))

    def softmax(s_all, p_ref, st_in, st_out, diag=None, ra=0, rb=nrow):
        masked = diag is not None
        for r0 in range(ra, rb, ATTN_ROWS):
            rs = slice(r0, r0 + ATTN_ROWS)
            row_lo = (r0 // (2 * tk)) * tk + r0 % tk
            key_lo = diag * tk if masked else 0
            tiles = []
            for cb in range(ncb):
                k0 = key_lo + cb * HEAD_W
                if masked and k0 > row_lo + ATTN_ROWS - 1:
                    tiles.append(None)
                    continue
                s = s_all[r0 - ra:r0 - ra + ATTN_ROWS, cb * HEAD_W:(cb + 1) * HEAD_W]
                if masked and k0 + HEAD_W - 1 > row_lo:
                    r = row_lo + lax.broadcasted_iota(jnp.int32, s.shape, 0)
                    cidx = k0 + lax.broadcasted_iota(jnp.int32, s.shape, 1)
                    s = jnp.where(cidx <= r, s, MASK_VALUE)
                tiles.append(s)
            live = [s for s in tiles if s is not None]
            mx = functools.reduce(jnp.maximum, live)
            m_blk = jnp.broadcast_to(jnp.max(mx, axis=-1, keepdims=True), mx.shape)
            m_old = st_in[0][rs, :]
            m_new = jnp.maximum(m_old, m_blk)
            alpha = jnp.exp2(m_old - m_new)
            psum = None
            for cb, s in enumerate(tiles):
                if s is None:
                    p_ref[rs, cb * HEAD_W:(cb + 1) * HEAD_W] = jnp.zeros((ATTN_ROWS, HEAD_W), BF16)
                    continue
                p = jnp.exp2(s - m_new)
                psum = p if psum is None else psum + p
                p_ref[rs, cb * HEAD_W:(cb + 1) * HEAD_W] = p.astype(BF16)
            st_out[0][rs, :] = m_new
            st_out[1][rs, :] = alpha * st_in[1][rs, :] + psum
            st_out[2][rs, :] = alpha

    def step(j, p_in, p_out, st_in, st_out):
        s = scores(j)
        accumulate(jnp.maximum(j - 1, 0), p_in, st_in)
        softmax(s, p_out, st_in, st_out)

    def pair(jj, carry):
        step(2 * jj, pb_ref, pa_ref, st_b, st_a)
        step(2 * jj + 1, pa_ref, pb_ref, st_a, st_b)
        return carry

    def q_tile(i, carry):
        for half in range(2):
            rows = pl.ds(pl.multiple_of(i * tq + half * tk, tk), tk)
            q = (q_ref[rows, :].astype(F32) * (scale * math.log2(math.e))).astype(BF16)
            lane = lax.broadcasted_iota(jnp.int32, q.shape, 1)
            zero = jnp.zeros_like(q)
            qq_ref[2 * half * tk:(2 * half + 1) * tk, :] = jnp.where(lane < DA_DQK, q, zero)
            qq_ref[(2 * half + 1) * tk:(2 * half + 2) * tk, :] = jnp.where(lane >= DA_DQK, q, zero)
        acc_ref[...] = jnp.zeros_like(acc_ref)
        mb_ref[...] = jnp.full_like(mb_ref, MASK_VALUE)
        lb_ref[...] = jnp.zeros_like(lb_ref)
        ab_ref[...] = jnp.ones_like(ab_ref)
        pb_ref[...] = jnp.zeros_like(pb_ref)

        lax.fori_loop(0, i, pair, 0)

        j0 = 2 * i
        s = scores(j0)
        accumulate(jnp.maximum(j0 - 1, 0), pb_ref, st_b)
        softmax(s, pt_ref, st_b, st_t, diag=0)
        s = scores(j0 + 1, tq, nrow)
        accumulate(j0, pt_ref, st_t)
        softmax(s, pu_ref, st_t, st_u, diag=1, ra=tq, rb=nrow)
        accumulate(j0 + 1, pu_ref, st_u, tq, nrow)

        for half, st in enumerate((st_t, st_u)):
            r0 = 2 * half * tk
            l = jnp.sum(st[1][r0:r0 + 2 * tk, :], axis=-1, keepdims=True)
            acc = acc_ref[r0:r0 + 2 * tk, :] / l
            o = acc[0:tk] - lam * acc[tk:2 * tk]
            ms = jnp.mean(o * o, axis=-1, keepdims=True)
            on = o * lax.rsqrt(ms + RMS_EPS) * ng_ref[...] * out_scale
            rows = pl.ds(pl.multiple_of(i * tq + half * tk, tk), tk)
            o_ref[rows, :] = (on * _silu(g_ref[rows, :].astype(F32))).astype(o_ref.dtype)
        return carry

    lax.fori_loop(0, nq, q_tile, 0)


def _diff_attn(h, lam, lam_init, norm_g, batch, seq, tk=512):
    t = h.shape[0]
    tq = 2 * tk
    nq = seq // tq
    per = GROUP_W // HEAD_W
    ng2 = norm_g.reshape(1, GROUP_W).astype(F32)
    lam2 = jnp.reshape(lam, (1, 1)).astype(F32)
    kern = functools.partial(_diff_attn_kernel, tk=tk, nq=nq, out_scale=1.0 - lam_init)

    def head_cols(blk):
        return pl.BlockSpec((seq, HEAD_W), lambda b, hh, blk=blk: (b, blk * per + hh))

    return pl.pallas_call(
        kern,
        grid=(batch, N_HEADS),
        in_specs=[pl.BlockSpec(memory_space=pltpu.SMEM),
                  head_cols(C_Q), head_cols(C_K), head_cols(C_V), head_cols(G_C),
                  pl.BlockSpec((1, HEAD_W), lambda b, hh: (0, hh))],
        out_specs=pl.BlockSpec((seq, HEAD_W), lambda b, hh: (b, hh)),
        out_shape=jax.ShapeDtypeStruct((t, GROUP_W), BF16),
        scratch_shapes=[pltpu.VMEM((2 * tq, HEAD_W), BF16),
                        *([pltpu.VMEM((2 * tq, tk), BF16)] * 4),
                        *([pltpu.VMEM((2 * tq, HEAD_W), F32)] * 13)],
        compiler_params=pltpu.CompilerParams(
            dimension_semantics=("arbitrary", "arbitrary"),
            vmem_limit_bytes=VMEM_LIMIT),
        name="diff_attn",
    )(lam2, h, h, h, h, ng2)


TAIL_ROWS = 256


def _tail_kernel(ab_ref, ac_ref, ax_ref, ga_ref, du_ref, dv_ref, gd_ref,
                 bq_ref, bf_ref, bi_ref, gb_ref, yc_ref, x_ref, p_ref,
                 cw_ref, sglng_ref, sglnb_ref, ws_ref, bst_ref,
                 hlb_ref, hng_ref, rng_ref, msk_ref,
                 wo_ref, wpg_ref, wpe_ref, lng_ref, lnb_ref,
                 xo_ref, xb_ref,
                 ya_ref, yb_ref, yd_ref, carry_ref, st_ref, *, seq, alpha):
    n = pl.program_id(0)

    @pl.when(n == 0)
    def _():
        ya_ref[...] = jnp.zeros_like(ya_ref)
        yb_ref[...] = jnp.zeros_like(yb_ref)
        yd_ref[...] = jnp.zeros_like(yd_ref)

    @pl.when((n * TAIL_ROWS) % seq == 0)
    def _():
        carry_ref[...] = jnp.zeros_like(carry_ref)
        st_ref[...] = jnp.zeros_like(st_ref)

    wslot = n % 2
    rslot = 1 - wslot

    def mixers():
        yield from _conv_sgu_body(ab_ref, ac_ref, ax_ref, ga_ref, du_ref, dv_ref, gd_ref,
                                  cw_ref, sglng_ref, sglnb_ref, ws_ref, bst_ref,
                                  ya_ref.at[wslot], yd_ref.at[wslot], carry_ref, ts=TAIL_ROWS)
        yield from _hgrn_body(bq_ref, bf_ref, bi_ref, gb_ref, hlb_ref, hng_ref, rng_ref, msk_ref,
                              yb_ref.at[wslot], st_ref, nsub=TAIL_ROWS // HG_CHUNK)

    def projections():
        d = x_ref.shape[-1]
        blocks = [slice(n0, n0 + GROUP_W) for n0 in range(0, d, GROUP_W)]
        ys = ((yc_ref, 2), (ya_ref.at[rslot], 0), (yd_ref.at[rslot], 3), (yb_ref.at[rslot], 1))
        accs = []
        for cols in blocks:
            acc = alpha * x_ref[:, cols]
            for y, g in ys:
                acc = acc + jnp.dot(y[...], wo_ref[g * GROUP_W:(g + 1) * GROUP_W, cols],
                                    preferred_element_type=F32)
                yield
            accs.append(acc)
        pe = jnp.dot(p_ref[...].astype(BF16), wpe_ref[...], preferred_element_type=F32)
        mu = sum(jnp.sum(a, axis=-1, keepdims=True) for a in accs) * (1.0 / d)
        xcs = [a - mu for a in accs]
        var = sum(jnp.sum(xc * xc, axis=-1, keepdims=True) for xc in xcs) * (1.0 / d)
        rstd = lax.rsqrt(var + LN_EPS)
        xns = [xc * rstd * lng_ref[:, cols] + lnb_ref[:, cols] for xc, cols in zip(xcs, blocks)]
        xnbs = [xn.astype(BF16) for xn in xns]
        yield
        for xn, cols in zip(xns, blocks):
            z = None
            for xnb, rows in zip(xnbs, blocks):
                part = jnp.dot(xnb, wpg_ref[rows, cols], preferred_element_type=F32)
                z = part if z is None else z + part
                yield
            out = xn + pe[:, cols] * _sigmoid(z)
            xo_ref[:, cols] = out
            xb_ref[:, cols] = out.astype(BF16)

    major, minor = mixers(), projections()
    live_major = live_minor = True
    while live_major or live_minor:
        live_major = live_major and next(major, False) is not False
        live_minor = live_minor and next(minor, False) is not False


def _tail_block(h, yc, x, p_all, small, wo_all, wpg_all, wpe_all, lng_all, lnb_all,
                layer, seq, alpha):
    t, d = x.shape
    tm = TAIL_ROWS
    pdim = p_all.shape[1]
    steps = t // tm

    def hblk(blk):
        return pl.BlockSpec((tm, GROUP_W), lambda i, blk=blk: (jnp.minimum(i, steps - 1), blk))

    def rows(w):
        return pl.BlockSpec((tm, w), lambda i: (jnp.maximum(i - 1, 0), 0))

    def full(a):
        nd = a.ndim
        return pl.BlockSpec(a.shape, lambda i, nd=nd: (0,) * nd)

    def resident(a):
        if a.ndim == 2:
            return pl.BlockSpec(a.shape, lambda i: (0, 0), pipeline_mode=pl.Buffered(1))
        return pl.BlockSpec((None,) + a.shape[1:], lambda i: (layer, 0, 0),
                            pipeline_mode=pl.Buffered(1))

    h_blocks = (A_B, A_C, A_X, G_A, D_U, D_V, G_D, B_Q, B_F, B_I, G_B)
    kern = functools.partial(_tail_kernel, seq=seq, alpha=alpha)
    return pl.pallas_call(
        kern,
        grid=(steps + 1,),
        in_specs=[hblk(b) for b in h_blocks]
                 + [rows(GROUP_W), rows(d),
                    pl.BlockSpec((tm, pdim),
                                 lambda i: (layer * steps + jnp.maximum(i - 1, 0), 0))]
                 + [full(a) for a in small]
                 + [resident(wo_all), resident(wpg_all), resident(wpe_all),
                    resident(lng_all), resident(lnb_all)],
        out_specs=[rows(d), rows(d)],
        out_shape=[jax.ShapeDtypeStruct((t, d), F32), jax.ShapeDtypeStruct((t, d), BF16)],
        scratch_shapes=[pltpu.VMEM((2, tm, GROUP_W), BF16), pltpu.VMEM((2, tm, GROUP_W), BF16),
                        pltpu.VMEM((2, tm, GROUP_W), BF16),
                        pltpu.VMEM((8, GROUP_W), F32),
                        pltpu.VMEM((N_HEADS, HEAD_W, HEAD_W), F32)],
        compiler_params=pltpu.CompilerParams(
            dimension_semantics=("arbitrary",), vmem_limit_bytes=VMEM_LIMIT),
        name="mix_out",
    )(*([h] * len(h_blocks)), yc, x, p_all, *small,
      wo_all, wpg_all, wpe_all, lng_all, lnb_all)


def kernel(x, p, w_in, conv_w, hgrn_lb, hgrn_norm_g, diff_lambda, diff_norm_g,
           sg_ln_g, sg_ln_b, sg_w, sg_b, w_out, ln_g, ln_b, w_pe, w_pg):
    batch, seq, d_model = x.shape
    depth = w_in.shape[0]
    t = batch * seq
    alpha = (2 * depth) ** 0.25

    lb_sm = jax.nn.softmax(hgrn_lb.astype(F32), axis=0)
    lower_bounds = jnp.cumsum(lb_sm, axis=0) - lb_sm[0]

    xf = x.reshape(t, d_model)
    xb = xf
    p_all = p.reshape(depth * t, p.shape[-1])
    lng_all = ln_g.reshape(depth, 1, d_model).astype(F32)
    lnb_all = ln_b.reshape(depth, 1, d_model).astype(F32)
    rng, masks = _hgrn_constants()
    for i in range(depth):
        lam_init = 0.8 - 0.6 * math.exp(-0.3 * i)
        dl = diff_lambda[i].astype(F32)
        lam = (jnp.exp(jnp.sum(dl[0] * dl[1])) - jnp.exp(jnp.sum(dl[2] * dl[3])) + lam_init)

        h, wo_b, wpg_b, wpe_b = _in_proj(xb, w_in, w_out, w_pg, w_pe, i)
        yc = _diff_attn(h, lam, lam_init, diff_norm_g[i], batch, seq)
        small = (conv_w[i].astype(F32),
                 sg_ln_g[i].reshape(1, GROUP_W).astype(F32), sg_ln_b[i].reshape(1, GROUP_W).astype(F32),
                 sg_w[i].astype(F32), sg_b[i].T.astype(F32),
                 lower_bounds[i].reshape(1, GROUP_W), hgrn_norm_g[i].reshape(1, GROUP_W).astype(F32),
                 rng, masks)
        xf, xb = _tail_block(h, yc, xf, p_all, small, wo_b, wpg_b, wpe_b,
                             lng_all, lnb_all, i, seq, alpha)
    return xf.reshape(batch, seq, d_model)
```

```python
import functools
import math

import numpy as np
import jax
import jax.numpy as jnp
from jax import lax
from jax.experimental import pallas as pl
from jax.experimental.pallas import tpu as pltpu

F32 = jnp.float32
BF16 = jnp.bfloat16

GROUP_W = 512
HEAD_W = 128
N_HEADS = GROUP_W // HEAD_W
DA_DQK = 64
F_FLOOR = 1e-30
MASK_VALUE = -1e30
LN_EPS = 1e-5
RMS_EPS = 1e-6

A_B, A_C, A_X, B_Q, B_F, B_I, C_Q, C_K, C_V, D_U, D_V, G_A, G_B, G_C, G_D = range(15)

HG_CHUNK = 128
HG_LEVELS = (2, 4, 8, 16, 32, 64, 128)
SG_CHUNK = 128
VMEM_LIMIT = 56 * 1024 * 1024


def _sigmoid(x):
    return 1.0 / (1.0 + jnp.exp(-x))


def _silu(x):
    return x * _sigmoid(x)


def _gelu_tanh(x):
    c = math.sqrt(2.0 / math.pi)
    return 0.5 * x * (1.0 + jnp.tanh(c * (x + 0.044715 * (x * x * x))))


def _matmul_kernel(x_ref, w_ref, o_ref):
    o_ref[...] = jnp.dot(x_ref[...].astype(BF16), w_ref[...].astype(BF16),
                         preferred_element_type=F32).astype(o_ref.dtype)


def _in_proj(xb, w_all, layer, tm=2048, tn=512):
    t, k = xb.shape
    n = w_all.shape[2]
    return pl.pallas_call(
        _matmul_kernel,
        grid=(t // tm, n // tn),
        in_specs=[pl.BlockSpec((tm, k), lambda i, j: (i, 0)),
                  pl.BlockSpec((None, k, tn), lambda i, j: (layer, 0, j))],
        out_specs=pl.BlockSpec((tm, tn), lambda i, j: (i, j)),
        out_shape=jax.ShapeDtypeStruct((t, n), BF16),
        compiler_params=pltpu.CompilerParams(
            dimension_semantics=("arbitrary", "arbitrary"),
            vmem_limit_bytes=VMEM_LIMIT),
        name="in_proj",
    )(xb, w_all)


def _conv_sgu_body(ab_ref, ac_ref, ax_ref, ga_ref, du_ref, dv_ref, gd_ref,
                   cw_ref, lng_ref, lnb_ref, ws_ref, bst_ref,
                   ya_ref, yd_ref, carry_ref, *, ts):
    z = ac_ref[...].astype(F32) * ax_ref[...].astype(F32)
    rows = lax.broadcasted_iota(jnp.int32, z.shape, 0)
    prev1 = carry_ref[7:8, :]
    prev2 = carry_ref[6:7, :]
    z1 = jnp.where(rows == 0, prev1, pltpu.roll(z, 1, 0))
    z2 = jnp.where(rows == 0, prev2, jnp.where(rows == 1, prev1, pltpu.roll(z, 2, 0)))
    cw = cw_ref[...]
    y = cw[0:1, :] * z2 + cw[1:2, :] * z1 + cw[2:3, :] * z
    ya = ab_ref[...].astype(F32) * y * _silu(ga_ref[...].astype(F32))
    ya_ref[...] = ya.astype(ya_ref.dtype)
    carry_ref[...] = z[ts - 8:ts, :]
    yield

    u = _gelu_tanh(du_ref[...].astype(F32))
    v = _gelu_tanh(dv_ref[...].astype(F32))
    mu = jnp.mean(v, axis=-1, keepdims=True)
    vc = v - mu
    var = jnp.mean(vc * vc, axis=-1, keepdims=True)
    vn = (vc * lax.rsqrt(var + LN_EPS) * lng_ref[...] + lnb_ref[...]).astype(BF16)
    gate = _silu(gd_ref[...].astype(F32))
    yield
    tri_r = lax.broadcasted_iota(jnp.int32, (SG_CHUNK, SG_CHUNK), 0)
    tri_c = lax.broadcasted_iota(jnp.int32, (SG_CHUNK, SG_CHUNK), 1)
    bst = bst_ref[...]
    for g in range(N_HEADS):
        w = jnp.where(tri_c <= tri_r, ws_ref[g], 0.0).astype(BF16)
        bias = bst[:, g:g + 1]
        lo = g * HEAD_W
        for c in range(ts // SG_CHUNK):
            r0 = c * SG_CHUNK
            sv = jnp.dot(w, vn[r0:r0 + SG_CHUNK, lo:lo + HEAD_W],
                         preferred_element_type=F32) + bias
            yd = u[r0:r0 + SG_CHUNK, lo:lo + HEAD_W] * sv * gate[r0:r0 + SG_CHUNK, lo:lo + HEAD_W]
            yd_ref[r0:r0 + SG_CHUNK, lo:lo + HEAD_W] = yd.astype(yd_ref.dtype)
        if g % 2 == 1:
            yield


def _hgrn_constants():
    c = HG_CHUNK
    t = np.arange(c)[:, None]
    s = np.arange(c)[None, :]
    mats = [(s <= t), (s > t)]
    for lv in HG_LEVELS:
        mid = (t // lv) * lv + lv // 2
        qside = t >= mid
        mats.append(np.where(qside, (s >= mid) & (s <= t), (s > t) & (s < mid)))
    rng = np.concatenate(mats, axis=0).astype(np.float32)
    rng = np.concatenate([rng, rng], axis=1)
    masks = np.stack([t == s] + [(t // lv) == (s // lv) for lv in HG_LEVELS]).astype(np.float32)
    masks = np.concatenate([masks, masks], axis=2)
    return jnp.asarray(rng, BF16), jnp.asarray(masks, F32)


def _hgrn_body(q_ref, f_ref, i_ref, g_ref, lb_ref, ng_ref, rng_ref, msk_ref,
               y_ref, st_ref, *, nsub):
    c = HG_CHUNK
    fz = f_ref[...].astype(F32)
    lb = lb_ref[...]
    e = jnp.exp(-jnp.abs(fz))
    r = 1.0 / (1.0 + e)
    pos = fz >= 0.0
    sig = jnp.where(pos, r, e * r)
    nsig = jnp.where(pos, e * r, r)
    f = lb + (1.0 - lb) * sig
    lg = jnp.log(jnp.maximum(f, F_FLOOR))
    kk_all = (1.0 - lb) * nsig

    g1 = lg.astype(BF16)
    g2 = (lg - g1.astype(F32)).astype(BF16)
    rng = rng_ref[...]
    exs = [jnp.dot(rng, jnp.concatenate([g1[u * c:(u + 1) * c], g2[u * c:(u + 1) * c]], axis=0),
                   preferred_element_type=F32)
           for u in range(nsub)]
    yield

    rows = lax.broadcasted_iota(jnp.int32, (c, HEAD_W), 0)
    zblk = jnp.zeros((c, HEAD_W), BF16)
    nt = (((1,), (1,)), ((), ()))

    def pair_rows(x0, x1):
        return jnp.concatenate([jnp.concatenate([x0, zblk], axis=1),
                                jnp.concatenate([zblk, x1], axis=1)], axis=0)

    for hp in range(N_HEADS // 2):
        los = (2 * hp * HEAD_W, (2 * hp + 1) * HEAD_W)
        sts = [st_ref[2 * hp], st_ref[2 * hp + 1]]
        for u in range(nsub):
            ex = exs[u]
            r0 = u * c
            qs, kks, vbs, outs = [], [], [], []
            for n, lo in enumerate(los):
                q = q_ref[r0:r0 + c, lo:lo + HEAD_W].astype(F32)
                vb = i_ref[r0:r0 + c, lo:lo + HEAD_W]
                kk = kk_all[r0:r0 + c, lo:lo + HEAD_W]
                bcum = ex[0:c, lo:lo + HEAD_W]
                q_in = (q * jnp.exp(bcum)).astype(BF16)
                outs.append(lax.dot_general(q_in, sts[n].astype(BF16), nt,
                                            preferred_element_type=F32))
                k_out = (kk * jnp.exp(ex[c:2 * c, lo:lo + HEAD_W])).astype(BF16)
                e_last = jnp.exp(bcum[c - 1:c, :])
                sts[n] = e_last * sts[n] + jnp.dot(vb.astype(F32).T.astype(BF16), k_out,
                                                   preferred_element_type=F32)
                qs.append(q)
                kks.append(kk)
                vbs.append(vb)
            yield

            a = msk_ref[0] * lax.dot_general(
                jnp.concatenate([qs[0].astype(BF16), qs[1].astype(BF16)], axis=1),
                pair_rows(kks[0].astype(BF16), kks[1].astype(BF16)), nt,
                preferred_element_type=F32)
            for li, lv in enumerate(HG_LEVELS):
                qside = (rows & (lv - 1)) >= (lv // 2)
                qls, kls = [], []
                for n, lo in enumerate(los):
                    el = jnp.exp(ex[(2 + li) * c:(3 + li) * c, lo:lo + HEAD_W])
                    qls.append(jnp.where(qside, qs[n] * el, 0.0).astype(BF16))
                    kls.append(jnp.where(qside, 0.0, kks[n] * el).astype(BF16))
                al = lax.dot_general(jnp.concatenate(qls, axis=1), pair_rows(kls[0], kls[1]), nt,
                                     preferred_element_type=F32)
                a = a + msk_ref[1 + li] * al
                if li % 2 == 0:
                    yield
            o2 = jnp.dot(a.astype(BF16), pair_rows(vbs[0], vbs[1]),
                         preferred_element_type=F32)

            for n, lo in enumerate(los):
                o = outs[n] + o2[:, n * HEAD_W:(n + 1) * HEAD_W]
                ms = jnp.mean(o * o, axis=-1, keepdims=True)
                on = o * lax.rsqrt(ms + RMS_EPS) * ng_ref[:, lo:lo + HEAD_W]
                y = on * _silu(g_ref[r0:r0 + c, lo:lo + HEAD_W].astype(F32))
                y_ref[r0:r0 + c, lo:lo + HEAD_W] = y.astype(y_ref.dtype)
            yield
        st_ref[2 * hp] = sts[0]
        st_ref[2 * hp + 1] = sts[1]


ATTN_ROWS = 64


def _diff_attn_kernel(lam_ref, q_ref, k_ref, v_ref, g_ref, ng_ref, wo_ref, wpg_ref, wpe_ref,
                      o_ref, wo_o, wpg_o, wpe_o,
                      qq_ref, pa_ref, pb_ref, pt_ref, pu_ref,
                      ma_ref, la_ref, aa_ref, mb_ref, lb_ref, ab_ref,
                      mt_ref, lt_ref, at_ref, mu_ref, lu_ref, au_ref,
                      acc_ref, *, tk, nq, out_scale):
    wo_o[...] = wo_ref[...].astype(BF16)
    wpg_o[...] = wpg_ref[...].astype(BF16)
    wpe_o[...] = wpe_ref[...].astype(BF16)

    lam = lam_ref[0, 0]
    scale = DA_DQK ** -0.5
    tq = 2 * tk
    nrow = 2 * tq
    ncb = tk // HEAD_W
    st_a = (ma_ref, la_ref, aa_ref)
    st_b = (mb_ref, lb_ref, ab_ref)
    st_t = (mt_ref, lt_ref, at_ref)
    st_u = (mu_ref, lu_ref, au_ref)

    def scores(j, ra=0, rb=nrow):
        kb = k_ref[pl.ds(pl.multiple_of(j * tk, tk), tk), :]
        return lax.dot_general(qq_ref[ra:rb, :], kb, (((1,), (1,)), ((), ())),
                               preferred_element_type=F32)

    def accumulate(j, p_ref, st, ra=0, rb=nrow):
        vb = v_ref[pl.ds(pl.multiple_of(j * tk, tk), tk), :]
        acc_ref[ra:rb, :] = (st[2][ra:rb, :] * acc_ref[ra:rb, :]
                             + jnp.dot(p_ref[ra:rb, :], vb, preferred_element_type=F32))

    def softmax(s_all, p_ref, st_in, st_out, diag=None, ra=0, rb=nrow):
        masked = diag is not None
        for r0 in range(ra, rb, ATTN_ROWS):
            rs = slice(r0, r0 + ATTN_ROWS)
            row_lo = (r0 // (2 * tk)) * tk + r0 % tk
            key_lo = diag * tk if masked else 0
            tiles = []
            for cb in range(ncb):
                k0 = key_lo + cb * HEAD_W
                if masked and k0 > row_lo + ATTN_ROWS - 1:
                    tiles.append(None)
                    continue
                s = s_all[r0 - ra:r0 - ra + ATTN_ROWS, cb * HEAD_W:(cb + 1) * HEAD_W]
                if masked and k0 + HEAD_W - 1 > row_lo:
                    r = row_lo + lax.broadcasted_iota(jnp.int32, s.shape, 0)
                    cidx = k0 + lax.broadcasted_iota(jnp.int32, s.shape, 1)
                    s = jnp.where(cidx <= r, s, MASK_VALUE)
                tiles.append(s)
            live = [s for s in tiles if s is not None]
            mx = functools.reduce(jnp.maximum, live)
            m_blk = jnp.broadcast_to(jnp.max(mx, axis=-1, keepdims=True), mx.shape)
            m_old = st_in[0][rs, :]
            m_new = jnp.maximum(m_old, m_blk)
            alpha = jnp.exp2(m_old - m_new)
            psum = None
            for cb, s in enumerate(tiles):
                if s is None:
                    p_ref[rs, cb * HEAD_W:(cb + 1) * HEAD_W] = jnp.zeros((ATTN_ROWS, HEAD_W), BF16)
                    continue
                p = jnp.exp2(s - m_new)
                psum = p if psum is None else psum + p
                p_ref[rs, cb * HEAD_W:(cb + 1) * HEAD_W] = p.astype(BF16)
            st_out[0][rs, :] = m_new
            st_out[1][rs, :] = alpha * st_in[1][rs, :] + psum
            st_out[2][rs, :] = alpha

    def step(j, p_in, p_out, st_in, st_out):
        s = scores(j)
        accumulate(jnp.maximum(j - 1, 0), p_in, st_in)
        softmax(s, p_out, st_in, st_out)

    def pair(jj, carry):
        step(2 * jj, pb_ref, pa_ref, st_b, st_a)
        step(2 * jj + 1, pa_ref, pb_ref, st_a, st_b)
        return carry

    def q_tile(i, carry):
        for half in range(2):
            rows = pl.ds(pl.multiple_of(i * tq + half * tk, tk), tk)
            q = (q_ref[rows, :].astype(F32) * (scale * math.log2(math.e))).astype(BF16)
            lane = lax.broadcasted_iota(jnp.int32, q.shape, 1)
            zero = jnp.zeros_like(q)
            qq_ref[2 * half * tk:(2 * half + 1) * tk, :] = jnp.where(lane < DA_DQK, q, zero)
            qq_ref[(2 * half + 1) * tk:(2 * half + 2) * tk, :] = jnp.where(lane >= DA_DQK, q, zero)
        acc_ref[...] = jnp.zeros_like(acc_ref)
        mb_ref[...] = jnp.full_like(mb_ref, MASK_VALUE)
        lb_ref[...] = jnp.zeros_like(lb_ref)
        ab_ref[...] = jnp.ones_like(ab_ref)
        pb_ref[...] = jnp.zeros_like(pb_ref)

        lax.fori_loop(0, i, pair, 0)

        j0 = 2 * i
        s = scores(j0)
        accumulate(jnp.maximum(j0 - 1, 0), pb_ref, st_b)
        softmax(s, pt_ref, st_b, st_t, diag=0)
        s = scores(j0 + 1, tq, nrow)
        accumulate(j0, pt_ref, st_t)
        softmax(s, pu_ref, st_t, st_u, diag=1, ra=tq, rb=nrow)
        accumulate(j0 + 1, pu_ref, st_u, tq, nrow)

        for half, st in enumerate((st_t, st_u)):
            r0 = 2 * half * tk
            l = jnp.sum(st[1][r0:r0 + 2 * tk, :], axis=-1, keepdims=True)
            acc = acc_ref[r0:r0 + 2 * tk, :] / l
            o = acc[0:tk] - lam * acc[tk:2 * tk]
            ms = jnp.mean(o * o, axis=-1, keepdims=True)
            on = o * lax.rsqrt(ms + RMS_EPS) * ng_ref[...] * out_scale
            rows = pl.ds(pl.multiple_of(i * tq + half * tk, tk), tk)
            o_ref[rows, :] = (on * _silu(g_ref[rows, :].astype(F32))).astype(o_ref.dtype)
        return carry

    lax.fori_loop(0, nq, q_tile, 0)


def _diff_attn(h, lam, lam_init, norm_g, wo_all, wpg_all, wpe_all, layer, batch, seq, tk=512):
    t = h.shape[0]
    tq = 2 * tk
    nq = seq // tq
    per = GROUP_W // HEAD_W
    ng2 = norm_g.reshape(1, GROUP_W).astype(F32)
    lam2 = jnp.reshape(lam, (1, 1)).astype(F32)
    kern = functools.partial(_diff_attn_kernel, tk=tk, nq=nq, out_scale=1.0 - lam_init)
    cast_cols = wo_all.shape[2] // (batch * N_HEADS)

    def head_cols(blk):
        return pl.BlockSpec((seq, HEAD_W), lambda b, hh, blk=blk: (b, blk * per + hh))

    def cast_in(a):
        return pl.BlockSpec((None, a.shape[1], cast_cols),
                            lambda b, hh: (layer, 0, b * N_HEADS + hh))

    def cast_out(a):
        return pl.BlockSpec((a.shape[1], cast_cols), lambda b, hh: (0, b * N_HEADS + hh))

    weights = (wo_all, wpg_all, wpe_all)
    return pl.pallas_call(
        kern,
        grid=(batch, N_HEADS),
        in_specs=[pl.BlockSpec(memory_space=pltpu.SMEM),
                  head_cols(C_Q), head_cols(C_K), head_cols(C_V), head_cols(G_C),
                  pl.BlockSpec((1, HEAD_W), lambda b, hh: (0, hh))]
                 + [cast_in(a) for a in weights],
        out_specs=[pl.BlockSpec((seq, HEAD_W), lambda b, hh: (b, hh))]
                  + [cast_out(a) for a in weights],
        out_shape=[jax.ShapeDtypeStruct((t, GROUP_W), BF16)]
                  + [jax.ShapeDtypeStruct(a.shape[1:], BF16) for a in weights],
        scratch_shapes=[pltpu.VMEM((2 * tq, HEAD_W), BF16),
                        *([pltpu.VMEM((2 * tq, tk), BF16)] * 4),
                        *([pltpu.VMEM((2 * tq, HEAD_W), F32)] * 13)],
        compiler_params=pltpu.CompilerParams(
            dimension_semantics=("arbitrary", "arbitrary"),
            vmem_limit_bytes=VMEM_LIMIT),
        name="diff_attn",
    )(lam2, h, h, h, h, ng2, *weights)


TAIL_ROWS = 256


def _tail_kernel(ab_ref, ac_ref, ax_ref, ga_ref, du_ref, dv_ref, gd_ref,
                 bq_ref, bf_ref, bi_ref, gb_ref, yc_ref, x_ref, p_ref,
                 cw_ref, sglng_ref, sglnb_ref, ws_ref, bst_ref,
                 hlb_ref, hng_ref, rng_ref, msk_ref,
                 wo_ref, wpg_ref, wpe_ref, lng_ref, lnb_ref,
                 xo_ref, xb_ref,
                 ya_ref, yb_ref, yd_ref, carry_ref, st_ref, *, seq, alpha):
    n = pl.program_id(0)

    @pl.when(n == 0)
    def _():
        ya_ref[...] = jnp.zeros_like(ya_ref)
        yb_ref[...] = jnp.zeros_like(yb_ref)
        yd_ref[...] = jnp.zeros_like(yd_ref)

    @pl.when((n * TAIL_ROWS) % seq == 0)
    def _():
        carry_ref[...] = jnp.zeros_like(carry_ref)
        st_ref[...] = jnp.zeros_like(st_ref)

    wslot = n % 2
    rslot = 1 - wslot

    def mixers():
        yield from _conv_sgu_body(ab_ref, ac_ref, ax_ref, ga_ref, du_ref, dv_ref, gd_ref,
                                  cw_ref, sglng_ref, sglnb_ref, ws_ref, bst_ref,
                                  ya_ref.at[wslot], yd_ref.at[wslot], carry_ref, ts=TAIL_ROWS)
        yield from _hgrn_body(bq_ref, bf_ref, bi_ref, gb_ref, hlb_ref, hng_ref, rng_ref, msk_ref,
                              yb_ref.at[wslot], st_ref, nsub=TAIL_ROWS // HG_CHUNK)

    def projections():
        d = x_ref.shape[-1]
        blocks = [slice(n0, n0 + GROUP_W) for n0 in range(0, d, GROUP_W)]
        ys = ((yc_ref, 2), (ya_ref.at[rslot], 0), (yd_ref.at[rslot], 3), (yb_ref.at[rslot], 1))
        accs = []
        for cols in blocks:
            acc = alpha * x_ref[:, cols]
            for y, g in ys:
                acc = acc + jnp.dot(y[...], wo_ref[g * GROUP_W:(g + 1) * GROUP_W, cols],
                                    preferred_element_type=F32)
                yield
            accs.append(acc)
        pe = jnp.dot(p_ref[...].astype(BF16), wpe_ref[...], preferred_element_type=F32)
        mu = sum(jnp.sum(a, axis=-1, keepdims=True) for a in accs) * (1.0 / d)
        xcs = [a - mu for a in accs]
        var = sum(jnp.sum(xc * xc, axis=-1, keepdims=True) for xc in xcs) * (1.0 / d)
        rstd = lax.rsqrt(var + LN_EPS)
        xns = [xc * rstd * lng_ref[:, cols] + lnb_ref[:, cols] for xc, cols in zip(xcs, blocks)]
        xnbs = [xn.astype(BF16) for xn in xns]
        yield
        for xn, cols in zip(xns, blocks):
            z = None
            for xnb, rows in zip(xnbs, blocks):
                part = jnp.dot(xnb, wpg_ref[rows, cols], preferred_element_type=F32)
                z = part if z is None else z + part
                yield
            out = xn + pe[:, cols] * _sigmoid(z)
            xo_ref[:, cols] = out
            xb_ref[:, cols] = out.astype(BF16)

    major, minor = mixers(), projections()
    live_major = live_minor = True
    while live_major or live_minor:
        live_major = live_major and next(major, False) is not False
        live_minor = live_minor and next(minor, False) is not False


def _tail_block(h, yc, x, p_all, small, wo_all, wpg_all, wpe_all, lng_all, lnb_all,
                layer, seq, alpha):
    t, d = x.shape
    tm = TAIL_ROWS
    pdim = p_all.shape[1]
    steps = t // tm

    def hblk(blk):
        return pl.BlockSpec((tm, GROUP_W), lambda i, blk=blk: (jnp.minimum(i, steps - 1), blk))

    def rows(w):
        return pl.BlockSpec((tm, w), lambda i: (jnp.maximum(i - 1, 0), 0))

    def full(a):
        nd = a.ndim
        return pl.BlockSpec(a.shape, lambda i, nd=nd: (0,) * nd)

    def resident(a):
        if a.ndim == 2:
            return pl.BlockSpec(a.shape, lambda i: (0, 0), pipeline_mode=pl.Buffered(1))
        return pl.BlockSpec((None,) + a.shape[1:], lambda i: (layer, 0, 0),
                            pipeline_mode=pl.Buffered(1))

    h_blocks = (A_B, A_C, A_X, G_A, D_U, D_V, G_D, B_Q, B_F, B_I, G_B)
    kern = functools.partial(_tail_kernel, seq=seq, alpha=alpha)
    return pl.pallas_call(
        kern,
        grid=(steps + 1,),
        in_specs=[hblk(b) for b in h_blocks]
                 + [rows(GROUP_W), rows(d),
                    pl.BlockSpec((tm, pdim),
                                 lambda i: (layer * steps + jnp.maximum(i - 1, 0), 0))]
                 + [full(a) for a in small]
                 + [resident(wo_all), resident(wpg_all), resident(wpe_all),
                    resident(lng_all), resident(lnb_all)],
        out_specs=[rows(d), rows(d)],
        out_shape=[jax.ShapeDtypeStruct((t, d), F32), jax.ShapeDtypeStruct((t, d), BF16)],
        scratch_shapes=[pltpu.VMEM((2, tm, GROUP_W), BF16), pltpu.VMEM((2, tm, GROUP_W), BF16),
                        pltpu.VMEM((2, tm, GROUP_W), BF16),
                        pltpu.VMEM((8, GROUP_W), F32),
                        pltpu.VMEM((N_HEADS, HEAD_W, HEAD_W), F32)],
        compiler_params=pltpu.CompilerParams(
            dimension_semantics=("arbitrary",), vmem_limit_bytes=VMEM_LIMIT),
        name="mix_out",
    )(*([h] * len(h_blocks)), yc, x, p_all, *small,
      wo_all, wpg_all, wpe_all, lng_all, lnb_all)


def kernel(x, p, w_in, conv_w, hgrn_lb, hgrn_norm_g, diff_lambda, diff_norm_g,
           sg_ln_g, sg_ln_b, sg_w, sg_b, w_out, ln_g, ln_b, w_pe, w_pg):
    batch, seq, d_model = x.shape
    depth = w_in.shape[0]
    t = batch * seq
    alpha = (2 * depth) ** 0.25

    lb_sm = jax.nn.softmax(hgrn_lb.astype(F32), axis=0)
    lower_bounds = jnp.cumsum(lb_sm, axis=0) - lb_sm[0]

    xf = x.reshape(t, d_model)
    xb = xf
    p_all = p.reshape(depth * t, p.shape[-1])
    lng_all = ln_g.reshape(depth, 1, d_model).astype(F32)
    lnb_all = ln_b.reshape(depth, 1, d_model).astype(F32)
    rng, masks = _hgrn_constants()
    for i in range(depth):
        lam_init = 0.8 - 0.6 * math.exp(-0.3 * i)
        dl = diff_lambda[i].astype(F32)
        lam = (jnp.exp(jnp.sum(dl[0] * dl[1])) - jnp.exp(jnp.sum(dl[2] * dl[3])) + lam_init)

        h = _in_proj(xb, w_in, i)
        yc, wo_b, wpg_b, wpe_b = _diff_attn(h, lam, lam_init, diff_norm_g[i],
                                            w_out, w_pg, w_pe, i, batch, seq)
        small = (conv_w[i].astype(F32),
                 sg_ln_g[i].reshape(1, GROUP_W).astype(F32), sg_ln_b[i].reshape(1, GROUP_W).astype(F32),
                 sg_w[i].astype(F32), sg_b[i].T.astype(F32),
                 lower_bounds[i].reshape(1, GROUP_W), hgrn_norm_g[i].reshape(1, GROUP_W).astype(F32),
                 rng, masks)
        xf, xb = _tail_block(h, yc, xf, p_all, small, wo_b, wpg_b, wpe_b,
                             lng_all, lnb_all, i, seq, alpha)
    return xf.reshape(batch, seq, d_model)
```

```python
import functools
import math

import numpy as np
import jax
import jax.numpy as jnp
from jax import lax
from jax.experimental import pallas as pl
from jax.experimental.pallas import tpu as pltpu

F32 = jnp.float32
BF16 = jnp.bfloat16

GROUP_W = 512
HEAD_W = 128
N_HEADS = GROUP_W // HEAD_W
DA_DQK = 64
F_FLOOR = 1e-30
MASK_VALUE = -1e30
LN_EPS = 1e-5
RMS_EPS = 1e-6

A_B, A_C, A_X, B_Q, B_F, B_I, C_Q, C_K, C_V, D_U, D_V, G_A, G_B, G_C, G_D = range(15)

HG_CHUNK = 128
HG_LEVELS = (2, 4, 8, 16, 32, 64, 128)
SG_CHUNK = 128
VMEM_LIMIT = 56 * 1024 * 1024


def _sigmoid(x):
    return 1.0 / (1.0 + jnp.exp(-x))


def _silu(x):
    return x * _sigmoid(x)


def _gelu_tanh(x):
    c = math.sqrt(2.0 / math.pi)
    return 0.5 * x * (1.0 + jnp.tanh(c * (x + 0.044715 * (x * x * x))))


def _matmul_kernel(x_ref, w_ref, o_ref):
    o_ref[...] = jnp.dot(x_ref[...].astype(BF16), w_ref[...].astype(BF16),
                         preferred_element_type=F32).astype(o_ref.dtype)


def _in_proj(xb, w_all, layer, tm=2048, tn=512):
    t, k = xb.shape
    n = w_all.shape[2]
    return pl.pallas_call(
        _matmul_kernel,
        grid=(t // tm, n // tn),
        in_specs=[pl.BlockSpec((tm, k), lambda i, j: (i, 0)),
                  pl.BlockSpec((None, k, tn), lambda i, j: (layer, 0, j))],
        out_specs=pl.BlockSpec((tm, tn), lambda i, j: (i, j)),
        out_shape=jax.ShapeDtypeStruct((t, n), BF16),
        compiler_params=pltpu.CompilerParams(
            dimension_semantics=("arbitrary", "arbitrary"),
            vmem_limit_bytes=VMEM_LIMIT),
        name="in_proj",
    )(xb, w_all)


def _conv_sgu_body(ab_ref, ac_ref, ax_ref, ga_ref, du_ref, dv_ref, gd_ref,
                   cw_ref, lng_ref, lnb_ref, ws_ref, bst_ref,
                   ya_ref, yd_ref, carry_ref, *, ts):
    z = ac_ref[...].astype(F32) * ax_ref[...].astype(F32)
    rows = lax.broadcasted_iota(jnp.int32, z.shape, 0)
    prev1 = carry_ref[7:8, :]
    prev2 = carry_ref[6:7, :]
    z1 = jnp.where(rows == 0, prev1, pltpu.roll(z, 1, 0))
    z2 = jnp.where(rows == 0, prev2, jnp.where(rows == 1, prev1, pltpu.roll(z, 2, 0)))
    cw = cw_ref[...]
    y = cw[0:1, :] * z2 + cw[1:2, :] * z1 + cw[2:3, :] * z
    ya = ab_ref[...].astype(F32) * y * _silu(ga_ref[...].astype(F32))
    ya_ref[...] = ya.astype(ya_ref.dtype)
    carry_ref[...] = z[ts - 8:ts, :]
    yield

    u = _gelu_tanh(du_ref[...].astype(F32))
    v = _gelu_tanh(dv_ref[...].astype(F32))
    mu = jnp.mean(v, axis=-1, keepdims=True)
    vc = v - mu
    var = jnp.mean(vc * vc, axis=-1, keepdims=True)
    vn = (vc * lax.rsqrt(var + LN_EPS) * lng_ref[...] + lnb_ref[...]).astype(BF16)
    gate = _silu(gd_ref[...].astype(F32))
    yield
    tri_r = lax.broadcasted_iota(jnp.int32, (SG_CHUNK, SG_CHUNK), 0)
    tri_c = lax.broadcasted_iota(jnp.int32, (SG_CHUNK, SG_CHUNK), 1)
    bst = bst_ref[...]
    for g in range(N_HEADS):
        w = jnp.where(tri_c <= tri_r, ws_ref[g], 0.0).astype(BF16)
        bias = bst[:, g:g + 1]
        lo = g * HEAD_W
        for c in range(ts // SG_CHUNK):
            r0 = c * SG_CHUNK
            sv = jnp.dot(w, vn[r0:r0 + SG_CHUNK, lo:lo + HEAD_W],
                         preferred_element_type=F32) + bias
            yd = u[r0:r0 + SG_CHUNK, lo:lo + HEAD_W] * sv * gate[r0:r0 + SG_CHUNK, lo:lo + HEAD_W]
            yd_ref[r0:r0 + SG_CHUNK, lo:lo + HEAD_W] = yd.astype(yd_ref.dtype)
        if g % 2 == 1:
            yield


def _hgrn_constants():
    c = HG_CHUNK
    t = np.arange(c)[:, None]
    s = np.arange(c)[None, :]
    mats = [(s <= t), (s > t)]
    for lv in HG_LEVELS:
        mid = (t // lv) * lv + lv // 2
        qside = t >= mid
        mats.append(np.where(qside, (s >= mid) & (s <= t), (s > t) & (s < mid)))
    rng = np.concatenate(mats, axis=0).astype(np.float32)
    rng = np.concatenate([rng, rng], axis=1)
    masks = np.stack([t == s] + [(t // lv) == (s // lv) for lv in HG_LEVELS]).astype(np.float32)
    masks = np.concatenate([masks, masks], axis=2)
    return jnp.asarray(rng, BF16), jnp.asarray(masks, F32)


def _hgrn_body(q_ref, f_ref, i_ref, g_ref, lb_ref, ng_ref, rng_ref, msk_ref,
               y_ref, st_ref, *, nsub):
    c = HG_CHUNK
    fz = f_ref[...].astype(F32)
    lb = lb_ref[...]
    e = jnp.exp(-jnp.abs(fz))
    r = 1.0 / (1.0 + e)
    pos = fz >= 0.0
    sig = jnp.where(pos, r, e * r)
    nsig = jnp.where(pos, e * r, r)
    f = lb + (1.0 - lb) * sig
    lg = jnp.log(jnp.maximum(f, F_FLOOR))
    kk_all = (1.0 - lb) * nsig

    g1 = lg.astype(BF16)
    g2 = (lg - g1.astype(F32)).astype(BF16)
    rng = rng_ref[...]
    exs = [jnp.dot(rng, jnp.concatenate([g1[u * c:(u + 1) * c], g2[u * c:(u + 1) * c]], axis=0),
                   preferred_element_type=F32)
           for u in range(nsub)]
    yield

    rows = lax.broadcasted_iota(jnp.int32, (c, HEAD_W), 0)
    zblk = jnp.zeros((c, HEAD_W), BF16)
    nt = (((1,), (1,)), ((), ()))

    def pair_rows(x0, x1):
        return jnp.concatenate([jnp.concatenate([x0, zblk], axis=1),
                                jnp.concatenate([zblk, x1], axis=1)], axis=0)

    for hp in range(N_HEADS // 2):
        los = (2 * hp * HEAD_W, (2 * hp + 1) * HEAD_W)
        sts = [st_ref[2 * hp], st_ref[2 * hp + 1]]
        for u in range(nsub):
            ex = exs[u]
            r0 = u * c
            qs, kks, vbs, outs = [], [], [], []
            for n, lo in enumerate(los):
                q = q_ref[r0:r0 + c, lo:lo + HEAD_W].astype(F32)
                vb = i_ref[r0:r0 + c, lo:lo + HEAD_W]
                kk = kk_all[r0:r0 + c, lo:lo + HEAD_W]
                bcum = ex[0:c, lo:lo + HEAD_W]
                q_in = (q * jnp.exp(bcum)).astype(BF16)
                outs.append(lax.dot_general(q_in, sts[n].astype(BF16), nt,
                                            preferred_element_type=F32))
                k_out = (kk * jnp.exp(ex[c:2 * c, lo:lo + HEAD_W])).astype(BF16)
                e_last = jnp.exp(bcum[c - 1:c, :])
                sts[n] = e_last * sts[n] + jnp.dot(vb.astype(F32).T.astype(BF16), k_out,
                                                   preferred_element_type=F32)
                qs.append(q)
                kks.append(kk)
                vbs.append(vb)
            yield

            a = msk_ref[0] * lax.dot_general(
                jnp.concatenate([qs[0].astype(BF16), qs[1].astype(BF16)], axis=1),
                pair_rows(kks[0].astype(BF16), kks[1].astype(BF16)), nt,
                preferred_element_type=F32)
            for li, lv in enumerate(HG_LEVELS):
                qside = (rows & (lv - 1)) >= (lv // 2)
                qls, kls = [], []
                for n, lo in enumerate(los):
                    el = jnp.exp(ex[(2 + li) * c:(3 + li) * c, lo:lo + HEAD_W])
                    qls.append(jnp.where(qside, qs[n] * el, 0.0).astype(BF16))
                    kls.append(jnp.where(qside, 0.0, kks[n] * el).astype(BF16))
                al = lax.dot_general(jnp.concatenate(qls, axis=1), pair_rows(kls[0], kls[1]), nt,
                                     preferred_element_type=F32)
                a = a + msk_ref[1 + li] * al
                if li % 2 == 0:
                    yield
            o2 = jnp.dot(a.astype(BF16), pair_rows(vbs[0], vbs[1]),
                         preferred_element_type=F32)

            for n, lo in enumerate(los):
                o = outs[n] + o2[:, n * HEAD_W:(n + 1) * HEAD_W]
                ms = jnp.mean(o * o, axis=-1, keepdims=True)
                on = o * lax.rsqrt(ms + RMS_EPS) * ng_ref[:, lo:lo + HEAD_W]
                y = on * _silu(g_ref[r0:r0 + c, lo:lo + HEAD_W].astype(F32))
                y_ref[r0:r0 + c, lo:lo + HEAD_W] = y.astype(y_ref.dtype)
            yield
        st_ref[2 * hp] = sts[0]
        st_ref[2 * hp + 1] = sts[1]


ATTN_ROWS = 64


def _diff_attn_kernel(lam_ref, q_ref, k_ref, v_ref, g_ref, ng_ref, wo_ref, wpg_ref, wpe_ref,
                      o_ref, wo_o, wpg_o, wpe_o,
                      qq_ref, pa_ref, pb_ref, pt_ref, pu_ref,
                      ma_ref, la_ref, aa_ref, mb_ref, lb_ref, ab_ref,
                      mt_ref, lt_ref, at_ref, mu_ref, lu_ref, au_ref,
                      acc_ref, *, tk, nq, out_scale):
    wo_o[...] = wo_ref[...].astype(BF16)
    wpg_o[...] = wpg_ref[...].astype(BF16)
    wpe_o[...] = wpe_ref[...].astype(BF16)

    lam = lam_ref[0, 0]
    scale = DA_DQK ** -0.5
    tq = 2 * tk
    nrow = 2 * tq
    ncb = tk // HEAD_W
    st_a = (ma_ref, la_ref, aa_ref)
    st_b = (mb_ref, lb_ref, ab_ref)
    st_t = (mt_ref, lt_ref, at_ref)
    st_u = (mu_ref, lu_ref, au_ref)

    def scores(j, ra=0, rb=nrow):
        kb = k_ref[pl.ds(pl.multiple_of(j * tk, tk), tk), :]
        return lax.dot_general(qq_ref[ra:rb, :], kb, (((1,), (1,)), ((), ())),
                               preferred_element_type=F32)

    def accumulate(j, p_ref, st, ra=0, rb=nrow):
        vb = v_ref[pl.ds(pl.multiple_of(j * tk, tk), tk), :]
        acc_ref[ra:rb, :] = (st[2][ra:rb, :] * acc_ref[ra:rb, :]
                             + jnp.dot(p_ref[ra:rb, :], vb, preferred_element_type=F32))

    def softmax(s_all, p_ref, st_in, st_out, diag=None, ra=0, rb=nrow):
        masked = diag is not None
        for r0 in range(ra, rb, ATTN_ROWS):
            rs = slice(r0, r0 + ATTN_ROWS)
            row_lo = (r0 // (2 * tk)) * tk + r0 % tk
            key_lo = diag * tk if masked else 0
            tiles = []
            for cb in range(ncb):
                k0 = key_lo + cb * HEAD_W
                if masked and k0 > row_lo + ATTN_ROWS - 1:
                    tiles.append(None)
                    continue
                s = s_all[r0 - ra:r0 - ra + ATTN_ROWS, cb * HEAD_W:(cb + 1) * HEAD_W]
                if masked and k0 + HEAD_W - 1 > row_lo:
                    r = row_lo + lax.broadcasted_iota(jnp.int32, s.shape, 0)
                    cidx = k0 + lax.broadcasted_iota(jnp.int32, s.shape, 1)
                    s = jnp.where(cidx <= r, s, MASK_VALUE)
                tiles.append(s)
            live = [s for s in tiles if s is not None]
            mx = functools.reduce(jnp.maximum, live)
            m_blk = jnp.broadcast_to(jnp.max(mx, axis=-1, keepdims=True), mx.shape)
            m_old = st_in[0][rs, :]
            m_new = jnp.maximum(m_old, m_blk)
            alpha = jnp.exp2(m_old - m_new)
            psum = None
            for cb, s in enumerate(tiles):
                if s is None:
                    p_ref[rs, cb * HEAD_W:(cb + 1) * HEAD_W] = jnp.zeros((ATTN_ROWS, HEAD_W), BF16)
                    continue
                p = jnp.exp2(s - m_new)
                psum = p if psum is None else psum + p
                p_ref[rs, cb * HEAD_W:(cb + 1) * HEAD_W] = p.astype(BF16)
            st_out[0][rs, :] = m_new
            st_out[1][rs, :] = alpha * st_in[1][rs, :] + psum
            st_out[2][rs, :] = alpha

    def step(j, p_in, p_out, st_in, st_out):
        s = scores(j)
        accumulate(jnp.maximum(j - 1, 0), p_in, st_in)
        softmax(s, p_out, st_in, st_out)

    def pair(jj, carry):
        step(2 * jj, pb_ref, pa_ref, st_b, st_a)
        step(2 * jj + 1, pa_ref, pb_ref, st_a, st_b)
        return carry

    def q_tile(i, carry):
        for half in range(2):
            rows = pl.ds(pl.multiple_of(i * tq + half * tk, tk), tk)
            q = (q_ref[rows, :].astype(F32) * (scale * math.log2(math.e))).astype(BF16)
            lane = lax.broadcasted_iota(jnp.int32, q.shape, 1)
            zero = jnp.zeros_like(q)
            qq_ref[2 * half * tk:(2 * half + 1) * tk, :] = jnp.where(lane < DA_DQK, q, zero)
            qq_ref[(2 * half + 1) * tk:(2 * half + 2) * tk, :] = jnp.where(lane >= DA_DQK, q, zero)
        acc_ref[...] = jnp.zeros_like(acc_ref)
        mb_ref[...] = jnp.full_like(mb_ref, MASK_VALUE)
        lb_ref[...] = jnp.zeros_like(lb_ref)
        ab_ref[...] = jnp.ones_like(ab_ref)
        pb_ref[...] = jnp.zeros_like(pb_ref)

        lax.fori_loop(0, i, pair, 0)

        j0 = 2 * i
        s = scores(j0)
        accumulate(jnp.maximum(j0 - 1, 0), pb_ref, st_b)
        softmax(s, pt_ref, st_b, st_t, diag=0)
        s = scores(j0 + 1, tq, nrow)
        accumulate(j0, pt_ref, st_t)
        softmax(s, pu_ref, st_t, st_u, diag=1, ra=tq, rb=nrow)
        accumulate(j0 + 1, pu_ref, st_u, tq, nrow)

        for half, st in enumerate((st_t, st_u)):
            r0 = 2 * half * tk
            l = jnp.sum(st[1][r0:r0 + 2 * tk, :], axis=-1, keepdims=True)
            acc = acc_ref[r0:r0 + 2 * tk, :] / l
            o = acc[0:tk] - lam * acc[tk:2 * tk]
            ms = jnp.mean(o * o, axis=-1, keepdims=True)
            on = o * lax.rsqrt(ms + RMS_EPS) * ng_ref[...] * out_scale
            rows = pl.ds(pl.multiple_of(i * tq + half * tk, tk), tk)
            o_ref[rows, :] = (on * _silu(g_ref[rows, :].astype(F32))).astype(o_ref.dtype)
        return carry

    lax.fori_loop(0, nq, q_tile, 0)


def _diff_attn(h, lam, lam_init, norm_g, wo_all, wpg_all, wpe_all, layer, batch, seq, tk=512):
    t = h.shape[0]
    tq = 2 * tk
    nq = seq // tq
    per = GROUP_W // HEAD_W
    ng2 = norm_g.reshape(1, GROUP_W).astype(F32)
    lam2 = jnp.reshape(lam, (1, 1)).astype(F32)
    kern = functools.partial(_diff_attn_kernel, tk=tk, nq=nq, out_scale=1.0 - lam_init)
    cast_cols = wo_all.shape[2] // (batch * N_HEADS)

    def head_cols(blk):
        return pl.BlockSpec((seq, HEAD_W), lambda b, hh, blk=blk: (b, blk * per + hh))

    def cast_in(a):
        return pl.BlockSpec((None, a.shape[1], cast_cols),
                            lambda b, hh: (layer, 0, b * N_HEADS + hh))

    def cast_out(a):
        return pl.BlockSpec((a.shape[1], cast_cols), lambda b, hh: (0, b * N_HEADS + hh))

    weights = (wo_all, wpg_all, wpe_all)
    return pl.pallas_call(
        kern,
        grid=(batch, N_HEADS),
        in_specs=[pl.BlockSpec(memory_space=pltpu.SMEM),
                  head_cols(C_Q), head_cols(C_K), head_cols(C_V), head_cols(G_C),
                  pl.BlockSpec((1, HEAD_W), lambda b, hh: (0, hh))]
                 + [cast_in(a) for a in weights],
        out_specs=[pl.BlockSpec((seq, HEAD_W), lambda b, hh: (b, hh))]
                  + [cast_out(a) for a in weights],
        out_shape=[jax.ShapeDtypeStruct((t, GROUP_W), BF16)]
                  + [jax.ShapeDtypeStruct(a.shape[1:], BF16) for a in weights],
        scratch_shapes=[pltpu.VMEM((2 * tq, HEAD_W), BF16),
                        *([pltpu.VMEM((2 * tq, tk), BF16)] * 4),
                        *([pltpu.VMEM((2 * tq, HEAD_W), F32)] * 13)],
        compiler_params=pltpu.CompilerParams(
            dimension_semantics=("arbitrary", "arbitrary"),
            vmem_limit_bytes=VMEM_LIMIT),
        name="diff_attn",
    )(lam2, h, h, h, h, ng2, *weights)


TAIL_ROWS = 256


def _tail_kernel(ab_ref, ac_ref, ax_ref, ga_ref, du_ref, dv_ref, gd_ref,
                 bq_ref, bf_ref, bi_ref, gb_ref, yc_ref, x_ref, p_ref,
                 cw_ref, sglng_ref, sglnb_ref, ws_ref, bst_ref,
                 hlb_ref, hng_ref, rng_ref, msk_ref,
                 wo_ref, wpg_ref, wpe_ref, lng_ref, lnb_ref,
                 xo_ref, xb_ref,
                 ya_ref, yb_ref, yd_ref, carry_ref, st_ref, *, seq, alpha):
    n = pl.program_id(0)

    @pl.when((n * TAIL_ROWS) % seq == 0)
    def _():
        carry_ref[...] = jnp.zeros_like(carry_ref)
        st_ref[...] = jnp.zeros_like(st_ref)

    wslot = n % 2
    rslot = 1 - wslot

    def mixers():
        yield from _conv_sgu_body(ab_ref, ac_ref, ax_ref, ga_ref, du_ref, dv_ref, gd_ref,
                                  cw_ref, sglng_ref, sglnb_ref, ws_ref, bst_ref,
                                  ya_ref.at[wslot], yd_ref.at[wslot], carry_ref, ts=TAIL_ROWS)
        yield from _hgrn_body(bq_ref, bf_ref, bi_ref, gb_ref, hlb_ref, hng_ref, rng_ref, msk_ref,
                              yb_ref.at[wslot], st_ref, nsub=TAIL_ROWS // HG_CHUNK)

    def projections():
        d = x_ref.shape[-1]
        blocks = [slice(n0, n0 + GROUP_W) for n0 in range(0, d, GROUP_W)]
        ys = ((yc_ref, 2), (ya_ref.at[rslot], 0), (yd_ref.at[rslot], 3), (yb_ref.at[rslot], 1))
        accs = []
        for cols in blocks:
            acc = alpha * x_ref[:, cols]
            for y, g in ys:
                acc = acc + jnp.dot(y[...], wo_ref[g * GROUP_W:(g + 1) * GROUP_W, cols],
                                    preferred_element_type=F32)
                yield
            accs.append(acc)
        pe = jnp.dot(p_ref[...].astype(BF16), wpe_ref[...], preferred_element_type=F32)
        mu = sum(jnp.sum(a, axis=-1, keepdims=True) for a in accs) * (1.0 / d)
        xcs = [a - mu for a in accs]
        var = sum(jnp.sum(xc * xc, axis=-1, keepdims=True) for xc in xcs) * (1.0 / d)
        rstd = lax.rsqrt(var + LN_EPS)
        xns = [xc * rstd * lng_ref[:, cols] + lnb_ref[:, cols] for xc, cols in zip(xcs, blocks)]
        xnbs = [xn.astype(BF16) for xn in xns]
        yield
        for xn, cols in zip(xns, blocks):
            z = None
            for xnb, rows in zip(xnbs, blocks):
                part = jnp.dot(xnb, wpg_ref[rows, cols], preferred_element_type=F32)
                z = part if z is None else z + part
                yield
            out = xn + pe[:, cols] * _sigmoid(z)
            xo_ref[:, cols] = out
            xb_ref[:, cols] = out.astype(BF16)

    last = pl.num_programs(0) - 1

    @pl.when(n == 0)
    def _():
        for _ in mixers():
            pass

    @pl.when(n == last)
    def _():
        for _ in projections():
            pass

    @pl.when(jnp.logical_and(n > 0, n < last))
    def _():
        major, minor = mixers(), projections()
        live_major = live_minor = True
        while live_major or live_minor:
            live_major = live_major and next(major, False) is not False
            live_minor = live_minor and next(minor, False) is not False


def _tail_block(h, yc, x, p_all, small, wo_all, wpg_all, wpe_all, lng_all, lnb_all,
                layer, seq, alpha):
    t, d = x.shape
    tm = TAIL_ROWS
    pdim = p_all.shape[1]
    steps = t // tm

    def hblk(blk):
        return pl.BlockSpec((tm, GROUP_W), lambda i, blk=blk: (jnp.minimum(i, steps - 1), blk))

    def rows(w):
        return pl.BlockSpec((tm, w), lambda i: (jnp.maximum(i - 1, 0), 0))

    def full(a):
        nd = a.ndim
        return pl.BlockSpec(a.shape, lambda i, nd=nd: (0,) * nd)

    def resident(a):
        if a.ndim == 2:
            return pl.BlockSpec(a.shape, lambda i: (0, 0), pipeline_mode=pl.Buffered(1))
        return pl.BlockSpec((None,) + a.shape[1:], lambda i: (layer, 0, 0),
                            pipeline_mode=pl.Buffered(1))

    h_blocks = (A_B, A_C, A_X, G_A, D_U, D_V, G_D, B_Q, B_F, B_I, G_B)
    kern = functools.partial(_tail_kernel, seq=seq, alpha=alpha)
    return pl.pallas_call(
        kern,
        grid=(steps + 1,),
        in_specs=[hblk(b) for b in h_blocks]
                 + [rows(GROUP_W), rows(d),
                    pl.BlockSpec((tm, pdim),
                                 lambda i: (layer * steps + jnp.maximum(i - 1, 0), 0))]
                 + [full(a) for a in small]
                 + [resident(wo_all), resident(wpg_all), resident(wpe_all),
                    resident(lng_all), resident(lnb_all)],
        out_specs=[rows(d), rows(d)],
        out_shape=[jax.ShapeDtypeStruct((t, d), F32), jax.ShapeDtypeStruct((t, d), BF16)],
        scratch_shapes=[pltpu.VMEM((2, tm, GROUP_W), BF16), pltpu.VMEM((2, tm, GROUP_W), BF16),
                        pltpu.VMEM((2, tm, GROUP_W), BF16),
                        pltpu.VMEM((8, GROUP_W), F32),
                        pltpu.VMEM((N_HEADS, HEAD_W, HEAD_W), F32)],
        compiler_params=pltpu.CompilerParams(
            dimension_semantics=("arbitrary",), vmem_limit_bytes=VMEM_LIMIT),
        name="mix_out",
    )(*([h] * len(h_blocks)), yc, x, p_all, *small,
      wo_all, wpg_all, wpe_all, lng_all, lnb_all)


def kernel(x, p, w_in, conv_w, hgrn_lb, hgrn_norm_g, diff_lambda, diff_norm_g,
           sg_ln_g, sg_ln_b, sg_w, sg_b, w_out, ln_g, ln_b, w_pe, w_pg):
    batch, seq, d_model = x.shape
    depth = w_in.shape[0]
    t = batch * seq
    alpha = (2 * depth) ** 0.25

    lb_sm = jax.nn.softmax(hgrn_lb.astype(F32), axis=0)
    lower_bounds = jnp.cumsum(lb_sm, axis=0) - lb_sm[0]

    xf = x.reshape(t, d_model)
    xb = xf
    p_all = p.reshape(depth * t, p.shape[-1])
    lng_all = ln_g.reshape(depth, 1, d_model).astype(F32)
    lnb_all = ln_b.reshape(depth, 1, d_model).astype(F32)
    rng, masks = _hgrn_constants()
    for i in range(depth):
        lam_init = 0.8 - 0.6 * math.exp(-0.3 * i)
        dl = diff_lambda[i].astype(F32)
        lam = (jnp.exp(jnp.sum(dl[0] * dl[1])) - jnp.exp(jnp.sum(dl[2] * dl[3])) + lam_init)

        h = _in_proj(xb, w_in, i)
        yc, wo_b, wpg_b, wpe_b = _diff_attn(h, lam, lam_init, diff_norm_g[i],
                                            w_out, w_pg, w_pe, i, batch, seq)
        small = (conv_w[i].astype(F32),
                 sg_ln_g[i].reshape(1, GROUP_W).astype(F32), sg_ln_b[i].reshape(1, GROUP_W).astype(F32),
                 sg_w[i].astype(F32), sg_b[i].T.astype(F32),
                 lower_bounds[i].reshape(1, GROUP_W), hgrn_norm_g[i].reshape(1, GROUP_W).astype(F32),
                 rng, masks)
        xf, xb = _tail_block(h, yc, xf, p_all, small, wo_b, wpg_b, wpe_b,
                             lng_all, lnb_all, i, seq, alpha)
    return xf.reshape(batch, seq, d_model)
```

```python
import functools
import math

import numpy as np
import jax
import jax.numpy as jnp
from jax import lax
from jax.experimental import pallas as pl
from jax.experimental.pallas import tpu as pltpu

F32 = jnp.float32
BF16 = jnp.bfloat16

GROUP_W = 512
HEAD_W = 128
N_HEADS = GROUP_W // HEAD_W
DA_DQK = 64
F_FLOOR = 1e-30
MASK_VALUE = -1e30
LN_EPS = 1e-5
RMS_EPS = 1e-6

A_B, A_C, A_X, B_Q, B_F, B_I, C_Q, C_K, C_V, D_U, D_V, G_A, G_B, G_C, G_D = range(15)

HG_CHUNK = 128
HG_LEVELS = (2, 4, 8, 16, 32, 64, 128)
SG_CHUNK = 128
VMEM_LIMIT = 56 * 1024 * 1024


def _sigmoid(x):
    return 1.0 / (1.0 + jnp.exp(-x))


def _silu(x):
    return x * _sigmoid(x)


def _gelu_tanh(x):
    c = math.sqrt(2.0 / math.pi)
    return 0.5 * x * (1.0 + jnp.tanh(c * (x + 0.044715 * (x * x * x))))


def _matmul_kernel(x_ref, w_ref, o_ref):
    o_ref[...] = jnp.dot(x_ref[...].astype(BF16), w_ref[...].astype(BF16),
                         preferred_element_type=F32).astype(o_ref.dtype)


def _in_proj(xb, w_all, layer, tm=2048, tn=512):
    t, k = xb.shape
    n = w_all.shape[2]
    return pl.pallas_call(
        _matmul_kernel,
        grid=(t // tm, n // tn),
        in_specs=[pl.BlockSpec((tm, k), lambda i, j: (i, 0)),
                  pl.BlockSpec((None, k, tn), lambda i, j: (layer, 0, j))],
        out_specs=pl.BlockSpec((tm, tn), lambda i, j: (i, j)),
        out_shape=jax.ShapeDtypeStruct((t, n), BF16),
        compiler_params=pltpu.CompilerParams(
            dimension_semantics=("arbitrary", "arbitrary"),
            vmem_limit_bytes=VMEM_LIMIT),
        name="in_proj",
    )(xb, w_all)


def _conv_sgu_body(ab_ref, ac_ref, ax_ref, ga_ref, du_ref, dv_ref, gd_ref,
                   cw_ref, lng_ref, lnb_ref, ws_ref, bst_ref,
                   ya_ref, yd_ref, carry_ref, *, ts):
    z = ac_ref[...].astype(F32) * ax_ref[...].astype(F32)
    rows = lax.broadcasted_iota(jnp.int32, z.shape, 0)
    prev1 = carry_ref[7:8, :]
    prev2 = carry_ref[6:7, :]
    z1 = jnp.where(rows == 0, prev1, pltpu.roll(z, 1, 0))
    z2 = jnp.where(rows == 0, prev2, jnp.where(rows == 1, prev1, pltpu.roll(z, 2, 0)))
    cw = cw_ref[...]
    y = cw[0:1, :] * z2 + cw[1:2, :] * z1 + cw[2:3, :] * z
    ya = ab_ref[...].astype(F32) * y * _silu(ga_ref[...].astype(F32))
    ya_ref[...] = ya.astype(ya_ref.dtype)
    carry_ref[...] = z[ts - 8:ts, :]
    yield

    u = _gelu_tanh(du_ref[...].astype(F32))
    v = _gelu_tanh(dv_ref[...].astype(F32))
    mu = jnp.mean(v, axis=-1, keepdims=True)
    vc = v - mu
    var = jnp.mean(vc * vc, axis=-1, keepdims=True)
    vn = (vc * lax.rsqrt(var + LN_EPS) * lng_ref[...] + lnb_ref[...]).astype(BF16)
    gate = _silu(gd_ref[...].astype(F32))
    yield
    tri_r = lax.broadcasted_iota(jnp.int32, (SG_CHUNK, SG_CHUNK), 0)
    tri_c = lax.broadcasted_iota(jnp.int32, (SG_CHUNK, SG_CHUNK), 1)
    bst = bst_ref[...]
    for g in range(N_HEADS):
        w = jnp.where(tri_c <= tri_r, ws_ref[g], 0.0).astype(BF16)
        bias = bst[:, g:g + 1]
        lo = g * HEAD_W
        for c in range(ts // SG_CHUNK):
            r0 = c * SG_CHUNK
            sv = jnp.dot(w, vn[r0:r0 + SG_CHUNK, lo:lo + HEAD_W],
                         preferred_element_type=F32) + bias
            yd = u[r0:r0 + SG_CHUNK, lo:lo + HEAD_W] * sv * gate[r0:r0 + SG_CHUNK, lo:lo + HEAD_W]
            yd_ref[r0:r0 + SG_CHUNK, lo:lo + HEAD_W] = yd.astype(yd_ref.dtype)
        if g % 2 == 1:
            yield


def _hgrn_constants():
    c = HG_CHUNK
    t = np.arange(c)[:, None]
    s = np.arange(c)[None, :]
    mats = [(s <= t), (s > t)]
    for lv in HG_LEVELS:
        mid = (t // lv) * lv + lv // 2
        qside = t >= mid
        mats.append(np.where(qside, (s >= mid) & (s <= t), (s > t) & (s < mid)))
    rng = np.concatenate(mats, axis=0).astype(np.float32)
    rng = np.concatenate([rng, rng], axis=1)
    masks = np.stack([t == s] + [(t // lv) == (s // lv) for lv in HG_LEVELS]).astype(np.float32)
    masks = np.concatenate([masks, masks], axis=2)
    return jnp.asarray(rng, BF16), jnp.asarray(masks, F32)


def _hgrn_body(q_ref, f_ref, i_ref, g_ref, lb_ref, ng_ref, rng_ref, msk_ref,
               y_ref, st_ref, *, nsub):
    c = HG_CHUNK
    fz = f_ref[...].astype(F32)
    lb = lb_ref[...]
    e = jnp.exp(-jnp.abs(fz))
    r = 1.0 / (1.0 + e)
    pos = fz >= 0.0
    sig = jnp.where(pos, r, e * r)
    nsig = jnp.where(pos, e * r, r)
    f = lb + (1.0 - lb) * sig
    lg = jnp.log(jnp.maximum(f, F_FLOOR))
    kk_all = (1.0 - lb) * nsig

    g1 = lg.astype(BF16)
    g2 = (lg - g1.astype(F32)).astype(BF16)
    rng = rng_ref[...]
    exs = [jnp.dot(rng, jnp.concatenate([g1[u * c:(u + 1) * c], g2[u * c:(u + 1) * c]], axis=0),
                   preferred_element_type=F32)
           for u in range(nsub)]
    yield

    rows = lax.broadcasted_iota(jnp.int32, (c, HEAD_W), 0)
    zblk = jnp.zeros((c, HEAD_W), BF16)
    nt = (((1,), (1,)), ((), ()))

    def pair_rows(x0, x1):
        return jnp.concatenate([jnp.concatenate([x0, zblk], axis=1),
                                jnp.concatenate([zblk, x1], axis=1)], axis=0)

    for hp in range(N_HEADS // 2):
        los = (2 * hp * HEAD_W, (2 * hp + 1) * HEAD_W)
        sts = [st_ref[2 * hp], st_ref[2 * hp + 1]]
        for u in range(nsub):
            ex = exs[u]
            r0 = u * c
            qs, kks, vbs, outs = [], [], [], []
            for n, lo in enumerate(los):
                q = q_ref[r0:r0 + c, lo:lo + HEAD_W].astype(F32)
                vb = i_ref[r0:r0 + c, lo:lo + HEAD_W]
                kk = kk_all[r0:r0 + c, lo:lo + HEAD_W]
                bcum = ex[0:c, lo:lo + HEAD_W]
                q_in = (q * jnp.exp(bcum)).astype(BF16)
                outs.append(lax.dot_general(q_in, sts[n].astype(BF16), nt,
                                            preferred_element_type=F32))
                k_out = (kk * jnp.exp(ex[c:2 * c, lo:lo + HEAD_W])).astype(BF16)
                e_last = jnp.exp(bcum[c - 1:c, :])
                sts[n] = e_last * sts[n] + jnp.dot(vb.astype(F32).T.astype(BF16), k_out,
                                                   preferred_element_type=F32)
                qs.append(q)
                kks.append(kk)
                vbs.append(vb)
            yield

            a = msk_ref[0] * lax.dot_general(
                jnp.concatenate([qs[0].astype(BF16), qs[1].astype(BF16)], axis=1),
                pair_rows(kks[0].astype(BF16), kks[1].astype(BF16)), nt,
                preferred_element_type=F32)
            for li, lv in enumerate(HG_LEVELS):
                qside = (rows & (lv - 1)) >= (lv // 2)
                qls, kls = [], []
                for n, lo in enumerate(los):
                    el = jnp.exp(ex[(2 + li) * c:(3 + li) * c, lo:lo + HEAD_W])
                    qls.append(jnp.where(qside, qs[n] * el, 0.0).astype(BF16))
                    kls.append(jnp.where(qside, 0.0, kks[n] * el).astype(BF16))
                al = lax.dot_general(jnp.concatenate(qls, axis=1), pair_rows(kls[0], kls[1]), nt,
                                     preferred_element_type=F32)
                a = a + msk_ref[1 + li] * al
                if li % 2 == 0:
                    yield
            o2 = jnp.dot(a.astype(BF16), pair_rows(vbs[0], vbs[1]),
                         preferred_element_type=F32)

            for n, lo in enumerate(los):
                o = outs[n] + o2[:, n * HEAD_W:(n + 1) * HEAD_W]
                ms = jnp.mean(o * o, axis=-1, keepdims=True)
                on = o * lax.rsqrt(ms + RMS_EPS) * ng_ref[:, lo:lo + HEAD_W]
                y = on * _silu(g_ref[r0:r0 + c, lo:lo + HEAD_W].astype(F32))
                y_ref[r0:r0 + c, lo:lo + HEAD_W] = y.astype(y_ref.dtype)
            yield
        st_ref[2 * hp] = sts[0]
        st_ref[2 * hp + 1] = sts[1]


ATTN_ROWS = 64


def _diff_attn_kernel(lam_ref, q_ref, k_ref, v_ref, g_ref, ng_ref, wo_ref, wpg_ref, wpe_ref,
                      o_ref, wo_o, wpg_o, wpe_o,
                      qq_ref, pa_ref, pb_ref, pt_ref, pu_ref,
                      ma_ref, la_ref, aa_ref, mb_ref, lb_ref, ab_ref,
                      mt_ref, lt_ref, at_ref, mu_ref, lu_ref, au_ref,
                      acc_ref, *, tk, nq, out_scale):
    wo_o[...] = wo_ref[...].astype(BF16)
    wpg_o[...] = wpg_ref[...].astype(BF16)
    wpe_o[...] = wpe_ref[...].astype(BF16)

    lam = lam_ref[0, 0]
    scale = DA_DQK ** -0.5
    tq = 2 * tk
    nrow = 2 * tq
    ncb = tk // HEAD_W
    st_a = (ma_ref, la_ref, aa_ref)
    st_b = (mb_ref, lb_ref, ab_ref)
    st_t = (mt_ref, lt_ref, at_ref)
    st_u = (mu_ref, lu_ref, au_ref)

    def scores(j, ra=0, rb=nrow):
        kb = k_ref[pl.ds(pl.multiple_of(j * tk, tk), tk), :]
        return lax.dot_general(qq_ref[ra:rb, :], kb, (((1,), (1,)), ((), ())),
                               preferred_element_type=F32)

    def accumulate(j, p_ref, st, ra=0, rb=nrow):
        vb = v_ref[pl.ds(pl.multiple_of(j * tk, tk), tk), :]
        acc_ref[ra:rb, :] = (st[2][ra:rb, :] * acc_ref[ra:rb, :]
                             + jnp.dot(p_ref[ra:rb, :], vb, preferred_element_type=F32))

    def softmax(s_all, p_ref, st_in, st_out, diag=None, ra=0, rb=nrow):
        masked = diag is not None
        for r0 in range(ra, rb, ATTN_ROWS):
            rs = slice(r0, r0 + ATTN_ROWS)
            row_lo = (r0 // (2 * tk)) * tk + r0 % tk
            key_lo = diag * tk if masked else 0
            tiles = []
            for cb in range(ncb):
                k0 = key_lo + cb * HEAD_W
                if masked and k0 > row_lo + ATTN_ROWS - 1:
                    tiles.append(None)
                    continue
                s = s_all[r0 - ra:r0 - ra + ATTN_ROWS, cb * HEAD_W:(cb + 1) * HEAD_W]
                if masked and k0 + HEAD_W - 1 > row_lo:
                    r = row_lo + lax.broadcasted_iota(jnp.int32, s.shape, 0)
                    cidx = k0 + lax.broadcasted_iota(jnp.int32, s.shape, 1)
                    s = jnp.where(cidx <= r, s, MASK_VALUE)
                tiles.append(s)
            live = [s for s in tiles if s is not None]
            mx = functools.reduce(jnp.maximum, live)
            m_blk = jnp.broadcast_to(jnp.max(mx, axis=-1, keepdims=True), mx.shape)
            m_old = st_in[0][rs, :]
            m_new = jnp.maximum(m_old, m_blk)
            alpha = jnp.exp2(m_old - m_new)
            psum = None
            for cb, s in enumerate(tiles):
                if s is None:
                    p_ref[rs, cb * HEAD_W:(cb + 1) * HEAD_W] = jnp.zeros((ATTN_ROWS, HEAD_W), BF16)
                    continue
                p = jnp.exp2(s - m_new)
                psum = p if psum is None else psum + p
                p_ref[rs, cb * HEAD_W:(cb + 1) * HEAD_W] = p.astype(BF16)
            st_out[0][rs, :] = m_new
            st_out[1][rs, :] = alpha * st_in[1][rs, :] + psum
            st_out[2][rs, :] = alpha

    def step(j, p_in, p_out, st_in, st_out):
        s = scores(j)
        accumulate(jnp.maximum(j - 1, 0), p_in, st_in)
        softmax(s, p_out, st_in, st_out)

    def pair(jj, carry):
        step(2 * jj, pb_ref, pa_ref, st_b, st_a)
        step(2 * jj + 1, pa_ref, pb_ref, st_a, st_b)
        return carry

    def quad(qq, carry):
        pair(2 * qq, carry)
        return pair(2 * qq + 1, carry)

    def q_tile(i, carry):
        for half in range(2):
            rows = pl.ds(pl.multiple_of(i * tq + half * tk, tk), tk)
            q = (q_ref[rows, :].astype(F32) * (scale * math.log2(math.e))).astype(BF16)
            lane = lax.broadcasted_iota(jnp.int32, q.shape, 1)
            zero = jnp.zeros_like(q)
            qq_ref[2 * half * tk:(2 * half + 1) * tk, :] = jnp.where(lane < DA_DQK, q, zero)
            qq_ref[(2 * half + 1) * tk:(2 * half + 2) * tk, :] = jnp.where(lane >= DA_DQK, q, zero)
        acc_ref[...] = jnp.zeros_like(acc_ref)
        mb_ref[...] = jnp.full_like(mb_ref, MASK_VALUE)
        lb_ref[...] = jnp.zeros_like(lb_ref)
        ab_ref[...] = jnp.ones_like(ab_ref)
        pb_ref[...] = jnp.zeros_like(pb_ref)

        lax.fori_loop(0, i // 2, quad, 0)
        lax.fori_loop(2 * (i // 2), i, pair, 0)

        j0 = 2 * i
        s = scores(j0)
        accumulate(jnp.maximum(j0 - 1, 0), pb_ref, st_b)
        softmax(s, pt_ref, st_b, st_t, diag=0)
        s = scores(j0 + 1, tq, nrow)
        accumulate(j0, pt_ref, st_t)
        softmax(s, pu_ref, st_t, st_u, diag=1, ra=tq, rb=nrow)
        accumulate(j0 + 1, pu_ref, st_u, tq, nrow)

        for half, st in enumerate((st_t, st_u)):
            r0 = 2 * half * tk
            l = jnp.sum(st[1][r0:r0 + 2 * tk, :], axis=-1, keepdims=True)
            acc = acc_ref[r0:r0 + 2 * tk, :] / l
            o = acc[0:tk] - lam * acc[tk:2 * tk]
            ms = jnp.mean(o * o, axis=-1, keepdims=True)
            on = o * lax.rsqrt(ms + RMS_EPS) * ng_ref[...] * out_scale
            rows = pl.ds(pl.multiple_of(i * tq + half * tk, tk), tk)
            o_ref[rows, :] = (on * _silu(g_ref[rows, :].astype(F32))).astype(o_ref.dtype)
        return carry

    lax.fori_loop(0, nq, q_tile, 0)


def _diff_attn(h, lam, lam_init, norm_g, wo_all, wpg_all, wpe_all, layer, batch, seq, tk=512):
    t = h.shape[0]
    tq = 2 * tk
    nq = seq // tq
    per = GROUP_W // HEAD_W
    ng2 = norm_g.reshape(1, GROUP_W).astype(F32)
    lam2 = jnp.reshape(lam, (1, 1)).astype(F32)
    kern = functools.partial(_diff_attn_kernel, tk=tk, nq=nq, out_scale=1.0 - lam_init)
    cast_cols = wo_all.shape[2] // (batch * N_HEADS)

    def head_cols(blk):
        return pl.BlockSpec((seq, HEAD_W), lambda b, hh, blk=blk: (b, blk * per + hh))

    def cast_in(a):
        return pl.BlockSpec((None, a.shape[1], cast_cols),
                            lambda b, hh: (layer, 0, b * N_HEADS + hh))

    def cast_out(a):
        return pl.BlockSpec((a.shape[1], cast_cols), lambda b, hh: (0, b * N_HEADS + hh))

    weights = (wo_all, wpg_all, wpe_all)
    return pl.pallas_call(
        kern,
        grid=(batch, N_HEADS),
        in_specs=[pl.BlockSpec(memory_space=pltpu.SMEM),
                  head_cols(C_Q), head_cols(C_K), head_cols(C_V), head_cols(G_C),
                  pl.BlockSpec((1, HEAD_W), lambda b, hh: (0, hh))]
                 + [cast_in(a) for a in weights],
        out_specs=[pl.BlockSpec((seq, HEAD_W), lambda b, hh: (b, hh))]
                  + [cast_out(a) for a in weights],
        out_shape=[jax.ShapeDtypeStruct((t, GROUP_W), BF16)]
                  + [jax.ShapeDtypeStruct(a.shape[1:], BF16) for a in weights],
        scratch_shapes=[pltpu.VMEM((2 * tq, HEAD_W), BF16),
                        *([pltpu.VMEM((2 * tq, tk), BF16)] * 4),
                        *([pltpu.VMEM((2 * tq, HEAD_W), F32)] * 13)],
        compiler_params=pltpu.CompilerParams(
            dimension_semantics=("arbitrary", "arbitrary"),
            vmem_limit_bytes=VMEM_LIMIT),
        name="diff_attn",
    )(lam2, h, h, h, h, ng2, *weights)


TAIL_ROWS = 256


def _tail_kernel(ab_ref, ac_ref, ax_ref, ga_ref, du_ref, dv_ref, gd_ref,
                 bq_ref, bf_ref, bi_ref, gb_ref, yc_ref, x_ref, p_ref,
                 cw_ref, sglng_ref, sglnb_ref, ws_ref, bst_ref,
                 hlb_ref, hng_ref, rng_ref, msk_ref,
                 wo_ref, wpg_ref, wpe_ref, lng_ref, lnb_ref,
                 xo_ref, xb_ref,
                 ya_ref, yb_ref, yd_ref, carry_ref, st_ref, *, seq, alpha):
    n = pl.program_id(0)

    @pl.when((n * TAIL_ROWS) % seq == 0)
    def _():
        carry_ref[...] = jnp.zeros_like(carry_ref)
        st_ref[...] = jnp.zeros_like(st_ref)

    wslot = n % 2
    rslot = 1 - wslot

    def mixers():
        yield from _conv_sgu_body(ab_ref, ac_ref, ax_ref, ga_ref, du_ref, dv_ref, gd_ref,
                                  cw_ref, sglng_ref, sglnb_ref, ws_ref, bst_ref,
                                  ya_ref.at[wslot], yd_ref.at[wslot], carry_ref, ts=TAIL_ROWS)
        yield from _hgrn_body(bq_ref, bf_ref, bi_ref, gb_ref, hlb_ref, hng_ref, rng_ref, msk_ref,
                              yb_ref.at[wslot], st_ref, nsub=TAIL_ROWS // HG_CHUNK)

    def projections():
        d = x_ref.shape[-1]
        blocks = [slice(n0, n0 + GROUP_W) for n0 in range(0, d, GROUP_W)]
        ys = ((yc_ref, 2), (ya_ref.at[rslot], 0), (yd_ref.at[rslot], 3), (yb_ref.at[rslot], 1))
        accs = []
        for cols in blocks:
            acc = alpha * x_ref[:, cols]
            for y, g in ys:
                acc = acc + jnp.dot(y[...], wo_ref[g * GROUP_W:(g + 1) * GROUP_W, cols],
                                    preferred_element_type=F32)
                yield
            accs.append(acc)
        pe = jnp.dot(p_ref[...].astype(BF16), wpe_ref[...], preferred_element_type=F32)
        mu = sum(jnp.sum(a, axis=-1, keepdims=True) for a in accs) * (1.0 / d)
        xcs = [a - mu for a in accs]
        var = sum(jnp.sum(xc * xc, axis=-1, keepdims=True) for xc in xcs) * (1.0 / d)
        rstd = lax.rsqrt(var + LN_EPS)
        xns = [xc * rstd * lng_ref[:, cols] + lnb_ref[:, cols] for xc, cols in zip(xcs, blocks)]
        xnbs = [xn.astype(BF16) for xn in xns]
        yield
        for xn, cols in zip(xns, blocks):
            z = None
            for xnb, rows in zip(xnbs, blocks):
                part = jnp.dot(xnb, wpg_ref[rows, cols], preferred_element_type=F32)
                z = part if z is None else z + part
                yield
            out = xn + pe[:, cols] * _sigmoid(z)
            xo_ref[:, cols] = out
            xb_ref[:, cols] = out.astype(BF16)

    last = pl.num_programs(0) - 1

    @pl.when(n == 0)
    def _():
        for _ in mixers():
            pass

    @pl.when(n == last)
    def _():
        for _ in projections():
            pass

    @pl.when(jnp.logical_and(n > 0, n < last))
    def _():
        major, minor = mixers(), projections()
        live_major = live_minor = True
        while live_major or live_minor:
            live_major = live_major and next(major, False) is not False
            live_minor = live_minor and next(minor, False) is not False


def _tail_block(h, yc, x, p_all, small, wo_all, wpg_all, wpe_all, lng_all, lnb_all,
                layer, seq, alpha):
    t, d = x.shape
    tm = TAIL_ROWS
    pdim = p_all.shape[1]
    steps = t // tm

    def hblk(blk):
        return pl.BlockSpec((tm, GROUP_W), lambda i, blk=blk: (jnp.minimum(i, steps - 1), blk))

    def rows(w):
        return pl.BlockSpec((tm, w), lambda i: (jnp.maximum(i - 1, 0), 0))

    def full(a):
        nd = a.ndim
        return pl.BlockSpec(a.shape, lambda i, nd=nd: (0,) * nd)

    def resident(a):
        if a.ndim == 2:
            return pl.BlockSpec(a.shape, lambda i: (0, 0), pipeline_mode=pl.Buffered(1))
        return pl.BlockSpec((None,) + a.shape[1:], lambda i: (layer, 0, 0),
                            pipeline_mode=pl.Buffered(1))

    h_blocks = (A_B, A_C, A_X, G_A, D_U, D_V, G_D, B_Q, B_F, B_I, G_B)
    kern = functools.partial(_tail_kernel, seq=seq, alpha=alpha)
    return pl.pallas_call(
        kern,
        grid=(steps + 1,),
        in_specs=[hblk(b) for b in h_blocks]
                 + [rows(GROUP_W), rows(d),
                    pl.BlockSpec((tm, pdim),
                                 lambda i: (layer * steps + jnp.maximum(i - 1, 0), 0))]
                 + [full(a) for a in small]
                 + [resident(wo_all), resident(wpg_all), resident(wpe_all),
                    resident(lng_all), resident(lnb_all)],
        out_specs=[rows(d), rows(d)],
        out_shape=[jax.ShapeDtypeStruct((t, d), F32), jax.ShapeDtypeStruct((t, d), BF16)],
        scratch_shapes=[pltpu.VMEM((2, tm, GROUP_W), BF16), pltpu.VMEM((2, tm, GROUP_W), BF16),
                        pltpu.VMEM((2, tm, GROUP_W), BF16),
                        pltpu.VMEM((8, GROUP_W), F32),
                        pltpu.VMEM((N_HEADS, HEAD_W, HEAD_W), F32)],
        compiler_params=pltpu.CompilerParams(
            dimension_semantics=("arbitrary",), vmem_limit_bytes=VMEM_LIMIT),
        name="mix_out",
    )(*([h] * len(h_blocks)), yc, x, p_all, *small,
      wo_all, wpg_all, wpe_all, lng_all, lnb_all)


def kernel(x, p, w_in, conv_w, hgrn_lb, hgrn_norm_g, diff_lambda, diff_norm_g,
           sg_ln_g, sg_ln_b, sg_w, sg_b, w_out, ln_g, ln_b, w_pe, w_pg):
    batch, seq, d_model = x.shape
    depth = w_in.shape[0]
    t = batch * seq
    alpha = (2 * depth) ** 0.25

    lb_sm = jax.nn.softmax(hgrn_lb.astype(F32), axis=0)
    lower_bounds = jnp.cumsum(lb_sm, axis=0) - lb_sm[0]

    xf = x.reshape(t, d_model)
    xb = xf
    p_all = p.reshape(depth * t, p.shape[-1])
    lng_all = ln_g.reshape(depth, 1, d_model).astype(F32)
    lnb_all = ln_b.reshape(depth, 1, d_model).astype(F32)
    rng, masks = _hgrn_constants()
    for i in range(depth):
        lam_init = 0.8 - 0.6 * math.exp(-0.3 * i)
        dl = diff_lambda[i].astype(F32)
        lam = (jnp.exp(jnp.sum(dl[0] * dl[1])) - jnp.exp(jnp.sum(dl[2] * dl[3])) + lam_init)

        h = _in_proj(xb, w_in, i)
        yc, wo_b, wpg_b, wpe_b = _diff_attn(h, lam, lam_init, diff_norm_g[i],
                                            w_out, w_pg, w_pe, i, batch, seq)
        small = (conv_w[i].astype(F32),
                 sg_ln_g[i].reshape(1, GROUP_W).astype(F32), sg_ln_b[i].reshape(1, GROUP_W).astype(F32),
                 sg_w[i].astype(F32), sg_b[i].T.astype(F32),
                 lower_bounds[i].reshape(1, GROUP_W), hgrn_norm_g[i].reshape(1, GROUP_W).astype(F32),
                 rng, masks)
        xf, xb = _tail_block(h, yc, xf, p_all, small, wo_b, wpg_b, wpe_b,
                             lng_all, lnb_all, i, seq, alpha)
    return xf.reshape(batch, seq, d_model)
```

```python
import functools
import math

import numpy as np
import jax
import jax.numpy as jnp
from jax import lax
from jax.experimental import pallas as pl
from jax.experimental.pallas import tpu as pltpu

F32 = jnp.float32
BF16 = jnp.bfloat16

GROUP_W = 512
HEAD_W = 128
N_HEADS = GROUP_W // HEAD_W
DA_DQK = 64
F_FLOOR = 1e-30
MASK_VALUE = -1e30
LN_EPS = 1e-5
RMS_EPS = 1e-6

A_B, A_C, A_X, B_Q, B_F, B_I, C_Q, C_K, C_V, D_U, D_V, G_A, G_B, G_C, G_D = range(15)

HG_CHUNK = 128
HG_LEVELS = (2, 4, 8, 16, 32, 64, 128)
SG_CHUNK = 128
V7X_VMEM_BYTES = 64 * 1024 * 1024
VMEM_LIMIT = V7X_VMEM_BYTES - 8 * 1024 * 1024


def _sigmoid(x):
    return 1.0 / (1.0 + jnp.exp(-x))


def _silu(x):
    return x * _sigmoid(x)


def _gelu_tanh(x):
    c = math.sqrt(2.0 / math.pi)
    return 0.5 * x * (1.0 + jnp.tanh(c * (x + 0.044715 * (x * x * x))))


def _matmul_kernel(x_ref, w_ref, o_ref):
    o_ref[...] = jnp.dot(x_ref[...].astype(BF16), w_ref[...].astype(BF16),
                         preferred_element_type=F32).astype(o_ref.dtype)


def _in_proj(xb, w_all, layer, tm=2048):
    t, k = xb.shape
    n = w_all.shape[2]
    tn = 512 if xb.dtype == jnp.float32 else 768
    return pl.pallas_call(
        _matmul_kernel,
        grid=(t // tm, n // tn),
        in_specs=[pl.BlockSpec((tm, k), lambda i, j: (i, 0)),
                  pl.BlockSpec((None, k, tn), lambda i, j: (layer, 0, j))],
        out_specs=pl.BlockSpec((tm, tn), lambda i, j: (i, j)),
        out_shape=jax.ShapeDtypeStruct((t, n), BF16),
        compiler_params=pltpu.CompilerParams(
            dimension_semantics=("arbitrary", "arbitrary"),
            vmem_limit_bytes=VMEM_LIMIT),
        name="in_proj",
    )(xb, w_all)


def _conv_sgu_body(ab_ref, ac_ref, ax_ref, ga_ref, du_ref, dv_ref, gd_ref,
                   cw_ref, lng_ref, lnb_ref, ws_ref, bst_ref,
                   ya_ref, yd_ref, carry_ref, *, ts):
    z = ac_ref[...].astype(F32) * ax_ref[...].astype(F32)
    rows = lax.broadcasted_iota(jnp.int32, z.shape, 0)
    prev1 = carry_ref[7:8, :]
    prev2 = carry_ref[6:7, :]
    z1 = jnp.where(rows == 0, prev1, pltpu.roll(z, 1, 0))
    z2 = jnp.where(rows == 0, prev2, jnp.where(rows == 1, prev1, pltpu.roll(z, 2, 0)))
    cw = cw_ref[...]
    y = cw[0:1, :] * z2 + cw[1:2, :] * z1 + cw[2:3, :] * z
    ya = ab_ref[...].astype(F32) * y * _silu(ga_ref[...].astype(F32))
    ya_ref[...] = ya.astype(ya_ref.dtype)
    carry_ref[...] = z[ts - 8:ts, :]
    yield

    u = _gelu_tanh(du_ref[...].astype(F32))
    v = _gelu_tanh(dv_ref[...].astype(F32))
    mu = jnp.mean(v, axis=-1, keepdims=True)
    vc = v - mu
    var = jnp.mean(vc * vc, axis=-1, keepdims=True)
    vn = (vc * lax.rsqrt(var + LN_EPS) * lng_ref[...] + lnb_ref[...]).astype(BF16)
    gate = _silu(gd_ref[...].astype(F32))
    yield
    tri_r = lax.broadcasted_iota(jnp.int32, (SG_CHUNK, SG_CHUNK), 0)
    tri_c = lax.broadcasted_iota(jnp.int32, (SG_CHUNK, SG_CHUNK), 1)
    bst = bst_ref[...]
    for g in range(N_HEADS):
        w = jnp.where(tri_c <= tri_r, ws_ref[g], 0.0).astype(BF16)
        bias = bst[:, g:g + 1]
        lo = g * HEAD_W
        for c in range(ts // SG_CHUNK):
            r0 = c * SG_CHUNK
            sv = jnp.dot(w, vn[r0:r0 + SG_CHUNK, lo:lo + HEAD_W],
                         preferred_element_type=F32) + bias
            yd = u[r0:r0 + SG_CHUNK, lo:lo + HEAD_W] * sv * gate[r0:r0 + SG_CHUNK, lo:lo + HEAD_W]
            yd_ref[r0:r0 + SG_CHUNK, lo:lo + HEAD_W] = yd.astype(yd_ref.dtype)
        if g % 2 == 1:
            yield


def _hgrn_constants():
    c = HG_CHUNK
    t = np.arange(c)[:, None]
    s = np.arange(c)[None, :]
    mats = [(s <= t)]
    for lv in HG_LEVELS[1:]:
        mid = (t // lv) * lv + lv // 2
        qside = t >= mid
        mats.append(np.where(qside, (s >= mid) & (s <= t), (s > t) & (s < mid)))
    rng = np.concatenate(mats, axis=0).astype(np.float32)
    rng = np.concatenate([rng, rng], axis=1)
    masks = np.stack([t == s] + [(t // lv) == (s // lv) for lv in HG_LEVELS]).astype(np.float32)
    masks = np.concatenate([masks, masks], axis=2)
    return jnp.asarray(rng, BF16), jnp.asarray(masks, F32)


def _hgrn_body(q_ref, f_ref, i_ref, g_ref, lb_ref, ng_ref, rng_ref, msk_ref,
               y_ref, st_ref, *, nsub):
    c = HG_CHUNK
    fz = f_ref[...].astype(F32)
    lb = lb_ref[...]
    e = jnp.exp(-jnp.abs(fz))
    r = 1.0 / (1.0 + e)
    pos = fz >= 0.0
    sig = jnp.where(pos, r, e * r)
    nsig = jnp.where(pos, e * r, r)
    f_all = jnp.maximum(lb + (1.0 - lb) * sig, F_FLOOR)
    lg = jnp.log(f_all)
    kk_all = (1.0 - lb) * nsig

    g1 = lg.astype(BF16)
    g2 = (lg - g1.astype(F32)).astype(BF16)
    rng = rng_ref[...]
    exs = [jnp.dot(rng, jnp.concatenate([g1[u * c:(u + 1) * c], g2[u * c:(u + 1) * c]], axis=0),
                   preferred_element_type=F32)
           for u in range(nsub)]
    yield

    rows = lax.broadcasted_iota(jnp.int32, (c, HEAD_W), 0)
    zblk = jnp.zeros((c, HEAD_W), BF16)
    nt = (((1,), (1,)), ((), ()))

    def pair_rows(x0, x1):
        return jnp.concatenate([jnp.concatenate([x0, zblk], axis=1),
                                jnp.concatenate([zblk, x1], axis=1)], axis=0)

    for hp in range(N_HEADS // 2):
        los = (2 * hp * HEAD_W, (2 * hp + 1) * HEAD_W)
        sts = [st_ref[2 * hp], st_ref[2 * hp + 1]]
        for u in range(nsub):
            ex = exs[u]
            r0 = u * c
            qs, kks, vbs, outs = [], [], [], []
            for n, lo in enumerate(los):
                q = q_ref[r0:r0 + c, lo:lo + HEAD_W].astype(F32)
                vb = i_ref[r0:r0 + c, lo:lo + HEAD_W]
                kk = kk_all[r0:r0 + c, lo:lo + HEAD_W]
                bcum = ex[0:c, lo:lo + HEAD_W]
                q_in = (q * jnp.exp(bcum)).astype(BF16)
                outs.append(lax.dot_general(q_in, sts[n].astype(BF16), nt,
                                            preferred_element_type=F32))
                k_out = (kk * jnp.exp(bcum[c - 1:c, :] - bcum)).astype(BF16)
                e_last = jnp.exp(bcum[c - 1:c, :])
                sts[n] = e_last * sts[n] + jnp.dot(vb.astype(F32).T.astype(BF16), k_out,
                                                   preferred_element_type=F32)
                qs.append(q)
                kks.append(kk)
                vbs.append(vb)
            yield

            a = msk_ref[0] * lax.dot_general(
                jnp.concatenate([qs[0].astype(BF16), qs[1].astype(BF16)], axis=1),
                pair_rows(kks[0].astype(BF16), kks[1].astype(BF16)), nt,
                preferred_element_type=F32)
            for li, lv in enumerate(HG_LEVELS):
                qside = (rows & (lv - 1)) >= (lv // 2)
                qls, kls = [], []
                for n, lo in enumerate(los):
                    if lv == 2:
                        eq, ek = f_all[r0:r0 + c, lo:lo + HEAD_W], None
                    else:
                        eq = ek = jnp.exp(ex[li * c:(li + 1) * c, lo:lo + HEAD_W])
                    qls.append(jnp.where(qside, qs[n] * eq, 0.0).astype(BF16))
                    kls.append(jnp.where(qside, 0.0, kks[n] if ek is None else kks[n] * ek)
                               .astype(BF16))
                al = lax.dot_general(jnp.concatenate(qls, axis=1), pair_rows(kls[0], kls[1]), nt,
                                     preferred_element_type=F32)
                a = a + msk_ref[1 + li] * al
                if li % 2 == 0:
                    yield
            o2 = jnp.dot(a.astype(BF16), pair_rows(vbs[0], vbs[1]),
                         preferred_element_type=F32)

            for n, lo in enumerate(los):
                o = outs[n] + o2[:, n * HEAD_W:(n + 1) * HEAD_W]
                ms = jnp.mean(o * o, axis=-1, keepdims=True)
                on = o * lax.rsqrt(ms + RMS_EPS) * ng_ref[:, lo:lo + HEAD_W]
                y = on * _silu(g_ref[r0:r0 + c, lo:lo + HEAD_W].astype(F32))
                y_ref[r0:r0 + c, lo:lo + HEAD_W] = y.astype(y_ref.dtype)
            yield
        st_ref[2 * hp] = sts[0]
        st_ref[2 * hp + 1] = sts[1]


ATTN_ROWS = 64


def _diff_attn_kernel(lam_ref, q_ref, k_ref, v_ref, g_ref, ng_ref, wo_ref, wpg_ref, wpe_ref,
                      o_ref, wo_o, wpg_o, wpe_o,
                      qq_ref, pa_ref, pb_ref, pt_ref, pu_ref,
                      ma_ref, la_ref, aa_ref, mb_ref, lb_ref, ab_ref,
                      mt_ref, lt_ref, at_ref, mu_ref, lu_ref, au_ref,
                      acc_ref, *, tk, nq, out_scale):
    wo_o[...] = wo_ref[...].astype(BF16)
    wpg_o[...] = wpg_ref[...].astype(BF16)
    wpe_o[...] = wpe_ref[...].astype(BF16)

    lam = lam_ref[0, 0]
    scale = DA_DQK ** -0.5
    tq = 2 * tk
    nrow = 2 * tq
    ncb = tk // HEAD_W
    st_a = (ma_ref, la_ref, aa_ref)
    st_b = (mb_ref, lb_ref, ab_ref)
    st_t = (mt_ref, lt_ref, at_ref)
    st_u = (mu_ref, lu_ref, au_ref)

    def scores(j, ra=0, rb=nrow):
        kb = k_ref[pl.ds(pl.multiple_of(j * tk, tk), tk), :]
        return lax.dot_general(qq_ref[ra:rb, :], kb, (((1,), (1,)), ((), ())),
                               preferred_element_type=F32)

    def accumulate(j, p_ref, st, ra=0, rb=nrow):
        vb = v_ref[pl.ds(pl.multiple_of(j * tk, tk), tk), :]
        acc_ref[ra:rb, :] = (st[2][ra:rb, :] * acc_ref[ra:rb, :]
                             + jnp.dot(p_ref[ra:rb, :], vb, preferred_element_type=F32))

    def softmax(s_all, p_ref, st_in, st_out, diag=None, ra=0, rb=nrow):
        masked = diag is not None
        for r0 in range(ra, rb, ATTN_ROWS):
            rs = slice(r0, r0 + ATTN_ROWS)
            row_lo = (r0 // (2 * tk)) * tk + r0 % tk
            key_lo = diag * tk if masked else 0
            tiles = []
            for cb in range(ncb):
                k0 = key_lo + cb * HEAD_W
                if masked and k0 > row_lo + ATTN_ROWS - 1:
                    tiles.append(None)
                    continue
                s = s_all[r0 - ra:r0 - ra + ATTN_ROWS, cb * HEAD_W:(cb + 1) * HEAD_W]
                if masked and k0 + HEAD_W - 1 > row_lo:
                    r = row_lo + lax.broadcasted_iota(jnp.int32, s.shape, 0)
                    cidx = k0 + lax.broadcasted_iota(jnp.int32, s.shape, 1)
                    s = jnp.where(cidx <= r, s, MASK_VALUE)
                tiles.append(s)
            live = [s for s in tiles if s is not None]
            mx = functools.reduce(jnp.maximum, live)
            m_blk = jnp.broadcast_to(jnp.max(mx, axis=-1, keepdims=True), mx.shape)
            m_old = st_in[0][rs, :]
            m_new = jnp.maximum(m_old, m_blk)
            alpha = jnp.exp2(m_old - m_new)
            psum = None
            for cb, s in enumerate(tiles):
                if s is None:
                    p_ref[rs, cb * HEAD_W:(cb + 1) * HEAD_W] = jnp.zeros((ATTN_ROWS, HEAD_W), BF16)
                    continue
                p = jnp.exp2(s - m_new)
                psum = p if psum is None else psum + p
                p_ref[rs, cb * HEAD_W:(cb + 1) * HEAD_W] = p.astype(BF16)
            st_out[0][rs, :] = m_new
            st_out[1][rs, :] = alpha * st_in[1][rs, :] + psum
            st_out[2][rs, :] = alpha

    def step(j, p_in, p_out, st_in, st_out):
        s = scores(j)
        accumulate(jnp.maximum(j - 1, 0), p_in, st_in)
        softmax(s, p_out, st_in, st_out)

    def pair(jj, carry):
        step(2 * jj, pb_ref, pa_ref, st_b, st_a)
        step(2 * jj + 1, pa_ref, pb_ref, st_a, st_b)
        return carry

    def quad(qq, carry):
        pair(2 * qq, carry)
        return pair(2 * qq + 1, carry)

    def q_tile(i, carry):
        for half in range(2):
            rows = pl.ds(pl.multiple_of(i * tq + half * tk, tk), tk)
            q = (q_ref[rows, :].astype(F32) * (scale * math.log2(math.e))).astype(BF16)
            lane = lax.broadcasted_iota(jnp.int32, q.shape, 1)
            zero = jnp.zeros_like(q)
            qq_ref[2 * half * tk:(2 * half + 1) * tk, :] = jnp.where(lane < DA_DQK, q, zero)
            qq_ref[(2 * half + 1) * tk:(2 * half + 2) * tk, :] = jnp.where(lane >= DA_DQK, q, zero)
        acc_ref[...] = jnp.zeros_like(acc_ref)
        mb_ref[...] = jnp.full_like(mb_ref, MASK_VALUE)
        lb_ref[...] = jnp.zeros_like(lb_ref)
        ab_ref[...] = jnp.ones_like(ab_ref)
        pb_ref[...] = jnp.zeros_like(pb_ref)

        lax.fori_loop(0, i // 2, quad, 0)
        lax.fori_loop(2 * (i // 2), i, pair, 0)

        j0 = 2 * i
        s = scores(j0)
        accumulate(jnp.maximum(j0 - 1, 0), pb_ref, st_b)
        softmax(s, pt_ref, st_b, st_t, diag=0)
        s = scores(j0 + 1, tq, nrow)
        accumulate(j0, pt_ref, st_t)
        softmax(s, pu_ref, st_t, st_u, diag=1, ra=tq, rb=nrow)
        accumulate(j0 + 1, pu_ref, st_u, tq, nrow)

        for half, st in enumerate((st_t, st_u)):
            r0 = 2 * half * tk
            l = jnp.sum(st[1][r0:r0 + 2 * tk, :], axis=-1, keepdims=True)
            acc = acc_ref[r0:r0 + 2 * tk, :] / l
            o = acc[0:tk] - lam * acc[tk:2 * tk]
            ms = jnp.mean(o * o, axis=-1, keepdims=True)
            on = o * lax.rsqrt(ms + RMS_EPS) * ng_ref[...] * out_scale
            rows = pl.ds(pl.multiple_of(i * tq + half * tk, tk), tk)
            o_ref[rows, :] = (on * _silu(g_ref[rows, :].astype(F32))).astype(o_ref.dtype)
        return carry

    lax.fori_loop(0, nq, q_tile, 0)


def _diff_attn(h, lam, lam_init, norm_g, wo_all, wpg_all, wpe_all, layer, batch, seq, tk=512):
    t = h.shape[0]
    tq = 2 * tk
    nq = seq // tq
    per = GROUP_W // HEAD_W
    ng2 = norm_g.reshape(1, GROUP_W).astype(F32)
    lam2 = jnp.reshape(lam, (1, 1)).astype(F32)
    kern = functools.partial(_diff_attn_kernel, tk=tk, nq=nq, out_scale=1.0 - lam_init)
    cast_cols = wo_all.shape[2] // (batch * N_HEADS)

    def head_cols(blk):
        return pl.BlockSpec((seq, HEAD_W), lambda b, hh, blk=blk: (b, blk * per + hh))

    def cast_in(a):
        return pl.BlockSpec((None, a.shape[1], cast_cols),
                            lambda b, hh: (layer, 0, b * N_HEADS + hh))

    def cast_out(a):
        return pl.BlockSpec((a.shape[1], cast_cols), lambda b, hh: (0, b * N_HEADS + hh))

    weights = (wo_all, wpg_all, wpe_all)
    return pl.pallas_call(
        kern,
        grid=(batch, N_HEADS),
        in_specs=[pl.BlockSpec(memory_space=pltpu.SMEM),
                  head_cols(C_Q), head_cols(C_K), head_cols(C_V), head_cols(G_C),
                  pl.BlockSpec((1, HEAD_W), lambda b, hh: (0, hh))]
                 + [cast_in(a) for a in weights],
        out_specs=[pl.BlockSpec((seq, HEAD_W), lambda b, hh: (b, hh))]
                  + [cast_out(a) for a in weights],
        out_shape=[jax.ShapeDtypeStruct((t, GROUP_W), BF16)]
                  + [jax.ShapeDtypeStruct(a.shape[1:], BF16) for a in weights],
        scratch_shapes=[pltpu.VMEM((2 * tq, HEAD_W), BF16),
                        *([pltpu.VMEM((2 * tq, tk), BF16)] * 4),
                        *([pltpu.VMEM((2 * tq, HEAD_W), F32)] * 13)],
        compiler_params=pltpu.CompilerParams(
            dimension_semantics=("arbitrary", "arbitrary"),
            vmem_limit_bytes=VMEM_LIMIT),
        name="diff_attn",
    )(lam2, h, h, h, h, ng2, *weights)


TAIL_ROWS = 256


def _tail_kernel(ab_ref, ac_ref, ax_ref, ga_ref, du_ref, dv_ref, gd_ref,
                 bq_ref, bf_ref, bi_ref, gb_ref, yc_ref, x_ref, p_ref,
                 cw_ref, sglng_ref, sglnb_ref, ws_ref, bst_ref,
                 hlb_ref, hng_ref, rng_ref, msk_ref,
                 wo_ref, wpg_ref, wpe_ref, lng_ref, lnb_ref,
                 xo_ref, xb_ref,
                 ya_ref, yb_ref, yd_ref, carry_ref, st_ref, *, seq, alpha):
    n = pl.program_id(0)

    @pl.when((n * TAIL_ROWS) % seq == 0)
    def _():
        carry_ref[...] = jnp.zeros_like(carry_ref)
        st_ref[...] = jnp.zeros_like(st_ref)

    wslot = n % 2
    rslot = 1 - wslot

    def mixers():
        yield from _conv_sgu_body(ab_ref, ac_ref, ax_ref, ga_ref, du_ref, dv_ref, gd_ref,
                                  cw_ref, sglng_ref, sglnb_ref, ws_ref, bst_ref,
                                  ya_ref.at[wslot], yd_ref.at[wslot], carry_ref, ts=TAIL_ROWS)
        yield from _hgrn_body(bq_ref, bf_ref, bi_ref, gb_ref, hlb_ref, hng_ref, rng_ref, msk_ref,
                              yb_ref.at[wslot], st_ref, nsub=TAIL_ROWS // HG_CHUNK)

    def projections():
        d = x_ref.shape[-1]
        blocks = [slice(n0, n0 + GROUP_W) for n0 in range(0, d, GROUP_W)]
        ys = ((yc_ref, 2), (ya_ref.at[rslot], 0), (yd_ref.at[rslot], 3), (yb_ref.at[rslot], 1))
        accs = []
        for cols in blocks:
            acc = alpha * x_ref[:, cols]
            for y, g in ys:
                acc = acc + jnp.dot(y[...], wo_ref[g * GROUP_W:(g + 1) * GROUP_W, cols],
                                    preferred_element_type=F32)
                yield
            accs.append(acc)
        pe = jnp.dot(p_ref[...].astype(BF16), wpe_ref[...], preferred_element_type=F32)
        mu = sum(jnp.sum(a, axis=-1, keepdims=True) for a in accs) * (1.0 / d)
        xcs = [a - mu for a in accs]
        var = sum(jnp.sum(xc * xc, axis=-1, keepdims=True) for xc in xcs) * (1.0 / d)
        rstd = lax.rsqrt(var + LN_EPS)
        xns = [xc * rstd * lng_ref[:, cols] + lnb_ref[:, cols] for xc, cols in zip(xcs, blocks)]
        xnbs = [xn.astype(BF16) for xn in xns]
        yield
        for xn, cols in zip(xns, blocks):
            z = None
            for xnb, rows in zip(xnbs, blocks):
                part = jnp.dot(xnb, wpg_ref[rows, cols], preferred_element_type=F32)
                z = part if z is None else z + part
                yield
            out = xn + pe[:, cols] * _sigmoid(z)
            xo_ref[:, cols] = out
            xb_ref[:, cols] = out.astype(BF16)

    last = pl.num_programs(0) - 1

    @pl.when(n == 0)
    def _():
        for _ in mixers():
            pass

    @pl.when(n == last)
    def _():
        for _ in projections():
            pass

    @pl.when(jnp.logical_and(n > 0, n < last))
    def _():
        major, minor = mixers(), projections()
        live_major = live_minor = True
        while live_major or live_minor:
            live_major = live_major and next(major, False) is not False
            live_minor = live_minor and next(minor, False) is not False


def _tail_block(h, yc, x, p_all, small, wo_all, wpg_all, wpe_all, lng_all, lnb_all,
                layer, seq, alpha):
    t, d = x.shape
    tm = TAIL_ROWS
    pdim = p_all.shape[1]
    steps = t // tm

    def hblk(blk):
        return pl.BlockSpec((tm, GROUP_W), lambda i, blk=blk: (jnp.minimum(i, steps - 1), blk))

    def rows(w):
        return pl.BlockSpec((tm, w), lambda i: (jnp.maximum(i - 1, 0), 0))

    def full(a):
        nd = a.ndim
        return pl.BlockSpec(a.shape, lambda i, nd=nd: (0,) * nd)

    def resident(a):
        if a.ndim == 2:
            return pl.BlockSpec(a.shape, lambda i: (0, 0), pipeline_mode=pl.Buffered(1))
        return pl.BlockSpec((None,) + a.shape[1:], lambda i: (layer, 0, 0),
                            pipeline_mode=pl.Buffered(1))

    h_blocks = (A_B, A_C, A_X, G_A, D_U, D_V, G_D, B_Q, B_F, B_I, G_B)
    kern = functools.partial(_tail_kernel, seq=seq, alpha=alpha)
    return pl.pallas_call(
        kern,
        grid=(steps + 1,),
        in_specs=[hblk(b) for b in h_blocks]
                 + [rows(GROUP_W), rows(d),
                    pl.BlockSpec((tm, pdim),
                                 lambda i: (layer * steps + jnp.maximum(i - 1, 0), 0))]
                 + [full(a) for a in small]
                 + [resident(wo_all), resident(wpg_all), resident(wpe_all),
                    resident(lng_all), resident(lnb_all)],
        out_specs=[rows(d), rows(d)],
        out_shape=[jax.ShapeDtypeStruct((t, d), F32), jax.ShapeDtypeStruct((t, d), BF16)],
        scratch_shapes=[pltpu.VMEM((2, tm, GROUP_W), BF16), pltpu.VMEM((2, tm, GROUP_W), BF16),
                        pltpu.VMEM((2, tm, GROUP_W), BF16),
                        pltpu.VMEM((8, GROUP_W), F32),
                        pltpu.VMEM((N_HEADS, HEAD_W, HEAD_W), F32)],
        compiler_params=pltpu.CompilerParams(
            dimension_semantics=("arbitrary",), vmem_limit_bytes=VMEM_LIMIT),
        name="mix_out",
    )(*([h] * len(h_blocks)), yc, x, p_all, *small,
      wo_all, wpg_all, wpe_all, lng_all, lnb_all)


def kernel(x, p, w_in, conv_w, hgrn_lb, hgrn_norm_g, diff_lambda, diff_norm_g,
           sg_ln_g, sg_ln_b, sg_w, sg_b, w_out, ln_g, ln_b, w_pe, w_pg):
    batch, seq, d_model = x.shape
    depth = w_in.shape[0]
    t = batch * seq
    alpha = (2 * depth) ** 0.25

    lb_sm = jax.nn.softmax(hgrn_lb.astype(F32), axis=0)
    lower_bounds = jnp.cumsum(lb_sm, axis=0) - lb_sm[0]

    xf = x.reshape(t, d_model)
    xb = xf
    p_all = p.reshape(depth * t, p.shape[-1])
    lng_all = ln_g.reshape(depth, 1, d_model).astype(F32)
    lnb_all = ln_b.reshape(depth, 1, d_model).astype(F32)
    rng, masks = _hgrn_constants()
    for i in range(depth):
        lam_init = 0.8 - 0.6 * math.exp(-0.3 * i)
        dl = diff_lambda[i].astype(F32)
        lam = (jnp.exp(jnp.sum(dl[0] * dl[1])) - jnp.exp(jnp.sum(dl[2] * dl[3])) + lam_init)

        h = _in_proj(xb, w_in, i)
        yc, wo_b, wpg_b, wpe_b = _diff_attn(h, lam, lam_init, diff_norm_g[i],
                                            w_out, w_pg, w_pe, i, batch, seq)
        small = (conv_w[i].astype(F32),
                 sg_ln_g[i].reshape(1, GROUP_W).astype(F32), sg_ln_b[i].reshape(1, GROUP_W).astype(F32),
                 sg_w[i].astype(F32), sg_b[i].T.astype(F32),
                 lower_bounds[i].reshape(1, GROUP_W), hgrn_norm_g[i].reshape(1, GROUP_W).astype(F32),
                 rng, masks)
        xf, xb = _tail_block(h, yc, xf, p_all, small, wo_b, wpg_b, wpe_b,
                             lng_all, lnb_all, i, seq, alpha)
    return xf.reshape(batch, seq, d_model)
```

```python
import functools
import math

import numpy as np
import jax
import jax.numpy as jnp
from jax import lax
from jax.experimental import pallas as pl
from jax.experimental.pallas import tpu as pltpu

F32 = jnp.float32
BF16 = jnp.bfloat16

GROUP_W = 512
HEAD_W = 128
N_HEADS = GROUP_W // HEAD_W
DA_DQK = 64
F_FLOOR = 1e-30
MASK_VALUE = -1e30
LN_EPS = 1e-5
RMS_EPS = 1e-6

A_B, A_C, A_X, B_Q, B_F, B_I, C_Q, C_K, C_V, D_U, D_V, G_A, G_B, G_C, G_D = range(15)

HG_CHUNK = 128
HG_LEVELS = (2, 4, 8, 16, 32, 64, 128)
SG_CHUNK = 128
V7X_VMEM_BYTES = 64 * 1024 * 1024
VMEM_LIMIT = V7X_VMEM_BYTES - 8 * 1024 * 1024


def _sigmoid(x):
    return 1.0 / (1.0 + jnp.exp(-x))


def _silu(x):
    return x * _sigmoid(x)


def _gelu_tanh(x):
    c = math.sqrt(2.0 / math.pi)
    return 0.5 * x * (1.0 + jnp.tanh(c * (x + 0.044715 * (x * x * x))))


def _matmul_kernel(x_ref, w_ref, o_ref):
    o_ref[...] = jnp.dot(x_ref[...].astype(BF16), w_ref[...].astype(BF16),
                         preferred_element_type=F32).astype(o_ref.dtype)


def _matmul_f32x_kernel(x_hbm, w_ref, o_ref, xs_ref, xb_ref, sem, *, tm):
    i = pl.program_id(0)
    j = pl.program_id(1)

    def row_copy(r):
        return pltpu.make_async_copy(x_hbm.at[pl.ds(r * tm, tm), :], xs_ref, sem)

    @pl.when(jnp.logical_and(i == 0, j == 0))
    def _():
        row_copy(0).start()

    @pl.when(j == 0)
    def _():
        row_copy(i).wait()
        xb_ref[...] = xs_ref[...].astype(BF16)

        @pl.when(i + 1 < pl.num_programs(0))
        def _():
            row_copy(i + 1).start()

    o_ref[...] = jnp.dot(xb_ref[...], w_ref[...].astype(BF16),
                         preferred_element_type=F32).astype(o_ref.dtype)


def _in_proj(xb, w_all, layer, tm=2048, tn=768):
    t, k = xb.shape
    n = w_all.shape[2]
    if xb.dtype == jnp.float32:
        return pl.pallas_call(
            functools.partial(_matmul_f32x_kernel, tm=tm),
            grid=(t // tm, n // tn),
            in_specs=[pl.BlockSpec(memory_space=pl.ANY),
                      pl.BlockSpec((None, k, tn), lambda i, j: (layer, 0, j))],
            out_specs=pl.BlockSpec((tm, tn), lambda i, j: (i, j)),
            out_shape=jax.ShapeDtypeStruct((t, n), BF16),
            scratch_shapes=[pltpu.VMEM((tm, k), F32), pltpu.VMEM((tm, k), BF16),
                            pltpu.SemaphoreType.DMA(())],
            compiler_params=pltpu.CompilerParams(
                dimension_semantics=("arbitrary", "arbitrary"),
                vmem_limit_bytes=VMEM_LIMIT),
            name="in_proj",
        )(xb, w_all)
    return pl.pallas_call(
        _matmul_kernel,
        grid=(t // tm, n // tn),
        in_specs=[pl.BlockSpec((tm, k), lambda i, j: (i, 0)),
                  pl.BlockSpec((None, k, tn), lambda i, j: (layer, 0, j))],
        out_specs=pl.BlockSpec((tm, tn), lambda i, j: (i, j)),
        out_shape=jax.ShapeDtypeStruct((t, n), BF16),
        compiler_params=pltpu.CompilerParams(
            dimension_semantics=("arbitrary", "arbitrary"),
            vmem_limit_bytes=VMEM_LIMIT),
        name="in_proj",
    )(xb, w_all)


def _conv_sgu_body(ab_ref, ac_ref, ax_ref, ga_ref, du_ref, dv_ref, gd_ref,
                   cw_ref, lng_ref, lnb_ref, ws_ref, bst_ref,
                   ya_ref, yd_ref, carry_ref, *, ts):
    z = ac_ref[...].astype(F32) * ax_ref[...].astype(F32)
    rows = lax.broadcasted_iota(jnp.int32, z.shape, 0)
    prev1 = carry_ref[7:8, :]
    prev2 = carry_ref[6:7, :]
    z1 = jnp.where(rows == 0, prev1, pltpu.roll(z, 1, 0))
    z2 = jnp.where(rows == 0, prev2, jnp.where(rows == 1, prev1, pltpu.roll(z, 2, 0)))
    cw = cw_ref[...]
    y = cw[0:1, :] * z2 + cw[1:2, :] * z1 + cw[2:3, :] * z
    ya = ab_ref[...].astype(F32) * y * _silu(ga_ref[...].astype(F32))
    ya_ref[...] = ya.astype(ya_ref.dtype)
    carry_ref[...] = z[ts - 8:ts, :]
    yield

    u = _gelu_tanh(du_ref[...].astype(F32))
    v = _gelu_tanh(dv_ref[...].astype(F32))
    mu = jnp.mean(v, axis=-1, keepdims=True)
    vc = v - mu
    var = jnp.mean(vc * vc, axis=-1, keepdims=True)
    vn = (vc * lax.rsqrt(var + LN_EPS) * lng_ref[...] + lnb_ref[...]).astype(BF16)
    gate = _silu(gd_ref[...].astype(F32))
    yield
    tri_r = lax.broadcasted_iota(jnp.int32, (SG_CHUNK, SG_CHUNK), 0)
    tri_c = lax.broadcasted_iota(jnp.int32, (SG_CHUNK, SG_CHUNK), 1)
    bst = bst_ref[...]
    for g in range(N_HEADS):
        w = jnp.where(tri_c <= tri_r, ws_ref[g], 0.0).astype(BF16)
        bias = bst[:, g:g + 1]
        lo = g * HEAD_W
        for c in range(ts // SG_CHUNK):
            r0 = c * SG_CHUNK
            sv = jnp.dot(w, vn[r0:r0 + SG_CHUNK, lo:lo + HEAD_W],
                         preferred_element_type=F32) + bias
            yd = u[r0:r0 + SG_CHUNK, lo:lo + HEAD_W] * sv * gate[r0:r0 + SG_CHUNK, lo:lo + HEAD_W]
            yd_ref[r0:r0 + SG_CHUNK, lo:lo + HEAD_W] = yd.astype(yd_ref.dtype)
        if g % 2 == 1:
            yield


def _hgrn_constants():
    c = HG_CHUNK
    t = np.arange(c)[:, None]
    s = np.arange(c)[None, :]
    mats = [(s <= t)]
    for lv in HG_LEVELS[1:]:
        mid = (t // lv) * lv + lv // 2
        qside = t >= mid
        mats.append(np.where(qside, (s >= mid) & (s <= t), (s > t) & (s < mid)))
    rng = np.concatenate(mats, axis=0).astype(np.float32)
    rng = np.concatenate([rng, rng], axis=1)
    masks = np.stack([t == s] + [(t // lv) == (s // lv) for lv in HG_LEVELS]).astype(np.float32)
    masks = np.concatenate([masks, masks], axis=2)
    return jnp.asarray(rng, BF16), jnp.asarray(masks, F32)


def _hgrn_body(q_ref, f_ref, i_ref, g_ref, lb_ref, ng_ref, rng_ref, msk_ref,
               y_ref, st_ref, *, nsub):
    c = HG_CHUNK
    fz = f_ref[...].astype(F32)
    lb = lb_ref[...]
    e = jnp.exp(-jnp.abs(fz))
    r = 1.0 / (1.0 + e)
    pos = fz >= 0.0
    sig = jnp.where(pos, r, e * r)
    nsig = jnp.where(pos, e * r, r)
    f_all = jnp.maximum(lb + (1.0 - lb) * sig, F_FLOOR)
    lg = jnp.log(f_all)
    kk_all = (1.0 - lb) * nsig

    g1 = lg.astype(BF16)
    g2 = (lg - g1.astype(F32)).astype(BF16)
    rng = rng_ref[...]
    exs = [jnp.dot(rng, jnp.concatenate([g1[u * c:(u + 1) * c], g2[u * c:(u + 1) * c]], axis=0),
                   preferred_element_type=F32)
           for u in range(nsub)]
    yield

    rows = lax.broadcasted_iota(jnp.int32, (c, HEAD_W), 0)
    zblk = jnp.zeros((c, HEAD_W), BF16)
    nt = (((1,), (1,)), ((), ()))

    def pair_rows(x0, x1):
        return jnp.concatenate([jnp.concatenate([x0, zblk], axis=1),
                                jnp.concatenate([zblk, x1], axis=1)], axis=0)

    for hp in range(N_HEADS // 2):
        los = (2 * hp * HEAD_W, (2 * hp + 1) * HEAD_W)
        sts = [st_ref[2 * hp], st_ref[2 * hp + 1]]
        for u in range(nsub):
            ex = exs[u]
            r0 = u * c
            qs, kks, vbs, outs = [], [], [], []
            for n, lo in enumerate(los):
                q = q_ref[r0:r0 + c, lo:lo + HEAD_W].astype(F32)
                vb = i_ref[r0:r0 + c, lo:lo + HEAD_W]
                kk = kk_all[r0:r0 + c, lo:lo + HEAD_W]
                bcum = ex[0:c, lo:lo + HEAD_W]
                q_in = (q * jnp.exp(bcum)).astype(BF16)
                outs.append(lax.dot_general(q_in, sts[n].astype(BF16), nt,
                                            preferred_element_type=F32))
                k_out = (kk * jnp.exp(bcum[c - 1:c, :] - bcum)).astype(BF16)
                e_last = jnp.exp(bcum[c - 1:c, :])
                sts[n] = e_last * sts[n] + jnp.dot(vb.astype(F32).T.astype(BF16), k_out,
                                                   preferred_element_type=F32)
                qs.append(q)
                kks.append(kk)
                vbs.append(vb)
            yield

            a = msk_ref[0] * lax.dot_general(
                jnp.concatenate([qs[0].astype(BF16), qs[1].astype(BF16)], axis=1),
                pair_rows(kks[0].astype(BF16), kks[1].astype(BF16)), nt,
                preferred_element_type=F32)
            for li, lv in enumerate(HG_LEVELS):
                qside = (rows & (lv - 1)) >= (lv // 2)
                qls, kls = [], []
                for n, lo in enumerate(los):
                    if lv == 2:
                        eq, ek = f_all[r0:r0 + c, lo:lo + HEAD_W], None
                    else:
                        eq = ek = jnp.exp(ex[li * c:(li + 1) * c, lo:lo + HEAD_W])
                    qls.append(jnp.where(qside, qs[n] * eq, 0.0).astype(BF16))
                    kls.append(jnp.where(qside, 0.0, kks[n] if ek is None else kks[n] * ek)
                               .astype(BF16))
                al = lax.dot_general(jnp.concatenate(qls, axis=1), pair_rows(kls[0], kls[1]), nt,
                                     preferred_element_type=F32)
                a = a + msk_ref[1 + li] * al
                if li % 2 == 0:
                    yield
            o2 = jnp.dot(a.astype(BF16), pair_rows(vbs[0], vbs[1]),
                         preferred_element_type=F32)

            for n, lo in enumerate(los):
                o = outs[n] + o2[:, n * HEAD_W:(n + 1) * HEAD_W]
                ms = jnp.mean(o * o, axis=-1, keepdims=True)
                on = o * lax.rsqrt(ms + RMS_EPS) * ng_ref[:, lo:lo + HEAD_W]
                y = on * _silu(g_ref[r0:r0 + c, lo:lo + HEAD_W].astype(F32))
                y_ref[r0:r0 + c, lo:lo + HEAD_W] = y.astype(y_ref.dtype)
            yield
        st_ref[2 * hp] = sts[0]
        st_ref[2 * hp + 1] = sts[1]


ATTN_ROWS = 64


def _diff_attn_kernel(lam_ref, q_ref, k_ref, v_ref, g_ref, ng_ref, wo_ref, wpg_ref, wpe_ref,
                      o_ref, wo_o, wpg_o, wpe_o,
                      qq_ref, pa_ref, pb_ref, pt_ref, pu_ref,
                      ma_ref, la_ref, aa_ref, mb_ref, lb_ref, ab_ref,
                      mt_ref, lt_ref, at_ref, mu_ref, lu_ref, au_ref,
                      acc_ref, *, tk, nq, out_scale):
    wo_o[...] = wo_ref[...].astype(BF16)
    wpg_o[...] = wpg_ref[...].astype(BF16)
    wpe_o[...] = wpe_ref[...].astype(BF16)

    lam = lam_ref[0, 0]
    scale = DA_DQK ** -0.5
    tq = 2 * tk
    nrow = 2 * tq
    ncb = tk // HEAD_W
    st_a = (ma_ref, la_ref, aa_ref)
    st_b = (mb_ref, lb_ref, ab_ref)
    st_t = (mt_ref, lt_ref, at_ref)
    st_u = (mu_ref, lu_ref, au_ref)

    def scores(j, ra=0, rb=nrow):
        kb = k_ref[pl.ds(pl.multiple_of(j * tk, tk), tk), :]
        return lax.dot_general(qq_ref[ra:rb, :], kb, (((1,), (1,)), ((), ())),
                               preferred_element_type=F32)

    def accumulate(j, p_ref, st, ra=0, rb=nrow):
        vb = v_ref[pl.ds(pl.multiple_of(j * tk, tk), tk), :]
        acc_ref[ra:rb, :] = (st[2][ra:rb, :] * acc_ref[ra:rb, :]
                             + jnp.dot(p_ref[ra:rb, :], vb, preferred_element_type=F32))

    def softmax(s_all, p_ref, st_in, st_out, diag=None, ra=0, rb=nrow):
        masked = diag is not None
        for r0 in range(ra, rb, ATTN_ROWS):
            rs = slice(r0, r0 + ATTN_ROWS)
            row_lo = (r0 // (2 * tk)) * tk + r0 % tk
            key_lo = diag * tk if masked else 0
            tiles = []
            for cb in range(ncb):
                k0 = key_lo + cb * HEAD_W
                if masked and k0 > row_lo + ATTN_ROWS - 1:
                    tiles.append(None)
                    continue
                s = s_all[r0 - ra:r0 - ra + ATTN_ROWS, cb * HEAD_W:(cb + 1) * HEAD_W]
                if masked and k0 + HEAD_W - 1 > row_lo:
                    r = row_lo + lax.broadcasted_iota(jnp.int32, s.shape, 0)
                    cidx = k0 + lax.broadcasted_iota(jnp.int32, s.shape, 1)
                    s = jnp.where(cidx <= r, s, MASK_VALUE)
                tiles.append(s)
            live = [s for s in tiles if s is not None]
            mx = functools.reduce(jnp.maximum, live)
            m_blk = jnp.broadcast_to(jnp.max(mx, axis=-1, keepdims=True), mx.shape)
            m_old = st_in[0][rs, :]
            m_new = jnp.maximum(m_old, m_blk)
            alpha = jnp.exp2(m_old - m_new)
            psum = None
            for cb, s in enumerate(tiles):
                if s is None:
                    p_ref[rs, cb * HEAD_W:(cb + 1) * HEAD_W] = jnp.zeros((ATTN_ROWS, HEAD_W), BF16)
                    continue
                p = jnp.exp2(s - m_new)
                psum = p if psum is None else psum + p
                p_ref[rs, cb * HEAD_W:(cb + 1) * HEAD_W] = p.astype(BF16)
            st_out[0][rs, :] = m_new
            st_out[1][rs, :] = alpha * st_in[1][rs, :] + psum
            st_out[2][rs, :] = alpha

    def step(j, p_in, p_out, st_in, st_out):
        s = scores(j)
        accumulate(jnp.maximum(j - 1, 0), p_in, st_in)
        softmax(s, p_out, st_in, st_out)

    def pair(jj, carry):
        step(2 * jj, pb_ref, pa_ref, st_b, st_a)
        step(2 * jj + 1, pa_ref, pb_ref, st_a, st_b)
        return carry

    def quad(qq, carry):
        pair(2 * qq, carry)
        return pair(2 * qq + 1, carry)

    def q_tile(i, carry):
        for half in range(2):
            rows = pl.ds(pl.multiple_of(i * tq + half * tk, tk), tk)
            q = (q_ref[rows, :].astype(F32) * (scale * math.log2(math.e))).astype(BF16)
            lane = lax.broadcasted_iota(jnp.int32, q.shape, 1)
            zero = jnp.zeros_like(q)
            qq_ref[2 * half * tk:(2 * half + 1) * tk, :] = jnp.where(lane < DA_DQK, q, zero)
            qq_ref[(2 * half + 1) * tk:(2 * half + 2) * tk, :] = jnp.where(lane >= DA_DQK, q, zero)
        acc_ref[...] = jnp.zeros_like(acc_ref)
        mb_ref[...] = jnp.full_like(mb_ref, MASK_VALUE)
        lb_ref[...] = jnp.zeros_like(lb_ref)
        ab_ref[...] = jnp.ones_like(ab_ref)
        pb_ref[...] = jnp.zeros_like(pb_ref)

        lax.fori_loop(0, i // 2, quad, 0)
        lax.fori_loop(2 * (i // 2), i, pair, 0)

        j0 = 2 * i
        s = scores(j0)
        accumulate(jnp.maximum(j0 - 1, 0), pb_ref, st_b)
        softmax(s, pt_ref, st_b, st_t, diag=0)
        s = scores(j0 + 1, tq, nrow)
        accumulate(j0, pt_ref, st_t)
        softmax(s, pu_ref, st_t, st_u, diag=1, ra=tq, rb=nrow)
        accumulate(j0 + 1, pu_ref, st_u, tq, nrow)

        for half, st in enumerate((st_t, st_u)):
            r0 = 2 * half * tk
            l = jnp.sum(st[1][r0:r0 + 2 * tk, :], axis=-1, keepdims=True)
            acc = acc_ref[r0:r0 + 2 * tk, :] / l
            o = acc[0:tk] - lam * acc[tk:2 * tk]
            ms = jnp.mean(o * o, axis=-1, keepdims=True)
            on = o * lax.rsqrt(ms + RMS_EPS) * ng_ref[...] * out_scale
            rows = pl.ds(pl.multiple_of(i * tq + half * tk, tk), tk)
            o_ref[rows, :] = (on * _silu(g_ref[rows, :].astype(F32))).astype(o_ref.dtype)
        return carry

    lax.fori_loop(0, nq, q_tile, 0)


def _diff_attn(h, lam, lam_init, norm_g, wo_all, wpg_all, wpe_all, layer, batch, seq, tk=512):
    t = h.shape[0]
    tq = 2 * tk
    nq = seq // tq
    per = GROUP_W // HEAD_W
    ng2 = norm_g.reshape(1, GROUP_W).astype(F32)
    lam2 = jnp.reshape(lam, (1, 1)).astype(F32)
    kern = functools.partial(_diff_attn_kernel, tk=tk, nq=nq, out_scale=1.0 - lam_init)
    cast_cols = wo_all.shape[2] // (batch * N_HEADS)

    def head_cols(blk):
        return pl.BlockSpec((seq, HEAD_W), lambda b, hh, blk=blk: (b, blk * per + hh))

    def cast_in(a):
        return pl.BlockSpec((None, a.shape[1], cast_cols),
                            lambda b, hh: (layer, 0, b * N_HEADS + hh))

    def cast_out(a):
        return pl.BlockSpec((a.shape[1], cast_cols), lambda b, hh: (0, b * N_HEADS + hh))

    weights = (wo_all, wpg_all, wpe_all)
    return pl.pallas_call(
        kern,
        grid=(batch, N_HEADS),
        in_specs=[pl.BlockSpec(memory_space=pltpu.SMEM),
                  head_cols(C_Q), head_cols(C_K), head_cols(C_V), head_cols(G_C),
                  pl.BlockSpec((1, HEAD_W), lambda b, hh: (0, hh))]
                 + [cast_in(a) for a in weights],
        out_specs=[pl.BlockSpec((seq, HEAD_W), lambda b, hh: (b, hh))]
                  + [cast_out(a) for a in weights],
        out_shape=[jax.ShapeDtypeStruct((t, GROUP_W), BF16)]
                  + [jax.ShapeDtypeStruct(a.shape[1:], BF16) for a in weights],
        scratch_shapes=[pltpu.VMEM((2 * tq, HEAD_W), BF16),
                        *([pltpu.VMEM((2 * tq, tk), BF16)] * 4),
                        *([pltpu.VMEM((2 * tq, HEAD_W), F32)] * 13)],
        compiler_params=pltpu.CompilerParams(
            dimension_semantics=("arbitrary", "arbitrary"),
            vmem_limit_bytes=VMEM_LIMIT),
        name="diff_attn",
    )(lam2, h, h, h, h, ng2, *weights)


TAIL_ROWS = 256


def _tail_kernel(ab_ref, ac_ref, ax_ref, ga_ref, du_ref, dv_ref, gd_ref,
                 bq_ref, bf_ref, bi_ref, gb_ref, yc_ref, x_ref, p_ref,
                 cw_ref, sglng_ref, sglnb_ref, ws_ref, bst_ref,
                 hlb_ref, hng_ref, rng_ref, msk_ref,
                 wo_ref, wpg_ref, wpe_ref, lng_ref, lnb_ref,
                 xo_ref, xb_ref,
                 ya_ref, yb_ref, yd_ref, carry_ref, st_ref, *, seq, alpha):
    n = pl.program_id(0)

    @pl.when((n * TAIL_ROWS) % seq == 0)
    def _():
        carry_ref[...] = jnp.zeros_like(carry_ref)
        st_ref[...] = jnp.zeros_like(st_ref)

    wslot = n % 2
    rslot = 1 - wslot

    def mixers():
        yield from _conv_sgu_body(ab_ref, ac_ref, ax_ref, ga_ref, du_ref, dv_ref, gd_ref,
                                  cw_ref, sglng_ref, sglnb_ref, ws_ref, bst_ref,
                                  ya_ref.at[wslot], yd_ref.at[wslot], carry_ref, ts=TAIL_ROWS)
        yield from _hgrn_body(bq_ref, bf_ref, bi_ref, gb_ref, hlb_ref, hng_ref, rng_ref, msk_ref,
                              yb_ref.at[wslot], st_ref, nsub=TAIL_ROWS // HG_CHUNK)

    def projections():
        d = x_ref.shape[-1]
        blocks = [slice(n0, n0 + GROUP_W) for n0 in range(0, d, GROUP_W)]
        ys = ((yc_ref, 2), (ya_ref.at[rslot], 0), (yd_ref.at[rslot], 3), (yb_ref.at[rslot], 1))
        accs = []
        for cols in blocks:
            acc = alpha * x_ref[:, cols]
            for y, g in ys:
                acc = acc + jnp.dot(y[...], wo_ref[g * GROUP_W:(g + 1) * GROUP_W, cols],
                                    preferred_element_type=F32)
                yield
            accs.append(acc)
        pe = jnp.dot(p_ref[...].astype(BF16), wpe_ref[...], preferred_element_type=F32)
        mu = sum(jnp.sum(a, axis=-1, keepdims=True) for a in accs) * (1.0 / d)
        xcs = [a - mu for a in accs]
        var = sum(jnp.sum(xc * xc, axis=-1, keepdims=True) for xc in xcs) * (1.0 / d)
        rstd = lax.rsqrt(var + LN_EPS)
        xns = [xc * rstd * lng_ref[:, cols] + lnb_ref[:, cols] for xc, cols in zip(xcs, blocks)]
        xnbs = [xn.astype(BF16) for xn in xns]
        yield
        for xn, cols in zip(xns, blocks):
            z = None
            for xnb, rows in zip(xnbs, blocks):
                part = jnp.dot(xnb, wpg_ref[rows, cols], preferred_element_type=F32)
                z = part if z is None else z + part
                yield
            out = xn + pe[:, cols] * _sigmoid(z)
            xo_ref[:, cols] = out
            xb_ref[:, cols] = out.astype(BF16)

    last = pl.num_programs(0) - 1

    @pl.when(n == 0)
    def _():
        for _ in mixers():
            pass

    @pl.when(n == last)
    def _():
        for _ in projections():
            pass

    @pl.when(jnp.logical_and(n > 0, n < last))
    def _():
        major, minor = mixers(), projections()
        live_major = live_minor = True
        while live_major or live_minor:
            live_major = live_major and next(major, False) is not False
            live_minor = live_minor and next(minor, False) is not False


def _tail_block(h, yc, x, p_all, small, wo_all, wpg_all, wpe_all, lng_all, lnb_all,
                layer, seq, alpha):
    t, d = x.shape
    tm = TAIL_ROWS
    pdim = p_all.shape[1]
    steps = t // tm

    def hblk(blk):
        return pl.BlockSpec((tm, GROUP_W), lambda i, blk=blk: (jnp.minimum(i, steps - 1), blk))

    def rows(w):
        return pl.BlockSpec((tm, w), lambda i: (jnp.maximum(i - 1, 0), 0))

    def full(a):
        nd = a.ndim
        return pl.BlockSpec(a.shape, lambda i, nd=nd: (0,) * nd)

    def resident(a):
        if a.ndim == 2:
            return pl.BlockSpec(a.shape, lambda i: (0, 0), pipeline_mode=pl.Buffered(1))
        return pl.BlockSpec((None,) + a.shape[1:], lambda i: (layer, 0, 0),
                            pipeline_mode=pl.Buffered(1))

    h_blocks = (A_B, A_C, A_X, G_A, D_U, D_V, G_D, B_Q, B_F, B_I, G_B)
    kern = functools.partial(_tail_kernel, seq=seq, alpha=alpha)
    return pl.pallas_call(
        kern,
        grid=(steps + 1,),
        in_specs=[hblk(b) for b in h_blocks]
                 + [rows(GROUP_W), rows(d),
                    pl.BlockSpec((tm, pdim),
                                 lambda i: (layer * steps + jnp.maximum(i - 1, 0), 0))]
                 + [full(a) for a in small]
                 + [resident(wo_all), resident(wpg_all), resident(wpe_all),
                    resident(lng_all), resident(lnb_all)],
        out_specs=[rows(d), rows(d)],
        out_shape=[jax.ShapeDtypeStruct((t, d), F32), jax.ShapeDtypeStruct((t, d), BF16)],
        scratch_shapes=[pltpu.VMEM((2, tm, GROUP_W), BF16), pltpu.VMEM((2, tm, GROUP_W), BF16),
                        pltpu.VMEM((2, tm, GROUP_W), BF16),
                        pltpu.VMEM((8, GROUP_W), F32),
                        pltpu.VMEM((N_HEADS, HEAD_W, HEAD_W), F32)],
        compiler_params=pltpu.CompilerParams(
            dimension_semantics=("arbitrary",), vmem_limit_bytes=VMEM_LIMIT),
        name="mix_out",
    )(*([h] * len(h_blocks)), yc, x, p_all, *small,
      wo_all, wpg_all, wpe_all, lng_all, lnb_all)


def kernel(x, p, w_in, conv_w, hgrn_lb, hgrn_norm_g, diff_lambda, diff_norm_g,
           sg_ln_g, sg_ln_b, sg_w, sg_b, w_out, ln_g, ln_b, w_pe, w_pg):
    batch, seq, d_model = x.shape
    depth = w_in.shape[0]
    t = batch * seq
    alpha = (2 * depth) ** 0.25

    lb_sm = jax.nn.softmax(hgrn_lb.astype(F32), axis=0)
    lower_bounds = jnp.cumsum(lb_sm, axis=0) - lb_sm[0]

    xf = x.reshape(t, d_model)
    xb = xf
    p_all = p.reshape(depth * t, p.shape[-1])
    lng_all = ln_g.reshape(depth, 1, d_model).astype(F32)
    lnb_all = ln_b.reshape(depth, 1, d_model).astype(F32)
    rng, masks = _hgrn_constants()
    for i in range(depth):
        lam_init = 0.8 - 0.6 * math.exp(-0.3 * i)
        dl = diff_lambda[i].astype(F32)
        lam = (jnp.exp(jnp.sum(dl[0] * dl[1])) - jnp.exp(jnp.sum(dl[2] * dl[3])) + lam_init)

        h = _in_proj(xb, w_in, i)
        yc, wo_b, wpg_b, wpe_b = _diff_attn(h, lam, lam_init, diff_norm_g[i],
                                            w_out, w_pg, w_pe, i, batch, seq)
        small = (conv_w[i].astype(F32),
                 sg_ln_g[i].reshape(1, GROUP_W).astype(F32), sg_ln_b[i].reshape(1, GROUP_W).astype(F32),
                 sg_w[i].astype(F32), sg_b[i].T.astype(F32),
                 lower_bounds[i].reshape(1, GROUP_W), hgrn_norm_g[i].reshape(1, GROUP_W).astype(F32),
                 rng, masks)
        xf, xb = _tail_block(h, yc, xf, p_all, small, wo_b, wpg_b, wpe_b,
                             lng_all, lnb_all, i, seq, alpha)
    return xf.reshape(batch, seq, d_model)
```

```python
import functools
import math

import numpy as np
import jax
import jax.numpy as jnp
from jax import lax
from jax.experimental import pallas as pl
from jax.experimental.pallas import tpu as pltpu

F32 = jnp.float32
BF16 = jnp.bfloat16

GROUP_W = 512
HEAD_W = 128
N_HEADS = GROUP_W // HEAD_W
DA_DQK = 64
F_FLOOR = 1e-30
MASK_VALUE = -1e30
LN_EPS = 1e-5
RMS_EPS = 1e-6

A_B, A_C, A_X, B_Q, B_F, B_I, C_Q, C_K, C_V, D_U, D_V, G_A, G_B, G_C, G_D = range(15)

HG_CHUNK = 128
HG_LEVELS = (2, 4, 8, 16, 32, 64, 128)
SG_CHUNK = 128
V7X_VMEM_BYTES = 64 * 1024 * 1024
VMEM_LIMIT = V7X_VMEM_BYTES - 8 * 1024 * 1024


def _sigmoid(x):
    return 1.0 / (1.0 + jnp.exp(-x))


def _silu(x):
    return x * _sigmoid(x)


def _gelu_tanh(x):
    c = math.sqrt(2.0 / math.pi)
    return 0.5 * x * (1.0 + jnp.tanh(c * (x + 0.044715 * (x * x * x))))


def _matmul_kernel(x_ref, w_ref, o_ref):
    o_ref[...] = jnp.dot(x_ref[...].astype(BF16), w_ref[...].astype(BF16),
                         preferred_element_type=F32).astype(o_ref.dtype)


def _matmul_f32x_kernel(x_hbm, w_ref, o_ref, xs_ref, xb_ref, sem, *, tm):
    i = pl.program_id(0)
    j = pl.program_id(1)

    def row_copy(r):
        return pltpu.make_async_copy(x_hbm.at[pl.ds(r * tm, tm), :], xs_ref, sem)

    @pl.when(jnp.logical_and(i == 0, j == 0))
    def _():
        row_copy(0).start()

    @pl.when(j == 0)
    def _():
        row_copy(i).wait()
        xb_ref[...] = xs_ref[...].astype(BF16)

        @pl.when(i + 1 < pl.num_programs(0))
        def _():
            row_copy(i + 1).start()

    o_ref[...] = jnp.dot(xb_ref[...], w_ref[...].astype(BF16),
                         preferred_element_type=F32).astype(o_ref.dtype)


def _in_proj(xb, w_all, layer, tm=2048, tn=768):
    t, k = xb.shape
    n = w_all.shape[2]
    if xb.dtype == jnp.float32:
        return pl.pallas_call(
            functools.partial(_matmul_f32x_kernel, tm=tm),
            grid=(t // tm, n // tn),
            in_specs=[pl.BlockSpec(memory_space=pl.ANY),
                      pl.BlockSpec((None, k, tn), lambda i, j: (layer, 0, j))],
            out_specs=pl.BlockSpec((tm, tn), lambda i, j: (i, j)),
            out_shape=jax.ShapeDtypeStruct((t, n), BF16),
            scratch_shapes=[pltpu.VMEM((tm, k), F32), pltpu.VMEM((tm, k), BF16),
                            pltpu.SemaphoreType.DMA(())],
            compiler_params=pltpu.CompilerParams(
                dimension_semantics=("arbitrary", "arbitrary"),
                vmem_limit_bytes=VMEM_LIMIT),
            name="in_proj",
        )(xb, w_all)
    return pl.pallas_call(
        _matmul_kernel,
        grid=(t // tm, n // tn),
        in_specs=[pl.BlockSpec((tm, k), lambda i, j: (i, 0)),
                  pl.BlockSpec((None, k, tn), lambda i, j: (layer, 0, j))],
        out_specs=pl.BlockSpec((tm, tn), lambda i, j: (i, j)),
        out_shape=jax.ShapeDtypeStruct((t, n), BF16),
        compiler_params=pltpu.CompilerParams(
            dimension_semantics=("arbitrary", "arbitrary"),
            vmem_limit_bytes=VMEM_LIMIT),
        name="in_proj",
    )(xb, w_all)


def _conv_sgu_body(ab_ref, ac_ref, ax_ref, ga_ref, du_ref, dv_ref, gd_ref,
                   cw_ref, lng_ref, lnb_ref, ws_ref, bst_ref,
                   ya_ref, yd_ref, carry_ref, *, ts):
    z = ac_ref[...].astype(F32) * ax_ref[...].astype(F32)
    rows = lax.broadcasted_iota(jnp.int32, z.shape, 0)
    prev1 = carry_ref[7:8, :]
    prev2 = carry_ref[6:7, :]
    z1 = jnp.where(rows == 0, prev1, pltpu.roll(z, 1, 0))
    z2 = jnp.where(rows == 0, prev2, jnp.where(rows == 1, prev1, pltpu.roll(z, 2, 0)))
    cw = cw_ref[...]
    y = cw[0:1, :] * z2 + cw[1:2, :] * z1 + cw[2:3, :] * z
    ya = ab_ref[...].astype(F32) * y * _silu(ga_ref[...].astype(F32))
    ya_ref[...] = ya.astype(ya_ref.dtype)
    carry_ref[...] = z[ts - 8:ts, :]
    yield

    u = _gelu_tanh(du_ref[...].astype(F32))
    v = _gelu_tanh(dv_ref[...].astype(F32))
    mu = jnp.mean(v, axis=-1, keepdims=True)
    vc = v - mu
    var = jnp.mean(vc * vc, axis=-1, keepdims=True)
    vn = (vc * lax.rsqrt(var + LN_EPS) * lng_ref[...] + lnb_ref[...]).astype(BF16)
    gate = _silu(gd_ref[...].astype(F32))
    yield
    tri_r = lax.broadcasted_iota(jnp.int32, (SG_CHUNK, SG_CHUNK), 0)
    tri_c = lax.broadcasted_iota(jnp.int32, (SG_CHUNK, SG_CHUNK), 1)
    bst = bst_ref[...]
    for g in range(N_HEADS):
        w = jnp.where(tri_c <= tri_r, ws_ref[g], 0.0).astype(BF16)
        bias = bst[:, g:g + 1]
        lo = g * HEAD_W
        for c in range(ts // SG_CHUNK):
            r0 = c * SG_CHUNK
            sv = jnp.dot(w, vn[r0:r0 + SG_CHUNK, lo:lo + HEAD_W],
                         preferred_element_type=F32) + bias
            yd = u[r0:r0 + SG_CHUNK, lo:lo + HEAD_W] * sv * gate[r0:r0 + SG_CHUNK, lo:lo + HEAD_W]
            yd_ref[r0:r0 + SG_CHUNK, lo:lo + HEAD_W] = yd.astype(yd_ref.dtype)
        if g % 2 == 1:
            yield


def _hgrn_constants():
    c = HG_CHUNK
    t = np.arange(c)[:, None]
    s = np.arange(c)[None, :]
    mats = [(s <= t)]
    for lv in HG_LEVELS[1:]:
        mid = (t // lv) * lv + lv // 2
        qside = t >= mid
        mats.append(np.where(qside, (s >= mid) & (s <= t), (s > t) & (s < mid)))
    rng = np.concatenate(mats, axis=0).astype(np.float32)
    rng = np.concatenate([rng, rng], axis=1)
    masks = np.stack([t == s] + [(t // lv) == (s // lv) for lv in HG_LEVELS]).astype(np.float32)
    masks = np.concatenate([masks, masks], axis=2)
    return jnp.asarray(rng, BF16), jnp.asarray(masks, F32)


def _hgrn_body(q_ref, f_ref, i_ref, g_ref, lb_ref, ng_ref, rng_ref, msk_ref,
               y_ref, st_ref, *, nsub):
    c = HG_CHUNK
    fz = f_ref[...].astype(F32)
    lb = lb_ref[...]
    e = jnp.exp(-jnp.abs(fz))
    r = 1.0 / (1.0 + e)
    pos = fz >= 0.0
    sig = jnp.where(pos, r, e * r)
    nsig = jnp.where(pos, e * r, r)
    f_all = jnp.maximum(lb + (1.0 - lb) * sig, F_FLOOR)
    lg = jnp.log(f_all)
    kk_all = (1.0 - lb) * nsig

    g1 = lg.astype(BF16)
    g2 = (lg - g1.astype(F32)).astype(BF16)
    rng = rng_ref[...]
    exs = [jnp.dot(rng, jnp.concatenate([g1[u * c:(u + 1) * c], g2[u * c:(u + 1) * c]], axis=0),
                   preferred_element_type=F32)
           for u in range(nsub)]
    yield

    rows = lax.broadcasted_iota(jnp.int32, (c, HEAD_W), 0)
    zblk = jnp.zeros((c, HEAD_W), BF16)
    nt = (((1,), (1,)), ((), ()))

    def pair_rows(x0, x1):
        return jnp.concatenate([jnp.concatenate([x0, zblk], axis=1),
                                jnp.concatenate([zblk, x1], axis=1)], axis=0)

    for hp in range(N_HEADS // 2):
        los = (2 * hp * HEAD_W, (2 * hp + 1) * HEAD_W)
        sts = [st_ref[2 * hp], st_ref[2 * hp + 1]]
        for u in range(nsub):
            ex = exs[u]
            r0 = u * c
            qs, kks, vbs, outs = [], [], [], []
            for n, lo in enumerate(los):
                q = q_ref[r0:r0 + c, lo:lo + HEAD_W].astype(F32)
                vb = i_ref[r0:r0 + c, lo:lo + HEAD_W]
                kk = kk_all[r0:r0 + c, lo:lo + HEAD_W]
                bcum = ex[0:c, lo:lo + HEAD_W]
                q_in = (q * jnp.exp(bcum)).astype(BF16)
                outs.append(lax.dot_general(q_in, sts[n].astype(BF16), nt,
                                            preferred_element_type=F32))
                k_out = (kk * jnp.exp(bcum[c - 1:c, :] - bcum)).astype(BF16)
                e_last = jnp.exp(bcum[c - 1:c, :])
                sts[n] = e_last * sts[n] + jnp.dot(vb.astype(F32).T.astype(BF16), k_out,
                                                   preferred_element_type=F32)
                qs.append(q)
                kks.append(kk)
                vbs.append(vb)
            yield

            a = msk_ref[0] * lax.dot_general(
                jnp.concatenate([qs[0].astype(BF16), qs[1].astype(BF16)], axis=1),
                pair_rows(kks[0].astype(BF16), kks[1].astype(BF16)), nt,
                preferred_element_type=F32)
            for li, lv in enumerate(HG_LEVELS):
                qside = (rows & (lv - 1)) >= (lv // 2)
                qls, kls = [], []
                for n, lo in enumerate(los):
                    if lv == 2:
                        eq, ek = f_all[r0:r0 + c, lo:lo + HEAD_W], None
                    else:
                        eq = ek = jnp.exp(ex[li * c:(li + 1) * c, lo:lo + HEAD_W])
                    qls.append(jnp.where(qside, qs[n] * eq, 0.0).astype(BF16))
                    kls.append(jnp.where(qside, 0.0, kks[n] if ek is None else kks[n] * ek)
                               .astype(BF16))
                al = lax.dot_general(jnp.concatenate(qls, axis=1), pair_rows(kls[0], kls[1]), nt,
                                     preferred_element_type=F32)
                a = a + msk_ref[1 + li] * al
                if li % 2 == 0:
                    yield
            o2 = jnp.dot(a.astype(BF16), pair_rows(vbs[0], vbs[1]),
                         preferred_element_type=F32)

            for n, lo in enumerate(los):
                o = outs[n] + o2[:, n * HEAD_W:(n + 1) * HEAD_W]
                ms = jnp.mean(o * o, axis=-1, keepdims=True)
                on = o * lax.rsqrt(ms + RMS_EPS) * ng_ref[:, lo:lo + HEAD_W]
                y = on * _silu(g_ref[r0:r0 + c, lo:lo + HEAD_W].astype(F32))
                y_ref[r0:r0 + c, lo:lo + HEAD_W] = y.astype(y_ref.dtype)
            yield
        st_ref[2 * hp] = sts[0]
        st_ref[2 * hp + 1] = sts[1]


ATTN_ROWS = 64


def _diff_attn_kernel(lam_ref, q_ref, k_ref, v_ref, g_ref, ng_ref, wo_ref, wpg_ref, wpe_ref,
                      o_ref, wo_o, wpg_o, wpe_o,
                      qq_ref, pa_ref, pb_ref, pt_ref, pu_ref,
                      ma_ref, la_ref, aa_ref, mb_ref, lb_ref, ab_ref,
                      mt_ref, lt_ref, at_ref, mu_ref, lu_ref, au_ref,
                      acc_ref, *, tk, nq, out_scale):
    wo_o[...] = wo_ref[...].astype(BF16)
    wpg_o[...] = wpg_ref[...].astype(BF16)
    wpe_o[...] = wpe_ref[...].astype(BF16)

    lam = lam_ref[0, 0]
    scale = DA_DQK ** -0.5
    tq = 2 * tk
    nrow = 2 * tq
    st_a = (ma_ref, la_ref, aa_ref)
    st_b = (mb_ref, lb_ref, ab_ref)
    st_t = (mt_ref, lt_ref, at_ref)
    st_u = (mu_ref, lu_ref, au_ref)

    hq = tk // 2

    def scores(j, ra=0, rb=nrow, nkeys=tk):
        kb = k_ref[pl.ds(pl.multiple_of(j * tk, tk), nkeys), :]
        return lax.dot_general(qq_ref[ra:rb, :], kb, (((1,), (1,)), ((), ())),
                               preferred_element_type=F32)

    def accumulate(j, p_ref, st, ra=0, rb=nrow, nkeys=tk):
        vb = v_ref[pl.ds(pl.multiple_of(j * tk, tk), nkeys), :]
        acc_ref[ra:rb, :] = (st[2][ra:rb, :] * acc_ref[ra:rb, :]
                             + jnp.dot(p_ref[ra:rb, 0:nkeys], vb, preferred_element_type=F32))

    def softmax(s_all, p_ref, st_in, st_out, diag=None, ra=0, rb=nrow):
        masked = diag is not None
        for r0 in range(ra, rb, ATTN_ROWS):
            rs = slice(r0, r0 + ATTN_ROWS)
            row_lo = (r0 // tk) * hq + r0 % hq
            key_lo = diag * tk if masked else 0
            tiles = []
            for cb in range(s_all.shape[1] // HEAD_W):
                k0 = key_lo + cb * HEAD_W
                if masked and k0 > row_lo + ATTN_ROWS - 1:
                    tiles.append(None)
                    continue
                s = s_all[r0 - ra:r0 - ra + ATTN_ROWS, cb * HEAD_W:(cb + 1) * HEAD_W]
                if masked and k0 + HEAD_W - 1 > row_lo:
                    r = row_lo + lax.broadcasted_iota(jnp.int32, s.shape, 0)
                    cidx = k0 + lax.broadcasted_iota(jnp.int32, s.shape, 1)
                    s = jnp.where(cidx <= r, s, MASK_VALUE)
                tiles.append(s)
            live = [s for s in tiles if s is not None]
            mx = functools.reduce(jnp.maximum, live)
            m_blk = jnp.broadcast_to(jnp.max(mx, axis=-1, keepdims=True), mx.shape)
            m_old = st_in[0][rs, :]
            m_new = jnp.maximum(m_old, m_blk)
            alpha = jnp.exp2(m_old - m_new)
            psum = None
            for cb, s in enumerate(tiles):
                if s is None:
                    p_ref[rs, cb * HEAD_W:(cb + 1) * HEAD_W] = jnp.zeros((ATTN_ROWS, HEAD_W), BF16)
                    continue
                p = jnp.exp2(s - m_new)
                psum = p if psum is None else psum + p
                p_ref[rs, cb * HEAD_W:(cb + 1) * HEAD_W] = p.astype(BF16)
            st_out[0][rs, :] = m_new
            st_out[1][rs, :] = alpha * st_in[1][rs, :] + psum
            st_out[2][rs, :] = alpha

    def step(j, p_in, p_out, st_in, st_out):
        s = scores(j)
        accumulate(jnp.maximum(j - 1, 0), p_in, st_in)
        softmax(s, p_out, st_in, st_out)

    def pair(jj, carry):
        step(2 * jj, pb_ref, pa_ref, st_b, st_a)
        step(2 * jj + 1, pa_ref, pb_ref, st_a, st_b)
        return carry

    def quad(qq, carry):
        pair(2 * qq, carry)
        return pair(2 * qq + 1, carry)

    def q_tile(i, carry):
        for c in range(nrow // tk):
            rows = pl.ds(pl.multiple_of(i * tq + c * hq, hq), hq)
            q = (q_ref[rows, :].astype(F32) * (scale * math.log2(math.e))).astype(BF16)
            lane = lax.broadcasted_iota(jnp.int32, q.shape, 1)
            zero = jnp.zeros_like(q)
            qq_ref[c * tk:c * tk + hq, :] = jnp.where(lane < DA_DQK, q, zero)
            qq_ref[c * tk + hq:(c + 1) * tk, :] = jnp.where(lane >= DA_DQK, q, zero)
        acc_ref[...] = jnp.zeros_like(acc_ref)
        mb_ref[...] = jnp.full_like(mb_ref, MASK_VALUE)
        lb_ref[...] = jnp.zeros_like(lb_ref)
        ab_ref[...] = jnp.ones_like(ab_ref)
        pb_ref[...] = jnp.zeros_like(pb_ref)

        lax.fori_loop(0, i // 2, quad, 0)
        lax.fori_loop(2 * (i // 2), i, pair, 0)

        j0 = 2 * i
        s_q0 = scores(j0, 0, tk, hq)
        s_rest = scores(j0, tk, nrow)
        accumulate(jnp.maximum(j0 - 1, 0), pb_ref, st_b)
        softmax(s_q0, pt_ref, st_b, st_t, diag=0, ra=0, rb=tk)
        softmax(s_rest, pt_ref, st_b, st_t, diag=0, ra=tk, rb=nrow)
        s_q2 = scores(j0 + 1, 2 * tk, 3 * tk, hq)
        s_q3 = scores(j0 + 1, 3 * tk, nrow)
        accumulate(j0, pt_ref, st_t, 0, tk, hq)
        accumulate(j0, pt_ref, st_t, tk, nrow)
        softmax(s_q2, pu_ref, st_t, st_u, diag=1, ra=2 * tk, rb=3 * tk)
        softmax(s_q3, pu_ref, st_t, st_u, diag=1, ra=3 * tk, rb=nrow)
        accumulate(j0 + 1, pu_ref, st_u, 2 * tk, 3 * tk, hq)
        accumulate(j0 + 1, pu_ref, st_u, 3 * tk, nrow)

        for c in range(nrow // tk):
            st = st_t if c < 2 else st_u
            r0 = c * tk
            l = jnp.sum(st[1][r0:r0 + tk, :], axis=-1, keepdims=True)
            acc = acc_ref[r0:r0 + tk, :] / l
            o = acc[0:hq] - lam * acc[hq:tk]
            ms = jnp.mean(o * o, axis=-1, keepdims=True)
            on = o * lax.rsqrt(ms + RMS_EPS) * ng_ref[...] * out_scale
            rows = pl.ds(pl.multiple_of(i * tq + c * hq, hq), hq)
            o_ref[rows, :] = (on * _silu(g_ref[rows, :].astype(F32))).astype(o_ref.dtype)
        return carry

    lax.fori_loop(0, nq, q_tile, 0)


def _diff_attn(h, lam, lam_init, norm_g, wo_all, wpg_all, wpe_all, layer, batch, seq, tk=512):
    t = h.shape[0]
    tq = 2 * tk
    nq = seq // tq
    per = GROUP_W // HEAD_W
    ng2 = norm_g.reshape(1, GROUP_W).astype(F32)
    lam2 = jnp.reshape(lam, (1, 1)).astype(F32)
    kern = functools.partial(_diff_attn_kernel, tk=tk, nq=nq, out_scale=1.0 - lam_init)
    cast_cols = wo_all.shape[2] // (batch * N_HEADS)

    def head_cols(blk):
        return pl.BlockSpec((seq, HEAD_W), lambda b, hh, blk=blk: (b, blk * per + hh))

    def cast_in(a):
        return pl.BlockSpec((None, a.shape[1], cast_cols),
                            lambda b, hh: (layer, 0, b * N_HEADS + hh))

    def cast_out(a):
        return pl.BlockSpec((a.shape[1], cast_cols), lambda b, hh: (0, b * N_HEADS + hh))

    weights = (wo_all, wpg_all, wpe_all)
    return pl.pallas_call(
        kern,
        grid=(batch, N_HEADS),
        in_specs=[pl.BlockSpec(memory_space=pltpu.SMEM),
                  head_cols(C_Q), head_cols(C_K), head_cols(C_V), head_cols(G_C),
                  pl.BlockSpec((1, HEAD_W), lambda b, hh: (0, hh))]
                 + [cast_in(a) for a in weights],
        out_specs=[pl.BlockSpec((seq, HEAD_W), lambda b, hh: (b, hh))]
                  + [cast_out(a) for a in weights],
        out_shape=[jax.ShapeDtypeStruct((t, GROUP_W), BF16)]
                  + [jax.ShapeDtypeStruct(a.shape[1:], BF16) for a in weights],
        scratch_shapes=[pltpu.VMEM((2 * tq, HEAD_W), BF16),
                        *([pltpu.VMEM((2 * tq, tk), BF16)] * 4),
                        *([pltpu.VMEM((2 * tq, HEAD_W), F32)] * 13)],
        compiler_params=pltpu.CompilerParams(
            dimension_semantics=("arbitrary", "arbitrary"),
            vmem_limit_bytes=VMEM_LIMIT),
        name="diff_attn",
    )(lam2, h, h, h, h, ng2, *weights)


TAIL_ROWS = 256


def _tail_kernel(ab_ref, ac_ref, ax_ref, ga_ref, du_ref, dv_ref, gd_ref,
                 bq_ref, bf_ref, bi_ref, gb_ref, yc_ref, x_ref, p_ref,
                 cw_ref, sglng_ref, sglnb_ref, ws_ref, bst_ref,
                 hlb_ref, hng_ref, rng_ref, msk_ref,
                 wo_ref, wpg_ref, wpe_ref, lng_ref, lnb_ref,
                 xo_ref, xb_ref,
                 ya_ref, yb_ref, yd_ref, carry_ref, st_ref, *, seq, alpha):
    n = pl.program_id(0)

    @pl.when((n * TAIL_ROWS) % seq == 0)
    def _():
        carry_ref[...] = jnp.zeros_like(carry_ref)
        st_ref[...] = jnp.zeros_like(st_ref)

    wslot = n % 2
    rslot = 1 - wslot

    def mixers():
        yield from _conv_sgu_body(ab_ref, ac_ref, ax_ref, ga_ref, du_ref, dv_ref, gd_ref,
                                  cw_ref, sglng_ref, sglnb_ref, ws_ref, bst_ref,
                                  ya_ref.at[wslot], yd_ref.at[wslot], carry_ref, ts=TAIL_ROWS)
        yield from _hgrn_body(bq_ref, bf_ref, bi_ref, gb_ref, hlb_ref, hng_ref, rng_ref, msk_ref,
                              yb_ref.at[wslot], st_ref, nsub=TAIL_ROWS // HG_CHUNK)

    def projections():
        d = x_ref.shape[-1]
        blocks = [slice(n0, n0 + GROUP_W) for n0 in range(0, d, GROUP_W)]
        ys = ((yc_ref, 2), (ya_ref.at[rslot], 0), (yd_ref.at[rslot], 3), (yb_ref.at[rslot], 1))
        accs = []
        for cols in blocks:
            acc = alpha * x_ref[:, cols]
            for y, g in ys:
                acc = acc + jnp.dot(y[...], wo_ref[g * GROUP_W:(g + 1) * GROUP_W, cols],
                                    preferred_element_type=F32)
                yield
            accs.append(acc)
        pe = jnp.dot(p_ref[...].astype(BF16), wpe_ref[...], preferred_element_type=F32)
        mu = sum(jnp.sum(a, axis=-1, keepdims=True) for a in accs) * (1.0 / d)
        xcs = [a - mu for a in accs]
        var = sum(jnp.sum(xc * xc, axis=-1, keepdims=True) for xc in xcs) * (1.0 / d)
        rstd = lax.rsqrt(var + LN_EPS)
        xns = [xc * rstd * lng_ref[:, cols] + lnb_ref[:, cols] for xc, cols in zip(xcs, blocks)]
        xnbs = [xn.astype(BF16) for xn in xns]
        yield
        for xn, cols in zip(xns, blocks):
            z = None
            for xnb, rows in zip(xnbs, blocks):
                part = jnp.dot(xnb, wpg_ref[rows, cols], preferred_element_type=F32)
                z = part if z is None else z + part
                yield
            out = xn + pe[:, cols] * _sigmoid(z)
            xo_ref[:, cols] = out
            xb_ref[:, cols] = out.astype(BF16)

    last = pl.num_programs(0) - 1

    @pl.when(n == 0)
    def _():
        for _ in mixers():
            pass

    @pl.when(n == last)
    def _():
        for _ in projections():
            pass

    @pl.when(jnp.logical_and(n > 0, n < last))
    def _():
        major, minor = mixers(), projections()
        live_major = live_minor = True
        while live_major or live_minor:
            live_major = live_major and next(major, False) is not False
            live_minor = live_minor and next(minor, False) is not False


def _tail_block(h, yc, x, p_all, small, wo_all, wpg_all, wpe_all, lng_all, lnb_all,
                layer, seq, alpha):
    t, d = x.shape
    tm = TAIL_ROWS
    pdim = p_all.shape[1]
    steps = t // tm

    def hblk(blk):
        return pl.BlockSpec((tm, GROUP_W), lambda i, blk=blk: (jnp.minimum(i, steps - 1), blk))

    def rows(w):
        return pl.BlockSpec((tm, w), lambda i: (jnp.maximum(i - 1, 0), 0))

    def full(a):
        nd = a.ndim
        return pl.BlockSpec(a.shape, lambda i, nd=nd: (0,) * nd)

    def resident(a):
        if a.ndim == 2:
            return pl.BlockSpec(a.shape, lambda i: (0, 0), pipeline_mode=pl.Buffered(1))
        return pl.BlockSpec((None,) + a.shape[1:], lambda i: (layer, 0, 0),
                            pipeline_mode=pl.Buffered(1))

    h_blocks = (A_B, A_C, A_X, G_A, D_U, D_V, G_D, B_Q, B_F, B_I, G_B)
    kern = functools.partial(_tail_kernel, seq=seq, alpha=alpha)
    return pl.pallas_call(
        kern,
        grid=(steps + 1,),
        in_specs=[hblk(b) for b in h_blocks]
                 + [rows(GROUP_W), rows(d),
                    pl.BlockSpec((tm, pdim),
                                 lambda i: (layer * steps + jnp.maximum(i - 1, 0), 0))]
                 + [full(a) for a in small]
                 + [resident(wo_all), resident(wpg_all), resident(wpe_all),
                    resident(lng_all), resident(lnb_all)],
        out_specs=[rows(d), rows(d)],
        out_shape=[jax.ShapeDtypeStruct((t, d), F32), jax.ShapeDtypeStruct((t, d), BF16)],
        scratch_shapes=[pltpu.VMEM((2, tm, GROUP_W), BF16), pltpu.VMEM((2, tm, GROUP_W), BF16),
                        pltpu.VMEM((2, tm, GROUP_W), BF16),
                        pltpu.VMEM((8, GROUP_W), F32),
                        pltpu.VMEM((N_HEADS, HEAD_W, HEAD_W), F32)],
        compiler_params=pltpu.CompilerParams(
            dimension_semantics=("arbitrary",), vmem_limit_bytes=VMEM_LIMIT),
        name="mix_out",
    )(*([h] * len(h_blocks)), yc, x, p_all, *small,
      wo_all, wpg_all, wpe_all, lng_all, lnb_all)


def kernel(x, p, w_in, conv_w, hgrn_lb, hgrn_norm_g, diff_lambda, diff_norm_g,
           sg_ln_g, sg_ln_b, sg_w, sg_b, w_out, ln_g, ln_b, w_pe, w_pg):
    batch, seq, d_model = x.shape
    depth = w_in.shape[0]
    t = batch * seq
    alpha = (2 * depth) ** 0.25

    lb_sm = jax.nn.softmax(hgrn_lb.astype(F32), axis=0)
    lower_bounds = jnp.cumsum(lb_sm, axis=0) - lb_sm[0]

    xf = x.reshape(t, d_model)
    xb = xf
    p_all = p.reshape(depth * t, p.shape[-1])
    lng_all = ln_g.reshape(depth, 1, d_model).astype(F32)
    lnb_all = ln_b.reshape(depth, 1, d_model).astype(F32)
    rng, masks = _hgrn_constants()
    for i in range(depth):
        lam_init = 0.8 - 0.6 * math.exp(-0.3 * i)
        dl = diff_lambda[i].astype(F32)
        lam = (jnp.exp(jnp.sum(dl[0] * dl[1])) - jnp.exp(jnp.sum(dl[2] * dl[3])) + lam_init)

        h = _in_proj(xb, w_in, i)
        yc, wo_b, wpg_b, wpe_b = _diff_attn(h, lam, lam_init, diff_norm_g[i],
                                            w_out, w_pg, w_pe, i, batch, seq)
        small = (conv_w[i].astype(F32),
                 sg_ln_g[i].reshape(1, GROUP_W).astype(F32), sg_ln_b[i].reshape(1, GROUP_W).astype(F32),
                 sg_w[i].astype(F32), sg_b[i].T.astype(F32),
                 lower_bounds[i].reshape(1, GROUP_W), hgrn_norm_g[i].reshape(1, GROUP_W).astype(F32),
                 rng, masks)
        xf, xb = _tail_block(h, yc, xf, p_all, small, wo_b, wpg_b, wpe_b,
                             lng_all, lnb_all, i, seq, alpha)
    return xf.reshape(batch, seq, d_model)
```

```python
import functools
import math

import numpy as np
import jax
import jax.numpy as jnp
from jax import lax
from jax.experimental import pallas as pl
from jax.experimental.pallas import tpu as pltpu

F32 = jnp.float32
BF16 = jnp.bfloat16

GROUP_W = 512
HEAD_W = 128
N_HEADS = GROUP_W // HEAD_W
DA_DQK = 64
F_FLOOR = 1e-30
MASK_VALUE = -1e30
LN_EPS = 1e-5
RMS_EPS = 1e-6

A_B, A_C, A_X, B_Q, B_F, B_I, C_Q, C_K, C_V, D_U, D_V, G_A, G_B, G_C, G_D = range(15)

HG_CHUNK = 128
HG_LEVELS = (2, 4, 8, 16, 32, 64, 128)
SG_CHUNK = 128
V7X_VMEM_BYTES = 64 * 1024 * 1024
VMEM_LIMIT = V7X_VMEM_BYTES - 8 * 1024 * 1024


def _sigmoid(x):
    return 1.0 / (1.0 + jnp.exp(-x))


def _silu(x):
    return x * _sigmoid(x)


def _gelu_tanh(x):
    c = math.sqrt(2.0 / math.pi)
    return 0.5 * x * (1.0 + jnp.tanh(c * (x + 0.044715 * (x * x * x))))


def _matmul_kernel(x_ref, w_ref, o_ref):
    o_ref[...] = jnp.dot(x_ref[...].astype(BF16), w_ref[...].astype(BF16),
                         preferred_element_type=F32).astype(o_ref.dtype)


def _matmul_f32x_kernel(x_hbm, w_ref, o_ref, xs_ref, xb_ref, sem, *, tm):
    i = pl.program_id(0)
    j = pl.program_id(1)

    def row_copy(r):
        return pltpu.make_async_copy(x_hbm.at[pl.ds(r * tm, tm), :], xs_ref, sem)

    @pl.when(jnp.logical_and(i == 0, j == 0))
    def _():
        row_copy(0).start()

    @pl.when(j == 0)
    def _():
        row_copy(i).wait()
        xb_ref[...] = xs_ref[...].astype(BF16)

        @pl.when(i + 1 < pl.num_programs(0))
        def _():
            row_copy(i + 1).start()

    o_ref[...] = jnp.dot(xb_ref[...], w_ref[...].astype(BF16),
                         preferred_element_type=F32).astype(o_ref.dtype)


def _in_proj(xb, w_all, layer, tm=2048, tn=768):
    t, k = xb.shape
    n = w_all.shape[2]
    if xb.dtype == jnp.float32:
        return pl.pallas_call(
            functools.partial(_matmul_f32x_kernel, tm=tm),
            grid=(t // tm, n // tn),
            in_specs=[pl.BlockSpec(memory_space=pl.ANY),
                      pl.BlockSpec((None, k, tn), lambda i, j: (layer, 0, j))],
            out_specs=pl.BlockSpec((tm, tn), lambda i, j: (i, j)),
            out_shape=jax.ShapeDtypeStruct((t, n), BF16),
            scratch_shapes=[pltpu.VMEM((tm, k), F32), pltpu.VMEM((tm, k), BF16),
                            pltpu.SemaphoreType.DMA(())],
            compiler_params=pltpu.CompilerParams(
                dimension_semantics=("arbitrary", "arbitrary"),
                vmem_limit_bytes=VMEM_LIMIT),
            name="in_proj",
        )(xb, w_all)
    return pl.pallas_call(
        _matmul_kernel,
        grid=(t // tm, n // tn),
        in_specs=[pl.BlockSpec((tm, k), lambda i, j: (i, 0)),
                  pl.BlockSpec((None, k, tn), lambda i, j: (layer, 0, j))],
        out_specs=pl.BlockSpec((tm, tn), lambda i, j: (i, j)),
        out_shape=jax.ShapeDtypeStruct((t, n), BF16),
        compiler_params=pltpu.CompilerParams(
            dimension_semantics=("arbitrary", "arbitrary"),
            vmem_limit_bytes=VMEM_LIMIT),
        name="in_proj",
    )(xb, w_all)


def _conv_sgu_body(ab_ref, ac_ref, ax_ref, ga_ref, du_ref, dv_ref, gd_ref,
                   cw_ref, lng_ref, lnb_ref, ws_ref, bst_ref,
                   ya_ref, yd_ref, carry_ref, *, ts):
    z = ac_ref[...].astype(F32) * ax_ref[...].astype(F32)
    rows = lax.broadcasted_iota(jnp.int32, z.shape, 0)
    prev1 = carry_ref[7:8, :]
    prev2 = carry_ref[6:7, :]
    z1 = jnp.where(rows == 0, prev1, pltpu.roll(z, 1, 0))
    z2 = jnp.where(rows == 0, prev2, jnp.where(rows == 1, prev1, pltpu.roll(z, 2, 0)))
    cw = cw_ref[...]
    y = cw[0:1, :] * z2 + cw[1:2, :] * z1 + cw[2:3, :] * z
    ya = ab_ref[...].astype(F32) * y * _silu(ga_ref[...].astype(F32))
    ya_ref[...] = ya.astype(ya_ref.dtype)
    carry_ref[...] = z[ts - 8:ts, :]
    yield

    u = _gelu_tanh(du_ref[...].astype(F32))
    v = _gelu_tanh(dv_ref[...].astype(F32))
    mu = jnp.mean(v, axis=-1, keepdims=True)
    vc = v - mu
    var = jnp.mean(vc * vc, axis=-1, keepdims=True)
    vn = (vc * lax.rsqrt(var + LN_EPS) * lng_ref[...] + lnb_ref[...]).astype(BF16)
    gate = _silu(gd_ref[...].astype(F32))
    yield
    tri_r = lax.broadcasted_iota(jnp.int32, (SG_CHUNK, SG_CHUNK), 0)
    tri_c = lax.broadcasted_iota(jnp.int32, (SG_CHUNK, SG_CHUNK), 1)
    bst = bst_ref[...]
    for g in range(N_HEADS):
        w = jnp.where(tri_c <= tri_r, ws_ref[g], 0.0).astype(BF16)
        bias = bst[:, g:g + 1]
        lo = g * HEAD_W
        for c in range(ts // SG_CHUNK):
            r0 = c * SG_CHUNK
            sv = jnp.dot(w, vn[r0:r0 + SG_CHUNK, lo:lo + HEAD_W],
                         preferred_element_type=F32) + bias
            yd = u[r0:r0 + SG_CHUNK, lo:lo + HEAD_W] * sv * gate[r0:r0 + SG_CHUNK, lo:lo + HEAD_W]
            yd_ref[r0:r0 + SG_CHUNK, lo:lo + HEAD_W] = yd.astype(yd_ref.dtype)
        if g % 2 == 1:
            yield


def _hgrn_constants():
    c = HG_CHUNK
    t = np.arange(c)[:, None]
    s = np.arange(c)[None, :]
    mats = [(s <= t)]
    for lv in HG_LEVELS[1:]:
        mid = (t // lv) * lv + lv // 2
        qside = t >= mid
        mats.append(np.where(qside, (s >= mid) & (s <= t), (s > t) & (s < mid)))
    rng = np.concatenate(mats, axis=0).astype(np.float32)
    rng = np.concatenate([rng, rng], axis=1)
    masks = np.stack([t == s] + [(t // lv) == (s // lv) for lv in HG_LEVELS]).astype(np.float32)
    masks = np.concatenate([masks, masks], axis=2)
    return jnp.asarray(rng, BF16), jnp.asarray(masks, F32)


def _hgrn_body(q_ref, f_ref, i_ref, g_ref, lb_ref, ng_ref, rng_ref, msk_ref,
               y_ref, st_ref, *, nsub):
    c = HG_CHUNK
    fz = f_ref[...].astype(F32)
    lb = lb_ref[...]
    e = jnp.exp(-jnp.abs(fz))
    r = 1.0 / (1.0 + e)
    pos = fz >= 0.0
    sig = jnp.where(pos, r, e * r)
    nsig = jnp.where(pos, e * r, r)
    f_all = jnp.maximum(lb + (1.0 - lb) * sig, F_FLOOR)
    lg = jnp.log(f_all)
    kk_all = (1.0 - lb) * nsig

    g1 = lg.astype(BF16)
    g2 = (lg - g1.astype(F32)).astype(BF16)
    rng = rng_ref[...]
    exs = [jnp.dot(rng, jnp.concatenate([g1[u * c:(u + 1) * c], g2[u * c:(u + 1) * c]], axis=0),
                   preferred_element_type=F32)
           for u in range(nsub)]
    yield

    rows = lax.broadcasted_iota(jnp.int32, (c, HEAD_W), 0)
    zblk = jnp.zeros((c, HEAD_W), BF16)
    nt = (((1,), (1,)), ((), ()))

    def pair_rows(x0, x1):
        return jnp.concatenate([jnp.concatenate([x0, zblk], axis=1),
                                jnp.concatenate([zblk, x1], axis=1)], axis=0)

    for hp in range(N_HEADS // 2):
        los = (2 * hp * HEAD_W, (2 * hp + 1) * HEAD_W)
        sts = [st_ref[2 * hp], st_ref[2 * hp + 1]]
        for u in range(nsub):
            ex = exs[u]
            r0 = u * c
            qs, kks, vbs, outs = [], [], [], []
            for n, lo in enumerate(los):
                q = q_ref[r0:r0 + c, lo:lo + HEAD_W].astype(F32)
                vb = i_ref[r0:r0 + c, lo:lo + HEAD_W]
                kk = kk_all[r0:r0 + c, lo:lo + HEAD_W]
                bcum = ex[0:c, lo:lo + HEAD_W]
                q_in = (q * jnp.exp(bcum)).astype(BF16)
                outs.append(lax.dot_general(q_in, sts[n].astype(BF16), nt,
                                            preferred_element_type=F32))
                k_out = (kk * jnp.exp(bcum[c - 1:c, :] - bcum)).astype(BF16)
                e_last = jnp.exp(bcum[c - 1:c, :])
                sts[n] = e_last * sts[n] + jnp.dot(vb.astype(F32).T.astype(BF16), k_out,
                                                   preferred_element_type=F32)
                qs.append(q)
                kks.append(kk)
                vbs.append(vb)
            yield

            a = msk_ref[0] * lax.dot_general(
                jnp.concatenate([qs[0].astype(BF16), qs[1].astype(BF16)], axis=1),
                pair_rows(kks[0].astype(BF16), kks[1].astype(BF16)), nt,
                preferred_element_type=F32)
            for li, lv in enumerate(HG_LEVELS):
                qside = (rows & (lv - 1)) >= (lv // 2)
                qls, kls = [], []
                for n, lo in enumerate(los):
                    if lv == 2:
                        eq, ek = f_all[r0:r0 + c, lo:lo + HEAD_W], None
                    else:
                        eq = ek = jnp.exp(ex[li * c:(li + 1) * c, lo:lo + HEAD_W])
                    qls.append(jnp.where(qside, qs[n] * eq, 0.0).astype(BF16))
                    kls.append(jnp.where(qside, 0.0, kks[n] if ek is None else kks[n] * ek)
                               .astype(BF16))
                al = lax.dot_general(jnp.concatenate(qls, axis=1), pair_rows(kls[0], kls[1]), nt,
                                     preferred_element_type=F32)
                a = a + msk_ref[1 + li] * al
                if li % 2 == 0:
                    yield
            o2 = jnp.dot(a.astype(BF16), pair_rows(vbs[0], vbs[1]),
                         preferred_element_type=F32)

            for n, lo in enumerate(los):
                o = outs[n] + o2[:, n * HEAD_W:(n + 1) * HEAD_W]
                ms = jnp.mean(o * o, axis=-1, keepdims=True)
                on = o * lax.rsqrt(ms + RMS_EPS) * ng_ref[:, lo:lo + HEAD_W]
                y = on * _silu(g_ref[r0:r0 + c, lo:lo + HEAD_W].astype(F32))
                y_ref[r0:r0 + c, lo:lo + HEAD_W] = y.astype(y_ref.dtype)
            yield
        st_ref[2 * hp] = sts[0]
        st_ref[2 * hp + 1] = sts[1]


ATTN_ROWS = 64


def _diff_attn_kernel(lam_ref, q_ref, k_ref, v_ref, g_ref, ng_ref, wo_ref, wpg_ref, wpe_ref,
                      o_ref, wo_o, wpg_o, wpe_o,
                      qq_ref, pa_ref, pb_ref, pt_ref, pu_ref,
                      ma_ref, la_ref, aa_ref, mb_ref, lb_ref, ab_ref,
                      mt_ref, lt_ref, at_ref, mu_ref, lu_ref, au_ref,
                      acc_ref, *, tk, nq, out_scale):
    wo_o[...] = wo_ref[...].astype(BF16)
    wpg_o[...] = wpg_ref[...].astype(BF16)
    wpe_o[...] = wpe_ref[...].astype(BF16)

    lam = lam_ref[0, 0]
    scale = DA_DQK ** -0.5
    tq = 2 * tk
    nrow = 2 * tq
    st_a = (ma_ref, la_ref, aa_ref)
    st_b = (mb_ref, lb_ref, ab_ref)
    st_t = (mt_ref, lt_ref, at_ref)
    st_u = (mu_ref, lu_ref, au_ref)

    hq = tk // 2

    def scores(j, ra=0, rb=nrow, nkeys=tk):
        kb = k_ref[pl.ds(pl.multiple_of(j * tk, tk), nkeys), :]
        return lax.dot_general(qq_ref[ra:rb, :], kb, (((1,), (1,)), ((), ())),
                               preferred_element_type=F32)

    def accumulate(j, p_ref, st, ra=0, rb=nrow, nkeys=tk):
        vb = v_ref[pl.ds(pl.multiple_of(j * tk, tk), nkeys), :]
        acc_ref[ra:rb, :] = (st[2][ra:rb, :] * acc_ref[ra:rb, :]
                             + jnp.dot(p_ref[ra:rb, 0:nkeys], vb, preferred_element_type=F32))

    def softmax(s_all, p_ref, st_in, st_out, diag=None, ra=0, rb=nrow):
        masked = diag is not None
        for r0 in range(ra, rb, ATTN_ROWS):
            rs = slice(r0, r0 + ATTN_ROWS)
            row_lo = (r0 // tk) * hq + r0 % hq
            key_lo = diag * tk if masked else 0
            tiles = []
            for cb in range(s_all.shape[1] // HEAD_W):
                k0 = key_lo + cb * HEAD_W
                if masked and k0 > row_lo + ATTN_ROWS - 1:
                    tiles.append(None)
                    continue
                s = s_all[r0 - ra:r0 - ra + ATTN_ROWS, cb * HEAD_W:(cb + 1) * HEAD_W]
                if masked and k0 + HEAD_W - 1 > row_lo:
                    r = row_lo + lax.broadcasted_iota(jnp.int32, s.shape, 0)
                    cidx = k0 + lax.broadcasted_iota(jnp.int32, s.shape, 1)
                    s = jnp.where(cidx <= r, s, MASK_VALUE)
                tiles.append(s)
            live = [s for s in tiles if s is not None]
            mx = functools.reduce(jnp.maximum, live)
            m_blk = jnp.broadcast_to(jnp.max(mx, axis=-1, keepdims=True), mx.shape)
            m_old = st_in[0][rs, :]
            m_new = jnp.maximum(m_old, m_blk)
            alpha = jnp.exp2(m_old - m_new)
            psum = None
            for cb, s in enumerate(tiles):
                if s is None:
                    p_ref[rs, cb * HEAD_W:(cb + 1) * HEAD_W] = jnp.zeros((ATTN_ROWS, HEAD_W), BF16)
                    continue
                p = jnp.exp2(s - m_new)
                psum = p if psum is None else psum + p
                p_ref[rs, cb * HEAD_W:(cb + 1) * HEAD_W] = p.astype(BF16)
            st_out[0][rs, :] = m_new
            st_out[1][rs, :] = alpha * st_in[1][rs, :] + psum
            st_out[2][rs, :] = alpha

    def step(j, p_in, p_out, st_in, st_out):
        s = scores(j)
        accumulate(jnp.maximum(j - 1, 0), p_in, st_in)
        softmax(s, p_out, st_in, st_out)

    def pair(jj, carry):
        step(2 * jj, pb_ref, pa_ref, st_b, st_a)
        step(2 * jj + 1, pa_ref, pb_ref, st_a, st_b)
        return carry

    def quad(qq, carry):
        pair(2 * qq, carry)
        return pair(2 * qq + 1, carry)

    def q_tile(i, carry):
        for c in range(nrow // tk):
            rows = pl.ds(pl.multiple_of(i * tq + c * hq, hq), hq)
            q = (q_ref[rows, :].astype(F32) * (scale * math.log2(math.e))).astype(BF16)
            lane = lax.broadcasted_iota(jnp.int32, q.shape, 1)
            zero = jnp.zeros_like(q)
            qq_ref[c * tk:c * tk + hq, :] = jnp.where(lane < DA_DQK, q, zero)
            qq_ref[c * tk + hq:(c + 1) * tk, :] = jnp.where(lane >= DA_DQK, q, zero)
        acc_ref[...] = jnp.zeros_like(acc_ref)
        mb_ref[...] = jnp.full_like(mb_ref, MASK_VALUE)
        lb_ref[...] = jnp.zeros_like(lb_ref)
        ab_ref[...] = jnp.ones_like(ab_ref)
        pb_ref[...] = jnp.zeros_like(pb_ref)

        lax.fori_loop(0, i // 2, quad, 0)
        lax.fori_loop(2 * (i // 2), i, pair, 0)

        j0 = 2 * i
        s_q0 = scores(j0, 0, tk, hq)
        s_rest = scores(j0, tk, nrow)
        accumulate(jnp.maximum(j0 - 1, 0), pb_ref, st_b)
        softmax(s_q0, pt_ref, st_b, st_t, diag=0, ra=0, rb=tk)
        softmax(s_rest, pt_ref, st_b, st_t, diag=0, ra=tk, rb=nrow)
        s_q2 = scores(j0 + 1, 2 * tk, 3 * tk, hq)
        s_q3 = scores(j0 + 1, 3 * tk, nrow)
        accumulate(j0, pt_ref, st_t, 0, tk, hq)
        accumulate(j0, pt_ref, st_t, tk, nrow)
        softmax(s_q2, pu_ref, st_t, st_u, diag=1, ra=2 * tk, rb=3 * tk)
        softmax(s_q3, pu_ref, st_t, st_u, diag=1, ra=3 * tk, rb=nrow)
        accumulate(j0 + 1, pu_ref, st_u, 2 * tk, 3 * tk, hq)
        accumulate(j0 + 1, pu_ref, st_u, 3 * tk, nrow)

        for c in range(nrow // tk):
            st = st_t if c < 2 else st_u
            r0 = c * tk
            l = jnp.sum(st[1][r0:r0 + tk, :], axis=-1, keepdims=True)
            acc = acc_ref[r0:r0 + tk, :] / l
            o = acc[0:hq] - lam * acc[hq:tk]
            ms = jnp.mean(o * o, axis=-1, keepdims=True)
            on = o * lax.rsqrt(ms + RMS_EPS) * ng_ref[...] * out_scale
            rows = pl.ds(pl.multiple_of(i * tq + c * hq, hq), hq)
            o_ref[rows, :] = (on * _silu(g_ref[rows, :].astype(F32))).astype(o_ref.dtype)
        return carry

    lax.fori_loop(0, nq, q_tile, 0)


def _diff_attn(h, lam, lam_init, norm_g, wo_all, wpg_all, wpe_all, layer, batch, seq, tk=512):
    t = h.shape[0]
    tq = 2 * tk
    nq = seq // tq
    per = GROUP_W // HEAD_W
    ng2 = norm_g.reshape(1, GROUP_W).astype(F32)
    lam2 = jnp.reshape(lam, (1, 1)).astype(F32)
    kern = functools.partial(_diff_attn_kernel, tk=tk, nq=nq, out_scale=1.0 - lam_init)
    cast_cols = wo_all.shape[2] // (batch * N_HEADS)

    def head_cols(blk):
        return pl.BlockSpec((seq, HEAD_W), lambda b, hh, blk=blk: (b, blk * per + hh))

    def cast_in(a):
        return pl.BlockSpec((None, a.shape[1], cast_cols),
                            lambda b, hh: (layer, 0, b * N_HEADS + hh))

    def cast_out(a):
        return pl.BlockSpec((a.shape[1], cast_cols), lambda b, hh: (0, b * N_HEADS + hh))

    weights = (wo_all, wpg_all, wpe_all)
    return pl.pallas_call(
        kern,
        grid=(batch, N_HEADS),
        in_specs=[pl.BlockSpec(memory_space=pltpu.SMEM),
                  head_cols(C_Q), head_cols(C_K), head_cols(C_V), head_cols(G_C),
                  pl.BlockSpec((1, HEAD_W), lambda b, hh: (0, hh))]
                 + [cast_in(a) for a in weights],
        out_specs=[pl.BlockSpec((seq, HEAD_W), lambda b, hh: (b, hh))]
                  + [cast_out(a) for a in weights],
        out_shape=[jax.ShapeDtypeStruct((t, GROUP_W), BF16)]
                  + [jax.ShapeDtypeStruct(a.shape[1:], BF16) for a in weights],
        scratch_shapes=[pltpu.VMEM((2 * tq, HEAD_W), BF16),
                        *([pltpu.VMEM((2 * tq, tk), BF16)] * 4),
                        *([pltpu.VMEM((2 * tq, HEAD_W), F32)] * 13)],
        compiler_params=pltpu.CompilerParams(
            dimension_semantics=("arbitrary", "arbitrary"),
            vmem_limit_bytes=VMEM_LIMIT),
        name="diff_attn",
    )(lam2, h, h, h, h, ng2, *weights)


TAIL_ROWS = 256


def _tail_kernel(ab_ref, ac_ref, ax_ref, ga_ref, du_ref, dv_ref, gd_ref,
                 bq_ref, bf_ref, bi_ref, gb_ref, yc_ref, x_ref, p_ref,
                 cw_ref, sglng_ref, sglnb_ref, ws_ref, bst_ref,
                 hlb_ref, hng_ref, rng_ref, msk_ref,
                 wo_hbm, wpg_hbm, wpe_hbm, lng_ref, lnb_ref,
                 xo_ref, xb_ref,
                 ya_ref, yb_ref, yd_ref, carry_ref, st_ref,
                 wo_ref, wpg_ref, wpe_ref, wsem, *, seq, alpha):
    n = pl.program_id(0)

    @pl.when((n * TAIL_ROWS) % seq == 0)
    def _():
        carry_ref[...] = jnp.zeros_like(carry_ref)
        st_ref[...] = jnp.zeros_like(st_ref)

    wslot = n % 2
    rslot = 1 - wslot

    def mixers():
        yield from _conv_sgu_body(ab_ref, ac_ref, ax_ref, ga_ref, du_ref, dv_ref, gd_ref,
                                  cw_ref, sglng_ref, sglnb_ref, ws_ref, bst_ref,
                                  ya_ref.at[wslot], yd_ref.at[wslot], carry_ref, ts=TAIL_ROWS)
        yield from _hgrn_body(bq_ref, bf_ref, bi_ref, gb_ref, hlb_ref, hng_ref, rng_ref, msk_ref,
                              yb_ref.at[wslot], st_ref, nsub=TAIL_ROWS // HG_CHUNK)

    def projections():
        d = x_ref.shape[-1]
        blocks = [slice(n0, n0 + GROUP_W) for n0 in range(0, d, GROUP_W)]
        ys = ((yc_ref, 2), (ya_ref.at[rslot], 0), (yd_ref.at[rslot], 3), (yb_ref.at[rslot], 1))
        accs = []
        for cols in blocks:
            acc = alpha * x_ref[:, cols]
            for y, g in ys:
                acc = acc + jnp.dot(y[...], wo_ref[g * GROUP_W:(g + 1) * GROUP_W, cols],
                                    preferred_element_type=F32)
                yield
            accs.append(acc)
        pe = jnp.dot(p_ref[...].astype(BF16), wpe_ref[...], preferred_element_type=F32)
        mu = sum(jnp.sum(a, axis=-1, keepdims=True) for a in accs) * (1.0 / d)
        xcs = [a - mu for a in accs]
        var = sum(jnp.sum(xc * xc, axis=-1, keepdims=True) for xc in xcs) * (1.0 / d)
        rstd = lax.rsqrt(var + LN_EPS)
        xns = [xc * rstd * lng_ref[:, cols] + lnb_ref[:, cols] for xc, cols in zip(xcs, blocks)]
        xnbs = [xn.astype(BF16) for xn in xns]
        yield
        for xn, cols in zip(xns, blocks):
            z = None
            for xnb, rows in zip(xnbs, blocks):
                part = jnp.dot(xnb, wpg_ref[rows, cols], preferred_element_type=F32)
                z = part if z is None else z + part
                yield
            out = xn + pe[:, cols] * _sigmoid(z)
            xo_ref[:, cols] = out
            xb_ref[:, cols] = out.astype(BF16)

    last = pl.num_programs(0) - 1

    def weight_copies():
        pairs = ((wo_hbm, wo_ref), (wpg_hbm, wpg_ref), (wpe_hbm, wpe_ref))
        return [pltpu.make_async_copy(src, dst, wsem.at[k]) for k, (src, dst) in enumerate(pairs)]

    @pl.when(n == 0)
    def _():
        for cp in weight_copies():
            cp.start()
        for _ in mixers():
            pass

    @pl.when(n == 1)
    def _():
        for cp in weight_copies():
            cp.wait()

    @pl.when(n == last)
    def _():
        for _ in projections():
            pass

    @pl.when(jnp.logical_and(n > 0, n < last))
    def _():
        major, minor = mixers(), projections()
        live_major = live_minor = True
        while live_major or live_minor:
            live_major = live_major and next(major, False) is not False
            live_minor = live_minor and next(minor, False) is not False


def _tail_block(h, yc, x, p_all, small, wo_all, wpg_all, wpe_all, lng_all, lnb_all,
                layer, seq, alpha):
    t, d = x.shape
    tm = TAIL_ROWS
    pdim = p_all.shape[1]
    steps = t // tm

    def hblk(blk):
        return pl.BlockSpec((tm, GROUP_W), lambda i, blk=blk: (jnp.minimum(i, steps - 1), blk))

    def rows(w):
        return pl.BlockSpec((tm, w), lambda i: (jnp.maximum(i - 1, 0), 0))

    def full(a):
        nd = a.ndim
        return pl.BlockSpec(a.shape, lambda i, nd=nd: (0,) * nd)

    def resident(a):
        if a.ndim == 2:
            return pl.BlockSpec(a.shape, lambda i: (0, 0), pipeline_mode=pl.Buffered(1))
        return pl.BlockSpec((None,) + a.shape[1:], lambda i: (layer, 0, 0),
                            pipeline_mode=pl.Buffered(1))

    h_blocks = (A_B, A_C, A_X, G_A, D_U, D_V, G_D, B_Q, B_F, B_I, G_B)
    kern = functools.partial(_tail_kernel, seq=seq, alpha=alpha)
    return pl.pallas_call(
        kern,
        grid=(steps + 1,),
        in_specs=[hblk(b) for b in h_blocks]
                 + [rows(GROUP_W), rows(d),
                    pl.BlockSpec((tm, pdim),
                                 lambda i: (layer * steps + jnp.maximum(i - 1, 0), 0))]
                 + [full(a) for a in small]
                 + [pl.BlockSpec(memory_space=pl.ANY)] * 3
                 + [resident(lng_all), resident(lnb_all)],
        out_specs=[rows(d), rows(d)],
        out_shape=[jax.ShapeDtypeStruct((t, d), F32), jax.ShapeDtypeStruct((t, d), BF16)],
        scratch_shapes=[pltpu.VMEM((2, tm, GROUP_W), BF16), pltpu.VMEM((2, tm, GROUP_W), BF16),
                        pltpu.VMEM((2, tm, GROUP_W), BF16),
                        pltpu.VMEM((8, GROUP_W), F32),
                        pltpu.VMEM((N_HEADS, HEAD_W, HEAD_W), F32),
                        pltpu.VMEM(wo_all.shape, BF16), pltpu.VMEM(wpg_all.shape, BF16),
                        pltpu.VMEM(wpe_all.shape, BF16), pltpu.SemaphoreType.DMA((3,))],
        compiler_params=pltpu.CompilerParams(
            dimension_semantics=("arbitrary",), vmem_limit_bytes=VMEM_LIMIT),
        name="mix_out",
    )(*([h] * len(h_blocks)), yc, x, p_all, *small,
      wo_all, wpg_all, wpe_all, lng_all, lnb_all)


def kernel(x, p, w_in, conv_w, hgrn_lb, hgrn_norm_g, diff_lambda, diff_norm_g,
           sg_ln_g, sg_ln_b, sg_w, sg_b, w_out, ln_g, ln_b, w_pe, w_pg):
    batch, seq, d_model = x.shape
    depth = w_in.shape[0]
    t = batch * seq
    alpha = (2 * depth) ** 0.25

    lb_sm = jax.nn.softmax(hgrn_lb.astype(F32), axis=0)
    lower_bounds = jnp.cumsum(lb_sm, axis=0) - lb_sm[0]

    xf = x.reshape(t, d_model)
    xb = xf
    p_all = p.reshape(depth * t, p.shape[-1])
    lng_all = ln_g.reshape(depth, 1, d_model).astype(F32)
    lnb_all = ln_b.reshape(depth, 1, d_model).astype(F32)
    rng, masks = _hgrn_constants()
    for i in range(depth):
        lam_init = 0.8 - 0.6 * math.exp(-0.3 * i)
        dl = diff_lambda[i].astype(F32)
        lam = (jnp.exp(jnp.sum(dl[0] * dl[1])) - jnp.exp(jnp.sum(dl[2] * dl[3])) + lam_init)

        h = _in_proj(xb, w_in, i)
        yc, wo_b, wpg_b, wpe_b = _diff_attn(h, lam, lam_init, diff_norm_g[i],
                                            w_out, w_pg, w_pe, i, batch, seq)
        small = (conv_w[i].astype(F32),
                 sg_ln_g[i].reshape(1, GROUP_W).astype(F32), sg_ln_b[i].reshape(1, GROUP_W).astype(F32),
                 sg_w[i].astype(F32), sg_b[i].T.astype(F32),
                 lower_bounds[i].reshape(1, GROUP_W), hgrn_norm_g[i].reshape(1, GROUP_W).astype(F32),
                 rng, masks)
        xf, xb = _tail_block(h, yc, xf, p_all, small, wo_b, wpg_b, wpe_b,
                             lng_all, lnb_all, i, seq, alpha)
    return xf.reshape(batch, seq, d_model)
```

```python
import functools
import math

import numpy as np
import jax
import jax.numpy as jnp
from jax import lax
from jax.experimental import pallas as pl
from jax.experimental.pallas import tpu as pltpu

F32 = jnp.float32
BF16 = jnp.bfloat16

GROUP_W = 512
HEAD_W = 128
N_HEADS = GROUP_W // HEAD_W
DA_DQK = 64
F_FLOOR = 1e-30
MASK_VALUE = -1e30
LN_EPS = 1e-5
RMS_EPS = 1e-6

A_B, A_C, A_X, B_Q, B_F, B_I, C_Q, C_K, C_V, D_U, D_V, G_A, G_B, G_C, G_D = range(15)

HG_CHUNK = 128
HG_LEVELS = (2, 4, 8, 16, 32, 64, 128)
SG_CHUNK = 128
V7X_VMEM_BYTES = 64 * 1024 * 1024
VMEM_LIMIT = V7X_VMEM_BYTES - 8 * 1024 * 1024


def _sigmoid(x):
    return 1.0 / (1.0 + jnp.exp(-x))


def _silu(x):
    return x * _sigmoid(x)


def _gelu_tanh(x):
    c = math.sqrt(2.0 / math.pi)
    return 0.5 * x * (1.0 + jnp.tanh(c * (x + 0.044715 * (x * x * x))))


def _matmul_kernel(x_ref, w_ref, o_ref):
    o_ref[...] = jnp.dot(x_ref[...].astype(BF16), w_ref[...].astype(BF16),
                         preferred_element_type=F32).astype(o_ref.dtype)


def _matmul_f32x_kernel(x_hbm, w_ref, o_ref, xs_ref, xb_ref, sem, *, tm):
    i = pl.program_id(0)
    j = pl.program_id(1)

    def row_copy(r):
        return pltpu.make_async_copy(x_hbm.at[pl.ds(r * tm, tm), :], xs_ref, sem)

    @pl.when(jnp.logical_and(i == 0, j == 0))
    def _():
        row_copy(0).start()

    @pl.when(j == 0)
    def _():
        row_copy(i).wait()
        xb_ref[...] = xs_ref[...].astype(BF16)

        @pl.when(i + 1 < pl.num_programs(0))
        def _():
            row_copy(i + 1).start()

    o_ref[...] = jnp.dot(xb_ref[...], w_ref[...].astype(BF16),
                         preferred_element_type=F32).astype(o_ref.dtype)


def _in_proj(xb, w_all, layer, tm=2048, tn=768):
    t, k = xb.shape
    n = w_all.shape[2]
    if xb.dtype == jnp.float32:
        return pl.pallas_call(
            functools.partial(_matmul_f32x_kernel, tm=tm),
            grid=(t // tm, n // tn),
            in_specs=[pl.BlockSpec(memory_space=pl.ANY),
                      pl.BlockSpec((None, k, tn), lambda i, j: (layer, 0, j))],
            out_specs=pl.BlockSpec((tm, tn), lambda i, j: (i, j)),
            out_shape=jax.ShapeDtypeStruct((t, n), BF16),
            scratch_shapes=[pltpu.VMEM((tm, k), F32), pltpu.VMEM((tm, k), BF16),
                            pltpu.SemaphoreType.DMA(())],
            compiler_params=pltpu.CompilerParams(
                dimension_semantics=("arbitrary", "arbitrary"),
                vmem_limit_bytes=VMEM_LIMIT),
            name="in_proj",
        )(xb, w_all)
    tm, tn = 2 * tm, 512
    return pl.pallas_call(
        _matmul_kernel,
        grid=(t // tm, n // tn),
        in_specs=[pl.BlockSpec((tm, k), lambda i, j: (i, 0)),
                  pl.BlockSpec((None, k, tn), lambda i, j: (layer, 0, j))],
        out_specs=pl.BlockSpec((tm, tn), lambda i, j: (i, j)),
        out_shape=jax.ShapeDtypeStruct((t, n), BF16),
        compiler_params=pltpu.CompilerParams(
            dimension_semantics=("arbitrary", "arbitrary"),
            vmem_limit_bytes=VMEM_LIMIT),
        name="in_proj",
    )(xb, w_all)


def _conv_sgu_body(ab_ref, ac_ref, ax_ref, ga_ref, du_ref, dv_ref, gd_ref,
                   cw_ref, lng_ref, lnb_ref, ws_ref, bst_ref,
                   ya_ref, yd_ref, carry_ref, *, ts):
    z = ac_ref[...].astype(F32) * ax_ref[...].astype(F32)
    rows = lax.broadcasted_iota(jnp.int32, z.shape, 0)
    prev1 = carry_ref[7:8, :]
    prev2 = carry_ref[6:7, :]
    z1 = jnp.where(rows == 0, prev1, pltpu.roll(z, 1, 0))
    z2 = jnp.where(rows == 0, prev2, jnp.where(rows == 1, prev1, pltpu.roll(z, 2, 0)))
    cw = cw_ref[...]
    y = cw[0:1, :] * z2 + cw[1:2, :] * z1 + cw[2:3, :] * z
    ya = ab_ref[...].astype(F32) * y * _silu(ga_ref[...].astype(F32))
    ya_ref[...] = ya.astype(ya_ref.dtype)
    carry_ref[...] = z[ts - 8:ts, :]
    yield

    u = _gelu_tanh(du_ref[...].astype(F32))
    v = _gelu_tanh(dv_ref[...].astype(F32))
    mu = jnp.mean(v, axis=-1, keepdims=True)
    vc = v - mu
    var = jnp.mean(vc * vc, axis=-1, keepdims=True)
    vn = (vc * lax.rsqrt(var + LN_EPS) * lng_ref[...] + lnb_ref[...]).astype(BF16)
    gate = _silu(gd_ref[...].astype(F32))
    yield
    tri_r = lax.broadcasted_iota(jnp.int32, (SG_CHUNK, SG_CHUNK), 0)
    tri_c = lax.broadcasted_iota(jnp.int32, (SG_CHUNK, SG_CHUNK), 1)
    bst = bst_ref[...]
    for g in range(N_HEADS):
        w = jnp.where(tri_c <= tri_r, ws_ref[g], 0.0).astype(BF16)
        bias = bst[:, g:g + 1]
        lo = g * HEAD_W
        for c in range(ts // SG_CHUNK):
            r0 = c * SG_CHUNK
            sv = jnp.dot(w, vn[r0:r0 + SG_CHUNK, lo:lo + HEAD_W],
                         preferred_element_type=F32) + bias
            yd = u[r0:r0 + SG_CHUNK, lo:lo + HEAD_W] * sv * gate[r0:r0 + SG_CHUNK, lo:lo + HEAD_W]
            yd_ref[r0:r0 + SG_CHUNK, lo:lo + HEAD_W] = yd.astype(yd_ref.dtype)
        if g % 2 == 1:
            yield


def _hgrn_constants():
    c = HG_CHUNK
    t = np.arange(c)[:, None]
    s = np.arange(c)[None, :]
    mats = [(s <= t)]
    for lv in HG_LEVELS[1:]:
        mid = (t // lv) * lv + lv // 2
        qside = t >= mid
        mats.append(np.where(qside, (s >= mid) & (s <= t), (s > t) & (s < mid)))
    rng = np.concatenate(mats, axis=0).astype(np.float32)
    rng = np.concatenate([rng, rng], axis=1)
    masks = np.stack([t == s] + [(t // lv) == (s // lv) for lv in HG_LEVELS]).astype(np.float32)
    masks = np.concatenate([masks, masks], axis=2)
    return jnp.asarray(rng, BF16), jnp.asarray(masks, F32)


def _hgrn_body(q_ref, f_ref, i_ref, g_ref, lb_ref, ng_ref, rng_ref, msk_ref,
               y_ref, st_ref, *, nsub):
    c = HG_CHUNK
    fz = f_ref[...].astype(F32)
    lb = lb_ref[...]
    e = jnp.exp(-jnp.abs(fz))
    r = 1.0 / (1.0 + e)
    pos = fz >= 0.0
    sig = jnp.where(pos, r, e * r)
    nsig = jnp.where(pos, e * r, r)
    f_all = jnp.maximum(lb + (1.0 - lb) * sig, F_FLOOR)
    lg = jnp.log(f_all)
    kk_all = (1.0 - lb) * nsig

    g1 = lg.astype(BF16)
    g2 = (lg - g1.astype(F32)).astype(BF16)
    rng = rng_ref[...]
    exs = [jnp.dot(rng, jnp.concatenate([g1[u * c:(u + 1) * c], g2[u * c:(u + 1) * c]], axis=0),
                   preferred_element_type=F32)
           for u in range(nsub)]
    yield

    rows = lax.broadcasted_iota(jnp.int32, (c, HEAD_W), 0)
    zblk = jnp.zeros((c, HEAD_W), BF16)
    nt = (((1,), (1,)), ((), ()))

    def pair_rows(x0, x1):
        return jnp.concatenate([jnp.concatenate([x0, zblk], axis=1),
                                jnp.concatenate([zblk, x1], axis=1)], axis=0)

    for hp in range(N_HEADS // 2):
        los = (2 * hp * HEAD_W, (2 * hp + 1) * HEAD_W)
        sts = [st_ref[2 * hp], st_ref[2 * hp + 1]]
        for u in range(nsub):
            ex = exs[u]
            r0 = u * c
            qs, kks, vbs, outs = [], [], [], []
            for n, lo in enumerate(los):
                q = q_ref[r0:r0 + c, lo:lo + HEAD_W].astype(F32)
                vb = i_ref[r0:r0 + c, lo:lo + HEAD_W]
                kk = kk_all[r0:r0 + c, lo:lo + HEAD_W]
                bcum = ex[0:c, lo:lo + HEAD_W]
                q_in = (q * jnp.exp(bcum)).astype(BF16)
                outs.append(lax.dot_general(q_in, sts[n].astype(BF16), nt,
                                            preferred_element_type=F32))
                k_out = (kk * jnp.exp(bcum[c - 1:c, :] - bcum)).astype(BF16)
                e_last = jnp.exp(bcum[c - 1:c, :])
                sts[n] = e_last * sts[n] + jnp.dot(vb.astype(F32).T.astype(BF16), k_out,
                                                   preferred_element_type=F32)
                qs.append(q)
                kks.append(kk)
                vbs.append(vb)
            yield

            a = msk_ref[0] * lax.dot_general(
                jnp.concatenate([qs[0].astype(BF16), qs[1].astype(BF16)], axis=1),
                pair_rows(kks[0].astype(BF16), kks[1].astype(BF16)), nt,
                preferred_element_type=F32)
            for li, lv in enumerate(HG_LEVELS):
                qside = (rows & (lv - 1)) >= (lv // 2)
                qls, kls = [], []
                for n, lo in enumerate(los):
                    if lv == 2:
                        eq, ek = f_all[r0:r0 + c, lo:lo + HEAD_W], None
                    else:
                        eq = ek = jnp.exp(ex[li * c:(li + 1) * c, lo:lo + HEAD_W])
                    qls.append(jnp.where(qside, qs[n] * eq, 0.0).astype(BF16))
                    kls.append(jnp.where(qside, 0.0, kks[n] if ek is None else kks[n] * ek)
                               .astype(BF16))
                al = lax.dot_general(jnp.concatenate(qls, axis=1), pair_rows(kls[0], kls[1]), nt,
                                     preferred_element_type=F32)
                a = a + msk_ref[1 + li] * al
                if li % 2 == 0:
                    yield
            o2 = jnp.dot(a.astype(BF16), pair_rows(vbs[0], vbs[1]),
                         preferred_element_type=F32)

            for n, lo in enumerate(los):
                o = outs[n] + o2[:, n * HEAD_W:(n + 1) * HEAD_W]
                ms = jnp.mean(o * o, axis=-1, keepdims=True)
                on = o * lax.rsqrt(ms + RMS_EPS) * ng_ref[:, lo:lo + HEAD_W]
                y = on * _silu(g_ref[r0:r0 + c, lo:lo + HEAD_W].astype(F32))
                y_ref[r0:r0 + c, lo:lo + HEAD_W] = y.astype(y_ref.dtype)
            yield
        st_ref[2 * hp] = sts[0]
        st_ref[2 * hp + 1] = sts[1]


ATTN_ROWS = 64


def _diff_attn_kernel(lam_ref, q_ref, k_ref, v_ref, g_ref, ng_ref, wo_ref, wpg_ref, wpe_ref,
                      o_ref, wo_o, wpg_o, wpe_o,
                      qq_ref, pa_ref, pb_ref, pt_ref, pu_ref,
                      ma_ref, la_ref, aa_ref, mb_ref, lb_ref, ab_ref,
                      mt_ref, lt_ref, at_ref, mu_ref, lu_ref, au_ref,
                      acc_ref, *, tk, nq, out_scale):
    wo_o[...] = wo_ref[...].astype(BF16)
    wpg_o[...] = wpg_ref[...].astype(BF16)
    wpe_o[...] = wpe_ref[...].astype(BF16)

    lam = lam_ref[0, 0]
    scale = DA_DQK ** -0.5
    tq = 2 * tk
    nrow = 2 * tq
    st_a = (ma_ref, la_ref, aa_ref)
    st_b = (mb_ref, lb_ref, ab_ref)
    st_t = (mt_ref, lt_ref, at_ref)
    st_u = (mu_ref, lu_ref, au_ref)

    hq = tk // 2

    def scores(j, ra=0, rb=nrow, nkeys=tk):
        kb = k_ref[pl.ds(pl.multiple_of(j * tk, tk), nkeys), :]
        return lax.dot_general(qq_ref[ra:rb, :], kb, (((1,), (1,)), ((), ())),
                               preferred_element_type=F32)

    def accumulate(j, p_ref, st, ra=0, rb=nrow, nkeys=tk):
        vb = v_ref[pl.ds(pl.multiple_of(j * tk, tk), nkeys), :]
        acc_ref[ra:rb, :] = (st[2][ra:rb, :] * acc_ref[ra:rb, :]
                             + jnp.dot(p_ref[ra:rb, 0:nkeys], vb, preferred_element_type=F32))

    def softmax(s_all, p_ref, st_in, st_out, diag=None, ra=0, rb=nrow):
        masked = diag is not None
        for r0 in range(ra, rb, ATTN_ROWS):
            rs = slice(r0, r0 + ATTN_ROWS)
            row_lo = (r0 // tk) * hq + r0 % hq
            key_lo = diag * tk if masked else 0
            tiles = []
            for cb in range(s_all.shape[1] // HEAD_W):
                k0 = key_lo + cb * HEAD_W
                if masked and k0 > row_lo + ATTN_ROWS - 1:
                    tiles.append(None)
                    continue
                s = s_all[r0 - ra:r0 - ra + ATTN_ROWS, cb * HEAD_W:(cb + 1) * HEAD_W]
                if masked and k0 + HEAD_W - 1 > row_lo:
                    r = row_lo + lax.broadcasted_iota(jnp.int32, s.shape, 0)
                    cidx = k0 + lax.broadcasted_iota(jnp.int32, s.shape, 1)
                    s = jnp.where(cidx <= r, s, MASK_VALUE)
                tiles.append(s)
            live = [s for s in tiles if s is not None]
            mx = functools.reduce(jnp.maximum, live)
            m_blk = jnp.broadcast_to(jnp.max(mx, axis=-1, keepdims=True), mx.shape)
            m_old = st_in[0][rs, :]
            m_new = jnp.maximum(m_old, m_blk)
            alpha = jnp.exp2(m_old - m_new)
            psum = None
            for cb, s in enumerate(tiles):
                if s is None:
                    p_ref[rs, cb * HEAD_W:(cb + 1) * HEAD_W] = jnp.zeros((ATTN_ROWS, HEAD_W), BF16)
                    continue
                p = jnp.exp2(s - m_new)
                psum = p if psum is None else psum + p
                p_ref[rs, cb * HEAD_W:(cb + 1) * HEAD_W] = p.astype(BF16)
            st_out[0][rs, :] = m_new
            st_out[1][rs, :] = alpha * st_in[1][rs, :] + psum
            st_out[2][rs, :] = alpha

    def step(j, p_in, p_out, st_in, st_out):
        s = scores(j)
        accumulate(jnp.maximum(j - 1, 0), p_in, st_in)
        softmax(s, p_out, st_in, st_out)

    def pair(jj, carry):
        step(2 * jj, pb_ref, pa_ref, st_b, st_a)
        step(2 * jj + 1, pa_ref, pb_ref, st_a, st_b)
        return carry

    def pairs_per_trip(count, first=0):
        def body(t, carry):
            for u in range(count):
                pair(first + count * t + u, carry)
            return carry
        return body

    def q_tile(i, carry):
        for c in range(nrow // tk):
            rows = pl.ds(pl.multiple_of(i * tq + c * hq, hq), hq)
            q = (q_ref[rows, :].astype(F32) * (scale * math.log2(math.e))).astype(BF16)
            lane = lax.broadcasted_iota(jnp.int32, q.shape, 1)
            zero = jnp.zeros_like(q)
            qq_ref[c * tk:c * tk + hq, :] = jnp.where(lane < DA_DQK, q, zero)
            qq_ref[c * tk + hq:(c + 1) * tk, :] = jnp.where(lane >= DA_DQK, q, zero)
        acc_ref[...] = jnp.zeros_like(acc_ref)
        mb_ref[...] = jnp.full_like(mb_ref, MASK_VALUE)
        lb_ref[...] = jnp.zeros_like(lb_ref)
        ab_ref[...] = jnp.ones_like(ab_ref)
        pb_ref[...] = jnp.zeros_like(pb_ref)

        n3 = i // 3
        rest = i - 3 * n3
        lax.fori_loop(0, n3, pairs_per_trip(3), 0)
        lax.fori_loop(0, rest // 2, pairs_per_trip(2, first=3 * n3), 0)
        lax.fori_loop(3 * n3 + 2 * (rest // 2), i, pair, 0)

        j0 = 2 * i
        s_q0 = scores(j0, 0, tk, hq)
        s_rest = scores(j0, tk, nrow)
        accumulate(jnp.maximum(j0 - 1, 0), pb_ref, st_b)
        softmax(s_q0, pt_ref, st_b, st_t, diag=0, ra=0, rb=tk)
        softmax(s_rest, pt_ref, st_b, st_t, diag=0, ra=tk, rb=nrow)
        s_q2 = scores(j0 + 1, 2 * tk, 3 * tk, hq)
        s_q3 = scores(j0 + 1, 3 * tk, nrow)
        accumulate(j0, pt_ref, st_t, 0, tk, hq)
        accumulate(j0, pt_ref, st_t, tk, nrow)
        softmax(s_q2, pu_ref, st_t, st_u, diag=1, ra=2 * tk, rb=3 * tk)
        softmax(s_q3, pu_ref, st_t, st_u, diag=1, ra=3 * tk, rb=nrow)
        accumulate(j0 + 1, pu_ref, st_u, 2 * tk, 3 * tk, hq)
        accumulate(j0 + 1, pu_ref, st_u, 3 * tk, nrow)

        for c in range(nrow // tk):
            st = st_t if c < 2 else st_u
            r0 = c * tk
            l = jnp.sum(st[1][r0:r0 + tk, :], axis=-1, keepdims=True)
            acc = acc_ref[r0:r0 + tk, :] / l
            o = acc[0:hq] - lam * acc[hq:tk]
            ms = jnp.mean(o * o, axis=-1, keepdims=True)
            on = o * lax.rsqrt(ms + RMS_EPS) * ng_ref[...] * out_scale
            rows = pl.ds(pl.multiple_of(i * tq + c * hq, hq), hq)
            o_ref[rows, :] = (on * _silu(g_ref[rows, :].astype(F32))).astype(o_ref.dtype)
        return carry

    lax.fori_loop(0, nq, q_tile, 0)


def _diff_attn(h, lam, lam_init, norm_g, wo_all, wpg_all, wpe_all, layer, batch, seq, tk=512):
    t = h.shape[0]
    tq = 2 * tk
    nq = seq // tq
    per = GROUP_W // HEAD_W
    ng2 = norm_g.reshape(1, GROUP_W).astype(F32)
    lam2 = jnp.reshape(lam, (1, 1)).astype(F32)
    kern = functools.partial(_diff_attn_kernel, tk=tk, nq=nq, out_scale=1.0 - lam_init)
    cast_cols = wo_all.shape[2] // (batch * N_HEADS)

    def head_cols(blk):
        return pl.BlockSpec((seq, HEAD_W), lambda b, hh, blk=blk: (b, blk * per + hh))

    def cast_in(a):
        return pl.BlockSpec((None, a.shape[1], cast_cols),
                            lambda b, hh: (layer, 0, b * N_HEADS + hh))

    def cast_out(a):
        return pl.BlockSpec((a.shape[1], cast_cols), lambda b, hh: (0, b * N_HEADS + hh))

    weights = (wo_all, wpg_all, wpe_all)
    return pl.pallas_call(
        kern,
        grid=(batch, N_HEADS),
        in_specs=[pl.BlockSpec(memory_space=pltpu.SMEM),
                  head_cols(C_Q), head_cols(C_K), head_cols(C_V), head_cols(G_C),
                  pl.BlockSpec((1, HEAD_W), lambda b, hh: (0, hh))]
                 + [cast_in(a) for a in weights],
        out_specs=[pl.BlockSpec((seq, HEAD_W), lambda b, hh: (b, hh))]
                  + [cast_out(a) for a in weights],
        out_shape=[jax.ShapeDtypeStruct((t, GROUP_W), BF16)]
                  + [jax.ShapeDtypeStruct(a.shape[1:], BF16) for a in weights],
        scratch_shapes=[pltpu.VMEM((2 * tq, HEAD_W), BF16),
                        *([pltpu.VMEM((2 * tq, tk), BF16)] * 4),
                        *([pltpu.VMEM((2 * tq, HEAD_W), F32)] * 13)],
        compiler_params=pltpu.CompilerParams(
            dimension_semantics=("arbitrary", "arbitrary"),
            vmem_limit_bytes=VMEM_LIMIT),
        name="diff_attn",
    )(lam2, h, h, h, h, ng2, *weights)


TAIL_ROWS = 256


def _tail_kernel(ab_ref, ac_ref, ax_ref, ga_ref, du_ref, dv_ref, gd_ref,
                 bq_ref, bf_ref, bi_ref, gb_ref, yc_ref, x_ref, p_ref,
                 cw_ref, sglng_ref, sglnb_ref, ws_ref, bst_ref,
                 hlb_ref, hng_ref, rng_ref, msk_ref,
                 wo_hbm, wpg_hbm, wpe_hbm, lng_ref, lnb_ref,
                 xo_ref, xb_ref,
                 ya_ref, yb_ref, yd_ref, carry_ref, st_ref,
                 wo_ref, wpg_ref, wpe_ref, wsem, *, seq, alpha):
    n = pl.program_id(0)

    @pl.when((n * TAIL_ROWS) % seq == 0)
    def _():
        carry_ref[...] = jnp.zeros_like(carry_ref)
        st_ref[...] = jnp.zeros_like(st_ref)

    wslot = n % 2
    rslot = 1 - wslot

    def mixers():
        yield from _conv_sgu_body(ab_ref, ac_ref, ax_ref, ga_ref, du_ref, dv_ref, gd_ref,
                                  cw_ref, sglng_ref, sglnb_ref, ws_ref, bst_ref,
                                  ya_ref.at[wslot], yd_ref.at[wslot], carry_ref, ts=TAIL_ROWS)
        yield from _hgrn_body(bq_ref, bf_ref, bi_ref, gb_ref, hlb_ref, hng_ref, rng_ref, msk_ref,
                              yb_ref.at[wslot], st_ref, nsub=TAIL_ROWS // HG_CHUNK)

    def projections():
        d = x_ref.shape[-1]
        blocks = [slice(n0, n0 + GROUP_W) for n0 in range(0, d, GROUP_W)]
        ys = ((yc_ref, 2), (ya_ref.at[rslot], 0), (yd_ref.at[rslot], 3), (yb_ref.at[rslot], 1))
        accs = []
        for cols in blocks:
            acc = alpha * x_ref[:, cols]
            for y, g in ys:
                acc = acc + jnp.dot(y[...], wo_ref[g * GROUP_W:(g + 1) * GROUP_W, cols],
                                    preferred_element_type=F32)
                yield
            accs.append(acc)
        pe = jnp.dot(p_ref[...].astype(BF16), wpe_ref[...], preferred_element_type=F32)
        mu = sum(jnp.sum(a, axis=-1, keepdims=True) for a in accs) * (1.0 / d)
        xcs = [a - mu for a in accs]
        var = sum(jnp.sum(xc * xc, axis=-1, keepdims=True) for xc in xcs) * (1.0 / d)
        rstd = lax.rsqrt(var + LN_EPS)
        xns = [xc * rstd * lng_ref[:, cols] + lnb_ref[:, cols] for xc, cols in zip(xcs, blocks)]
        xnbs = [xn.astype(BF16) for xn in xns]
        yield
        for xn, cols in zip(xns, blocks):
            z = None
            for xnb, rows in zip(xnbs, blocks):
                part = jnp.dot(xnb, wpg_ref[rows, cols], preferred_element_type=F32)
                z = part if z is None else z + part
                yield
            out = xn + pe[:, cols] * _sigmoid(z)
            xo_ref[:, cols] = out
            xb_ref[:, cols] = out.astype(BF16)

    last = pl.num_programs(0) - 1

    def weight_copies():
        pairs = ((wo_hbm, wo_ref), (wpg_hbm, wpg_ref), (wpe_hbm, wpe_ref))
        return [pltpu.make_async_copy(src, dst, wsem.at[k]) for k, (src, dst) in enumerate(pairs)]

    @pl.when(n == 0)
    def _():
        for cp in weight_copies():
            cp.start()
        for _ in mixers():
            pass

    @pl.when(n == 1)
    def _():
        for cp in weight_copies():
            cp.wait()

    @pl.when(n == last)
    def _():
        for _ in projections():
            pass

    @pl.when(jnp.logical_and(n > 0, n < last))
    def _():
        major, minor = mixers(), projections()
        live_major = live_minor = True
        while live_major or live_minor:
            live_major = live_major and next(major, False) is not False
            live_minor = live_minor and next(minor, False) is not False


def _tail_block(h, yc, x, p_all, small, wo_all, wpg_all, wpe_all, lng_all, lnb_all,
                layer, seq, alpha):
    t, d = x.shape
    tm = TAIL_ROWS
    pdim = p_all.shape[1]
    steps = t // tm

    def hblk(blk):
        return pl.BlockSpec((tm, GROUP_W), lambda i, blk=blk: (jnp.minimum(i, steps - 1), blk))

    def rows(w):
        return pl.BlockSpec((tm, w), lambda i: (jnp.maximum(i - 1, 0), 0))

    def full(a):
        nd = a.ndim
        return pl.BlockSpec(a.shape, lambda i, nd=nd: (0,) * nd)

    def resident(a):
        if a.ndim == 2:
            return pl.BlockSpec(a.shape, lambda i: (0, 0), pipeline_mode=pl.Buffered(1))
        return pl.BlockSpec((None,) + a.shape[1:], lambda i: (layer, 0, 0),
                            pipeline_mode=pl.Buffered(1))

    h_blocks = (A_B, A_C, A_X, G_A, D_U, D_V, G_D, B_Q, B_F, B_I, G_B)
    kern = functools.partial(_tail_kernel, seq=seq, alpha=alpha)
    return pl.pallas_call(
        kern,
        grid=(steps + 1,),
        in_specs=[hblk(b) for b in h_blocks]
                 + [rows(GROUP_W), rows(d),
                    pl.BlockSpec((tm, pdim),
                                 lambda i: (layer * steps + jnp.maximum(i - 1, 0), 0))]
                 + [full(a) for a in small]
                 + [pl.BlockSpec(memory_space=pl.ANY)] * 3
                 + [resident(lng_all), resident(lnb_all)],
        out_specs=[rows(d), rows(d)],
        out_shape=[jax.ShapeDtypeStruct((t, d), F32), jax.ShapeDtypeStruct((t, d), BF16)],
        scratch_shapes=[pltpu.VMEM((2, tm, GROUP_W), BF16), pltpu.VMEM((2, tm, GROUP_W), BF16),
                        pltpu.VMEM((2, tm, GROUP_W), BF16),
                        pltpu.VMEM((8, GROUP_W), F32),
                        pltpu.VMEM((N_HEADS, HEAD_W, HEAD_W), F32),
                        pltpu.VMEM(wo_all.shape, BF16), pltpu.VMEM(wpg_all.shape, BF16),
                        pltpu.VMEM(wpe_all.shape, BF16), pltpu.SemaphoreType.DMA((3,))],
        compiler_params=pltpu.CompilerParams(
            dimension_semantics=("arbitrary",), vmem_limit_bytes=VMEM_LIMIT),
        name="mix_out",
    )(*([h] * len(h_blocks)), yc, x, p_all, *small,
      wo_all, wpg_all, wpe_all, lng_all, lnb_all)


def kernel(x, p, w_in, conv_w, hgrn_lb, hgrn_norm_g, diff_lambda, diff_norm_g,
           sg_ln_g, sg_ln_b, sg_w, sg_b, w_out, ln_g, ln_b, w_pe, w_pg):
    batch, seq, d_model = x.shape
    depth = w_in.shape[0]
    t = batch * seq
    alpha = (2 * depth) ** 0.25

    lb_sm = jax.nn.softmax(hgrn_lb.astype(F32), axis=0)
    lower_bounds = jnp.cumsum(lb_sm, axis=0) - lb_sm[0]

    xf = x.reshape(t, d_model)
    xb = xf
    p_all = p.reshape(depth * t, p.shape[-1])
    lng_all = ln_g.reshape(depth, 1, d_model).astype(F32)
    lnb_all = ln_b.reshape(depth, 1, d_model).astype(F32)
    rng, masks = _hgrn_constants()
    for i in range(depth):
        lam_init = 0.8 - 0.6 * math.exp(-0.3 * i)
        dl = diff_lambda[i].astype(F32)
        lam = (jnp.exp(jnp.sum(dl[0] * dl[1])) - jnp.exp(jnp.sum(dl[2] * dl[3])) + lam_init)

        h = _in_proj(xb, w_in, i)
        yc, wo_b, wpg_b, wpe_b = _diff_attn(h, lam, lam_init, diff_norm_g[i],
                                            w_out, w_pg, w_pe, i, batch, seq)
        small = (conv_w[i].astype(F32),
                 sg_ln_g[i].reshape(1, GROUP_W).astype(F32), sg_ln_b[i].reshape(1, GROUP_W).astype(F32),
                 sg_w[i].astype(F32), sg_b[i].T.astype(F32),
                 lower_bounds[i].reshape(1, GROUP_W), hgrn_norm_g[i].reshape(1, GROUP_W).astype(F32),
                 rng, masks)
        xf, xb = _tail_block(h, yc, xf, p_all, small, wo_b, wpg_b, wpe_b,
                             lng_all, lnb_all, i, seq, alpha)
    return xf.reshape(batch, seq, d_model)
```

```python
import functools
import math

import numpy as np
import jax
import jax.numpy as jnp
from jax import lax
from jax.experimental import pallas as pl
from jax.experimental.pallas import tpu as pltpu

F32 = jnp.float32
BF16 = jnp.bfloat16

GROUP_W = 512
HEAD_W = 128
N_HEADS = GROUP_W // HEAD_W
DA_DQK = 64
F_FLOOR = 1e-30
MASK_VALUE = -1e30
LN_EPS = 1e-5
RMS_EPS = 1e-6

A_B, A_C, A_X, B_Q, B_F, B_I, C_Q, C_K, C_V, D_U, D_V, G_A, G_B, G_C, G_D = range(15)

HG_CHUNK = 128
HG_LEVELS = (2, 4, 8, 16, 32, 64, 128)
SG_CHUNK = 128
V7X_VMEM_BYTES = 64 * 1024 * 1024
VMEM_LIMIT = V7X_VMEM_BYTES - 8 * 1024 * 1024


def _sigmoid(x):
    return 1.0 / (1.0 + jnp.exp(-x))


def _silu(x):
    return x * _sigmoid(x)


def _gelu_tanh(x):
    c = math.sqrt(2.0 / math.pi)
    return 0.5 * x * (1.0 + jnp.tanh(c * (x + 0.044715 * (x * x * x))))


def _matmul_kernel(x_ref, w_ref, o_ref):
    o_ref[...] = jnp.dot(x_ref[...].astype(BF16), w_ref[...].astype(BF16),
                         preferred_element_type=F32).astype(o_ref.dtype)


def _matmul_f32x_kernel(x_hbm, w_ref, o_ref, xs_ref, xb_ref, sem, *, tm):
    i = pl.program_id(0)
    j = pl.program_id(1)

    def row_copy(r):
        return pltpu.make_async_copy(x_hbm.at[pl.ds(r * tm, tm), :], xs_ref, sem)

    @pl.when(jnp.logical_and(i == 0, j == 0))
    def _():
        row_copy(0).start()

    @pl.when(j == 0)
    def _():
        row_copy(i).wait()
        xb_ref[...] = xs_ref[...].astype(BF16)

        @pl.when(i + 1 < pl.num_programs(0))
        def _():
            row_copy(i + 1).start()

    o_ref[...] = jnp.dot(xb_ref[...], w_ref[...].astype(BF16),
                         preferred_element_type=F32).astype(o_ref.dtype)


def _in_proj(xb, w_all, layer, tm=2048, tn=768):
    t, k = xb.shape
    n = w_all.shape[2]
    if xb.dtype == jnp.float32:
        return pl.pallas_call(
            functools.partial(_matmul_f32x_kernel, tm=tm),
            grid=(t // tm, n // tn),
            in_specs=[pl.BlockSpec(memory_space=pl.ANY),
                      pl.BlockSpec((None, k, tn), lambda i, j: (layer, 0, j))],
            out_specs=pl.BlockSpec((tm, tn), lambda i, j: (i, j)),
            out_shape=jax.ShapeDtypeStruct((t, n), BF16),
            scratch_shapes=[pltpu.VMEM((tm, k), F32), pltpu.VMEM((tm, k), BF16),
                            pltpu.SemaphoreType.DMA(())],
            compiler_params=pltpu.CompilerParams(
                dimension_semantics=("arbitrary", "arbitrary"),
                vmem_limit_bytes=VMEM_LIMIT),
            name="in_proj",
        )(xb, w_all)
    return pl.pallas_call(
        _matmul_kernel,
        grid=(t // tm, n // tn),
        in_specs=[pl.BlockSpec((tm, k), lambda i, j: (i, 0)),
                  pl.BlockSpec((None, k, tn), lambda i, j: (layer, 0, j))],
        out_specs=pl.BlockSpec((tm, tn), lambda i, j: (i, j)),
        out_shape=jax.ShapeDtypeStruct((t, n), BF16),
        compiler_params=pltpu.CompilerParams(
            dimension_semantics=("arbitrary", "arbitrary"),
            vmem_limit_bytes=VMEM_LIMIT),
        name="in_proj",
    )(xb, w_all)


def _conv_sgu_body(ab_ref, ac_ref, ax_ref, ga_ref, du_ref, dv_ref, gd_ref,
                   cw_ref, lng_ref, lnb_ref, ws_ref, bst_ref,
                   ya_ref, yd_ref, carry_ref, *, ts):
    z = ac_ref[...].astype(F32) * ax_ref[...].astype(F32)
    rows = lax.broadcasted_iota(jnp.int32, z.shape, 0)
    prev1 = carry_ref[7:8, :]
    prev2 = carry_ref[6:7, :]
    z1 = jnp.where(rows == 0, prev1, pltpu.roll(z, 1, 0))
    z2 = jnp.where(rows == 0, prev2, jnp.where(rows == 1, prev1, pltpu.roll(z, 2, 0)))
    cw = cw_ref[...]
    y = cw[0:1, :] * z2 + cw[1:2, :] * z1 + cw[2:3, :] * z
    ya = ab_ref[...].astype(F32) * y * _silu(ga_ref[...].astype(F32))
    ya_ref[...] = ya.astype(ya_ref.dtype)
    carry_ref[...] = z[ts - 8:ts, :]
    yield

    u = _gelu_tanh(du_ref[...].astype(F32))
    v = _gelu_tanh(dv_ref[...].astype(F32))
    mu = jnp.mean(v, axis=-1, keepdims=True)
    vc = v - mu
    var = jnp.mean(vc * vc, axis=-1, keepdims=True)
    vn = (vc * lax.rsqrt(var + LN_EPS) * lng_ref[...] + lnb_ref[...]).astype(BF16)
    gate = _silu(gd_ref[...].astype(F32))
    yield
    tri_r = lax.broadcasted_iota(jnp.int32, (SG_CHUNK, SG_CHUNK), 0)
    tri_c = lax.broadcasted_iota(jnp.int32, (SG_CHUNK, SG_CHUNK), 1)
    bst = bst_ref[...]
    for g in range(N_HEADS):
        w = jnp.where(tri_c <= tri_r, ws_ref[g], 0.0).astype(BF16)
        bias = bst[:, g:g + 1]
        lo = g * HEAD_W
        for c in range(ts // SG_CHUNK):
            r0 = c * SG_CHUNK
            sv = jnp.dot(w, vn[r0:r0 + SG_CHUNK, lo:lo + HEAD_W],
                         preferred_element_type=F32) + bias
            yd = u[r0:r0 + SG_CHUNK, lo:lo + HEAD_W] * sv * gate[r0:r0 + SG_CHUNK, lo:lo + HEAD_W]
            yd_ref[r0:r0 + SG_CHUNK, lo:lo + HEAD_W] = yd.astype(yd_ref.dtype)
        if g % 2 == 1:
            yield


def _hgrn_constants():
    c = HG_CHUNK
    t = np.arange(c)[:, None]
    s = np.arange(c)[None, :]
    mats = [(s <= t)]
    for lv in HG_LEVELS[1:]:
        mid = (t // lv) * lv + lv // 2
        qside = t >= mid
        mats.append(np.where(qside, (s >= mid) & (s <= t), (s > t) & (s < mid)))
    rng = np.concatenate(mats, axis=0).astype(np.float32)
    rng = np.concatenate([rng, rng], axis=1)
    masks = np.stack([t == s] + [(t // lv) == (s // lv) for lv in HG_LEVELS]).astype(np.float32)
    masks = np.concatenate([masks, masks], axis=2)
    return jnp.asarray(rng, BF16), jnp.asarray(masks, F32)


def _hgrn_body(q_ref, f_ref, i_ref, g_ref, lb_ref, ng_ref, rng_ref, msk_ref,
               y_ref, st_ref, *, nsub):
    c = HG_CHUNK
    fz = f_ref[...].astype(F32)
    lb = lb_ref[...]
    e = jnp.exp(-jnp.abs(fz))
    r = 1.0 / (1.0 + e)
    pos = fz >= 0.0
    sig = jnp.where(pos, r, e * r)
    nsig = jnp.where(pos, e * r, r)
    f_all = jnp.maximum(lb + (1.0 - lb) * sig, F_FLOOR)
    lg = jnp.log(f_all)
    kk_all = (1.0 - lb) * nsig

    g1 = lg.astype(BF16)
    g2 = (lg - g1.astype(F32)).astype(BF16)
    rng = rng_ref[...]
    exs = [jnp.dot(rng, jnp.concatenate([g1[u * c:(u + 1) * c], g2[u * c:(u + 1) * c]], axis=0),
                   preferred_element_type=F32)
           for u in range(nsub)]
    yield

    rows = lax.broadcasted_iota(jnp.int32, (c, HEAD_W), 0)
    zblk = jnp.zeros((c, HEAD_W), BF16)
    nt = (((1,), (1,)), ((), ()))

    def pair_rows(x0, x1):
        return jnp.concatenate([jnp.concatenate([x0, zblk], axis=1),
                                jnp.concatenate([zblk, x1], axis=1)], axis=0)

    for hp in range(N_HEADS // 2):
        los = (2 * hp * HEAD_W, (2 * hp + 1) * HEAD_W)
        sts = [st_ref[2 * hp], st_ref[2 * hp + 1]]
        for u in range(nsub):
            ex = exs[u]
            r0 = u * c
            qs, kks, vbs, outs = [], [], [], []
            for n, lo in enumerate(los):
                q = q_ref[r0:r0 + c, lo:lo + HEAD_W].astype(F32)
                vb = i_ref[r0:r0 + c, lo:lo + HEAD_W]
                kk = kk_all[r0:r0 + c, lo:lo + HEAD_W]
                bcum = ex[0:c, lo:lo + HEAD_W]
                q_in = (q * jnp.exp(bcum)).astype(BF16)
                outs.append(lax.dot_general(q_in, sts[n].astype(BF16), nt,
                                            preferred_element_type=F32))
                k_out = (kk * jnp.exp(bcum[c - 1:c, :] - bcum)).astype(BF16)
                e_last = jnp.exp(bcum[c - 1:c, :])
                sts[n] = e_last * sts[n] + jnp.dot(vb.astype(F32).T.astype(BF16), k_out,
                                                   preferred_element_type=F32)
                qs.append(q)
                kks.append(kk)
                vbs.append(vb)
            yield

            a = msk_ref[0] * lax.dot_general(
                jnp.concatenate([qs[0].astype(BF16), qs[1].astype(BF16)], axis=1),
                pair_rows(kks[0].astype(BF16), kks[1].astype(BF16)), nt,
                preferred_element_type=F32)
            for li, lv in enumerate(HG_LEVELS):
                qside = (rows & (lv - 1)) >= (lv // 2)
                qls, kls = [], []
                for n, lo in enumerate(los):
                    if lv == 2:
                        eq, ek = f_all[r0:r0 + c, lo:lo + HEAD_W], None
                    else:
                        eq = ek = jnp.exp(ex[li * c:(li + 1) * c, lo:lo + HEAD_W])
                    qls.append(jnp.where(qside, qs[n] * eq, 0.0).astype(BF16))
                    kls.append(jnp.where(qside, 0.0, kks[n] if ek is None else kks[n] * ek)
                               .astype(BF16))
                al = lax.dot_general(jnp.concatenate(qls, axis=1), pair_rows(kls[0], kls[1]), nt,
                                     preferred_element_type=F32)
                a = a + msk_ref[1 + li] * al
                if li % 2 == 0:
                    yield
            o2 = jnp.dot(a.astype(BF16), pair_rows(vbs[0], vbs[1]),
                         preferred_element_type=F32)

            for n, lo in enumerate(los):
                o = outs[n] + o2[:, n * HEAD_W:(n + 1) * HEAD_W]
                ms = jnp.mean(o * o, axis=-1, keepdims=True)
                on = o * lax.rsqrt(ms + RMS_EPS) * ng_ref[:, lo:lo + HEAD_W]
                y = on * _silu(g_ref[r0:r0 + c, lo:lo + HEAD_W].astype(F32))
                y_ref[r0:r0 + c, lo:lo + HEAD_W] = y.astype(y_ref.dtype)
            yield
        st_ref[2 * hp] = sts[0]
        st_ref[2 * hp + 1] = sts[1]


ATTN_ROWS = 64


def _diff_attn_kernel(lam_ref, q_ref, k_ref, v_ref, g_ref, ng_ref, wo_ref, wpg_ref, wpe_ref,
                      o_ref, wo_o, wpg_o, wpe_o,
                      qq_ref, pa_ref, pb_ref, pt_ref, pu_ref,
                      ma_ref, la_ref, aa_ref, mb_ref, lb_ref, ab_ref,
                      mt_ref, lt_ref, at_ref, mu_ref, lu_ref, au_ref,
                      acc_ref, *, tk, nq, out_scale):
    wo_o[...] = wo_ref[...].astype(BF16)
    wpg_o[...] = wpg_ref[...].astype(BF16)
    wpe_o[...] = wpe_ref[...].astype(BF16)

    lam = lam_ref[0, 0]
    scale = DA_DQK ** -0.5
    tq = 2 * tk
    nrow = 2 * tq
    st_a = (ma_ref, la_ref, aa_ref)
    st_b = (mb_ref, lb_ref, ab_ref)
    st_t = (mt_ref, lt_ref, at_ref)
    st_u = (mu_ref, lu_ref, au_ref)

    hq = tk // 2

    def scores(j, ra=0, rb=nrow, nkeys=tk):
        kb = k_ref[pl.ds(pl.multiple_of(j * tk, tk), nkeys), :]
        return lax.dot_general(qq_ref[ra:rb, :], kb, (((1,), (1,)), ((), ())),
                               preferred_element_type=F32)

    def accumulate(j, p_ref, st, ra=0, rb=nrow, nkeys=tk):
        vb = v_ref[pl.ds(pl.multiple_of(j * tk, tk), nkeys), :]
        acc_ref[ra:rb, :] = (st[2][ra:rb, :] * acc_ref[ra:rb, :]
                             + jnp.dot(p_ref[ra:rb, 0:nkeys], vb, preferred_element_type=F32))

    def softmax(s_all, p_ref, st_in, st_out, diag=None, ra=0, rb=nrow):
        masked = diag is not None
        for r0 in range(ra, rb, ATTN_ROWS):
            rs = slice(r0, r0 + ATTN_ROWS)
            row_lo = (r0 // tk) * hq + r0 % hq
            key_lo = diag * tk if masked else 0
            tiles = []
            for cb in range(s_all.shape[1] // HEAD_W):
                k0 = key_lo + cb * HEAD_W
                if masked and k0 > row_lo + ATTN_ROWS - 1:
                    tiles.append(None)
                    continue
                s = s_all[r0 - ra:r0 - ra + ATTN_ROWS, cb * HEAD_W:(cb + 1) * HEAD_W]
                if masked and k0 + HEAD_W - 1 > row_lo:
                    r = row_lo + lax.broadcasted_iota(jnp.int32, s.shape, 0)
                    cidx = k0 + lax.broadcasted_iota(jnp.int32, s.shape, 1)
                    s = jnp.where(cidx <= r, s, MASK_VALUE)
                tiles.append(s)
            live = [s for s in tiles if s is not None]
            mx = functools.reduce(jnp.maximum, live)
            m_blk = jnp.broadcast_to(jnp.max(mx, axis=-1, keepdims=True), mx.shape)
            m_old = st_in[0][rs, :]
            m_new = jnp.maximum(m_old, m_blk)
            alpha = jnp.exp2(m_old - m_new)
            psum = None
            for cb, s in enumerate(tiles):
                if s is None:
                    p_ref[rs, cb * HEAD_W:(cb + 1) * HEAD_W] = jnp.zeros((ATTN_ROWS, HEAD_W), BF16)
                    continue
                p = jnp.exp2(s - m_new)
                psum = p if psum is None else psum + p
                p_ref[rs, cb * HEAD_W:(cb + 1) * HEAD_W] = p.astype(BF16)
            st_out[0][rs, :] = m_new
            st_out[1][rs, :] = alpha * st_in[1][rs, :] + psum
            st_out[2][rs, :] = alpha

    def step(j, p_in, p_out, st_in, st_out):
        s = scores(j)
        accumulate(jnp.maximum(j - 1, 0), p_in, st_in)
        softmax(s, p_out, st_in, st_out)

    def pair(jj, carry):
        step(2 * jj, pb_ref, pa_ref, st_b, st_a)
        step(2 * jj + 1, pa_ref, pb_ref, st_a, st_b)
        return carry

    def quad(qq, carry):
        pair(2 * qq, carry)
        return pair(2 * qq + 1, carry)

    def q_tile(i, carry):
        for c in range(nrow // tk):
            rows = pl.ds(pl.multiple_of(i * tq + c * hq, hq), hq)
            q = (q_ref[rows, :].astype(F32) * (scale * math.log2(math.e))).astype(BF16)
            lane = lax.broadcasted_iota(jnp.int32, q.shape, 1)
            zero = jnp.zeros_like(q)
            qq_ref[c * tk:c * tk + hq, :] = jnp.where(lane < DA_DQK, q, zero)
            qq_ref[c * tk + hq:(c + 1) * tk, :] = jnp.where(lane >= DA_DQK, q, zero)
        acc_ref[...] = jnp.zeros_like(acc_ref)
        mb_ref[...] = jnp.full_like(mb_ref, MASK_VALUE)
        lb_ref[...] = jnp.zeros_like(lb_ref)
        ab_ref[...] = jnp.ones_like(ab_ref)
        pb_ref[...] = jnp.zeros_like(pb_ref)

        lax.fori_loop(0, i // 2, quad, 0)
        lax.fori_loop(2 * (i // 2), i, pair, 0)

        j0 = 2 * i
        s_q0 = scores(j0, 0, tk, hq)
        s_rest = scores(j0, tk, nrow)
        accumulate(jnp.maximum(j0 - 1, 0), pb_ref, st_b)
        softmax(s_q0, pt_ref, st_b, st_t, diag=0, ra=0, rb=tk)
        softmax(s_rest, pt_ref, st_b, st_t, diag=0, ra=tk, rb=nrow)
        s_q2 = scores(j0 + 1, 2 * tk, 3 * tk, hq)
        s_q3 = scores(j0 + 1, 3 * tk, nrow)
        accumulate(j0, pt_ref, st_t, 0, tk, hq)
        accumulate(j0, pt_ref, st_t, tk, nrow)
        softmax(s_q2, pu_ref, st_t, st_u, diag=1, ra=2 * tk, rb=3 * tk)
        softmax(s_q3, pu_ref, st_t, st_u, diag=1, ra=3 * tk, rb=nrow)
        accumulate(j0 + 1, pu_ref, st_u, 2 * tk, 3 * tk, hq)
        accumulate(j0 + 1, pu_ref, st_u, 3 * tk, nrow)

        for c in range(nrow // tk):
            st = st_t if c < 2 else st_u
            r0 = c * tk
            l = jnp.sum(st[1][r0:r0 + tk, :], axis=-1, keepdims=True)
            acc = acc_ref[r0:r0 + tk, :] / l
            o = acc[0:hq] - lam * acc[hq:tk]
            ms = jnp.mean(o * o, axis=-1, keepdims=True)
            on = o * lax.rsqrt(ms + RMS_EPS) * ng_ref[...] * out_scale
            rows = pl.ds(pl.multiple_of(i * tq + c * hq, hq), hq)
            o_ref[rows, :] = (on * _silu(g_ref[rows, :].astype(F32))).astype(o_ref.dtype)
        return carry

    lax.fori_loop(0, nq, q_tile, 0)


def _diff_attn(h, lam, lam_init, norm_g, wo_all, wpg_all, wpe_all, layer, batch, seq, tk=512):
    t = h.shape[0]
    tq = 2 * tk
    nq = seq // tq
    per = GROUP_W // HEAD_W
    ng2 = norm_g.reshape(1, GROUP_W).astype(F32)
    lam2 = jnp.reshape(lam, (1, 1)).astype(F32)
    kern = functools.partial(_diff_attn_kernel, tk=tk, nq=nq, out_scale=1.0 - lam_init)
    cast_cols = wo_all.shape[2] // (batch * N_HEADS)

    def head_cols(blk):
        return pl.BlockSpec((seq, HEAD_W), lambda b, hh, blk=blk: (b, blk * per + hh))

    def cast_in(a):
        return pl.BlockSpec((None, a.shape[1], cast_cols),
                            lambda b, hh: (layer, 0, b * N_HEADS + hh))

    def cast_out(a):
        return pl.BlockSpec((a.shape[1], cast_cols), lambda b, hh: (0, b * N_HEADS + hh))

    weights = (wo_all, wpg_all, wpe_all)
    return pl.pallas_call(
        kern,
        grid=(batch, N_HEADS),
        in_specs=[pl.BlockSpec(memory_space=pltpu.SMEM),
                  head_cols(C_Q), head_cols(C_K), head_cols(C_V), head_cols(G_C),
                  pl.BlockSpec((1, HEAD_W), lambda b, hh: (0, hh))]
                 + [cast_in(a) for a in weights],
        out_specs=[pl.BlockSpec((seq, HEAD_W), lambda b, hh: (b, hh))]
                  + [cast_out(a) for a in weights],
        out_shape=[jax.ShapeDtypeStruct((t, GROUP_W), BF16)]
                  + [jax.ShapeDtypeStruct(a.shape[1:], BF16) for a in weights],
        scratch_shapes=[pltpu.VMEM((2 * tq, HEAD_W), BF16),
                        *([pltpu.VMEM((2 * tq, tk), BF16)] * 4),
                        *([pltpu.VMEM((2 * tq, HEAD_W), F32)] * 13)],
        compiler_params=pltpu.CompilerParams(
            dimension_semantics=("arbitrary", "arbitrary"),
            vmem_limit_bytes=VMEM_LIMIT),
        name="diff_attn",
    )(lam2, h, h, h, h, ng2, *weights)


TAIL_ROWS = 256


def _tail_kernel(a3_ref, ga_ref, du_ref, dv_ref, gd_ref,
                 b3_ref, gb_ref, yc_ref, x_ref, p_ref,
                 cw_ref, sglng_ref, sglnb_ref, ws_ref, bst_ref,
                 hlb_ref, hng_ref, rng_ref, msk_ref,
                 wo_hbm, wpg_hbm, wpe_hbm, lng_ref, lnb_ref,
                 xo_ref, xb_ref,
                 ya_ref, yb_ref, yd_ref, carry_ref, st_ref,
                 wo_ref, wpg_ref, wpe_ref, wsem, *, seq, alpha):
    n = pl.program_id(0)
    ab_ref, ac_ref, ax_ref = (a3_ref.at[:, k * GROUP_W:(k + 1) * GROUP_W] for k in range(3))
    bq_ref, bf_ref, bi_ref = (b3_ref.at[:, k * GROUP_W:(k + 1) * GROUP_W] for k in range(3))

    @pl.when((n * TAIL_ROWS) % seq == 0)
    def _():
        carry_ref[...] = jnp.zeros_like(carry_ref)
        st_ref[...] = jnp.zeros_like(st_ref)

    wslot = n % 2
    rslot = 1 - wslot

    def mixers():
        yield from _conv_sgu_body(ab_ref, ac_ref, ax_ref, ga_ref, du_ref, dv_ref, gd_ref,
                                  cw_ref, sglng_ref, sglnb_ref, ws_ref, bst_ref,
                                  ya_ref.at[wslot], yd_ref.at[wslot], carry_ref, ts=TAIL_ROWS)
        yield from _hgrn_body(bq_ref, bf_ref, bi_ref, gb_ref, hlb_ref, hng_ref, rng_ref, msk_ref,
                              yb_ref.at[wslot], st_ref, nsub=TAIL_ROWS // HG_CHUNK)

    def projections():
        d = x_ref.shape[-1]
        blocks = [slice(n0, n0 + GROUP_W) for n0 in range(0, d, GROUP_W)]
        ys = ((yc_ref, 2), (ya_ref.at[rslot], 0), (yd_ref.at[rslot], 3), (yb_ref.at[rslot], 1))
        accs = []
        for cols in blocks:
            acc = alpha * x_ref[:, cols]
            for y, g in ys:
                acc = acc + jnp.dot(y[...], wo_ref[g * GROUP_W:(g + 1) * GROUP_W, cols],
                                    preferred_element_type=F32)
                yield
            accs.append(acc)
        pe = jnp.dot(p_ref[...].astype(BF16), wpe_ref[...], preferred_element_type=F32)
        mu = sum(jnp.sum(a, axis=-1, keepdims=True) for a in accs) * (1.0 / d)
        xcs = [a - mu for a in accs]
        var = sum(jnp.sum(xc * xc, axis=-1, keepdims=True) for xc in xcs) * (1.0 / d)
        rstd = lax.rsqrt(var + LN_EPS)
        xns = [xc * rstd * lng_ref[:, cols] + lnb_ref[:, cols] for xc, cols in zip(xcs, blocks)]
        xnbs = [xn.astype(BF16) for xn in xns]
        yield
        for xn, cols in zip(xns, blocks):
            z = None
            for xnb, rows in zip(xnbs, blocks):
                part = jnp.dot(xnb, wpg_ref[rows, cols], preferred_element_type=F32)
                z = part if z is None else z + part
                yield
            out = xn + pe[:, cols] * _sigmoid(z)
            xo_ref[:, cols] = out
            xb_ref[:, cols] = out.astype(BF16)

    last = pl.num_programs(0) - 1

    def weight_copies():
        pairs = ((wo_hbm, wo_ref), (wpg_hbm, wpg_ref), (wpe_hbm, wpe_ref))
        return [pltpu.make_async_copy(src, dst, wsem.at[k]) for k, (src, dst) in enumerate(pairs)]

    @pl.when(n == 0)
    def _():
        for cp in weight_copies():
            cp.start()
        for _ in mixers():
            pass

    @pl.when(n == 1)
    def _():
        for cp in weight_copies():
            cp.wait()

    @pl.when(n == last)
    def _():
        for _ in projections():
            pass

    @pl.when(jnp.logical_and(n > 0, n < last))
    def _():
        major, minor = mixers(), projections()
        live_major = live_minor = True
        while live_major or live_minor:
            live_major = live_major and next(major, False) is not False
            live_minor = live_minor and next(minor, False) is not False


def _tail_block(h, yc, x, p_all, small, wo_all, wpg_all, wpe_all, lng_all, lnb_all,
                layer, seq, alpha):
    t, d = x.shape
    tm = TAIL_ROWS
    pdim = p_all.shape[1]
    steps = t // tm

    def hblk(blk):
        return pl.BlockSpec((tm, GROUP_W), lambda i, blk=blk: (jnp.minimum(i, steps - 1), blk))

    def hblk3(blk):
        return pl.BlockSpec((tm, 3 * GROUP_W),
                            lambda i, blk=blk: (jnp.minimum(i, steps - 1), blk // 3))

    def rows(w):
        return pl.BlockSpec((tm, w), lambda i: (jnp.maximum(i - 1, 0), 0))

    def full(a):
        nd = a.ndim
        return pl.BlockSpec(a.shape, lambda i, nd=nd: (0,) * nd)

    def resident(a):
        if a.ndim == 2:
            return pl.BlockSpec(a.shape, lambda i: (0, 0), pipeline_mode=pl.Buffered(1))
        return pl.BlockSpec((None,) + a.shape[1:], lambda i: (layer, 0, 0),
                            pipeline_mode=pl.Buffered(1))

    assert (A_B, A_C, A_X) == (0, 1, 2) and (B_Q, B_F, B_I) == (3, 4, 5)
    h_specs = [hblk3(A_B), hblk(G_A), hblk(D_U), hblk(D_V), hblk(G_D), hblk3(B_Q), hblk(G_B)]
    kern = functools.partial(_tail_kernel, seq=seq, alpha=alpha)
    return pl.pallas_call(
        kern,
        grid=(steps + 1,),
        in_specs=h_specs
                 + [rows(GROUP_W), rows(d),
                    pl.BlockSpec((tm, pdim),
                                 lambda i: (layer * steps + jnp.maximum(i - 1, 0), 0))]
                 + [full(a) for a in small]
                 + [pl.BlockSpec(memory_space=pl.ANY)] * 3
                 + [resident(lng_all), resident(lnb_all)],
        out_specs=[rows(d), rows(d)],
        out_shape=[jax.ShapeDtypeStruct((t, d), F32), jax.ShapeDtypeStruct((t, d), BF16)],
        scratch_shapes=[pltpu.VMEM((2, tm, GROUP_W), BF16), pltpu.VMEM((2, tm, GROUP_W), BF16),
                        pltpu.VMEM((2, tm, GROUP_W), BF16),
                        pltpu.VMEM((8, GROUP_W), F32),
                        pltpu.VMEM((N_HEADS, HEAD_W, HEAD_W), F32),
                        pltpu.VMEM(wo_all.shape, BF16), pltpu.VMEM(wpg_all.shape, BF16),
                        pltpu.VMEM(wpe_all.shape, BF16), pltpu.SemaphoreType.DMA((3,))],
        compiler_params=pltpu.CompilerParams(
            dimension_semantics=("arbitrary",), vmem_limit_bytes=VMEM_LIMIT),
        name="mix_out",
    )(*([h] * len(h_specs)), yc, x, p_all, *small,
      wo_all, wpg_all, wpe_all, lng_all, lnb_all)


def kernel(x, p, w_in, conv_w, hgrn_lb, hgrn_norm_g, diff_lambda, diff_norm_g,
           sg_ln_g, sg_ln_b, sg_w, sg_b, w_out, ln_g, ln_b, w_pe, w_pg):
    batch, seq, d_model = x.shape
    depth = w_in.shape[0]
    t = batch * seq
    alpha = (2 * depth) ** 0.25

    lb_sm = jax.nn.softmax(hgrn_lb.astype(F32), axis=0)
    lower_bounds = jnp.cumsum(lb_sm, axis=0) - lb_sm[0]

    xf = x.reshape(t, d_model)
    xb = xf
    p_all = p.reshape(depth * t, p.shape[-1])
    lng_all = ln_g.reshape(depth, 1, d_model).astype(F32)
    lnb_all = ln_b.reshape(depth, 1, d_model).astype(F32)
    rng, masks = _hgrn_constants()
    for i in range(depth):
        lam_init = 0.8 - 0.6 * math.exp(-0.3 * i)
        dl = diff_lambda[i].astype(F32)
        lam = (jnp.exp(jnp.sum(dl[0] * dl[1])) - jnp.exp(jnp.sum(dl[2] * dl[3])) + lam_init)

        h = _in_proj(xb, w_in, i)
        yc, wo_b, wpg_b, wpe_b = _diff_attn(h, lam, lam_init, diff_norm_g[i],
                                            w_out, w_pg, w_pe, i, batch, seq)
        small = (conv_w[i].astype(F32),
                 sg_ln_g[i].reshape(1, GROUP_W).astype(F32), sg_ln_b[i].reshape(1, GROUP_W).astype(F32),
                 sg_w[i].astype(F32), sg_b[i].T.astype(F32),
                 lower_bounds[i].reshape(1, GROUP_W), hgrn_norm_g[i].reshape(1, GROUP_W).astype(F32),
                 rng, masks)
        xf, xb = _tail_block(h, yc, xf, p_all, small, wo_b, wpg_b, wpe_b,
                             lng_all, lnb_all, i, seq, alpha)
    return xf.reshape(batch, seq, d_model)
```

```python
import functools
import math

import numpy as np
import jax
import jax.numpy as jnp
from jax import lax
from jax.experimental import pallas as pl
from jax.experimental.pallas import tpu as pltpu

F32 = jnp.float32
BF16 = jnp.bfloat16

GROUP_W = 512
HEAD_W = 128
N_HEADS = GROUP_W // HEAD_W
DA_DQK = 64
F_FLOOR = 1e-30
MASK_VALUE = -1e30
LN_EPS = 1e-5
RMS_EPS = 1e-6

A_B, A_C, A_X, B_Q, B_F, B_I, C_Q, C_K, C_V, D_U, D_V, G_A, G_B, G_C, G_D = range(15)

HG_CHUNK = 128
HG_LEVELS = (2, 4, 8, 16, 32, 64, 128)
SG_CHUNK = 128
V7X_VMEM_BYTES = 64 * 1024 * 1024
VMEM_LIMIT = V7X_VMEM_BYTES - 8 * 1024 * 1024


def _sigmoid(x):
    return 1.0 / (1.0 + jnp.exp(-x))


def _silu(x):
    return x * _sigmoid(x)


def _gelu_tanh(x):
    c = math.sqrt(2.0 / math.pi)
    return 0.5 * x * (1.0 + jnp.tanh(c * (x + 0.044715 * (x * x * x))))


def _matmul_kernel(x_ref, w_ref, o_ref):
    o_ref[...] = jnp.dot(x_ref[...].astype(BF16), w_ref[...].astype(BF16),
                         preferred_element_type=F32).astype(o_ref.dtype)


def _matmul_f32x_kernel(x_hbm, w_ref, o_ref, xs_ref, xb_ref, sem, *, tm):
    i = pl.program_id(0)
    j = pl.program_id(1)

    def row_copy(r):
        return pltpu.make_async_copy(x_hbm.at[pl.ds(r * tm, tm), :], xs_ref, sem)

    @pl.when(jnp.logical_and(i == 0, j == 0))
    def _():
        row_copy(0).start()

    @pl.when(j == 0)
    def _():
        row_copy(i).wait()
        xb_ref[...] = xs_ref[...].astype(BF16)

        @pl.when(i + 1 < pl.num_programs(0))
        def _():
            row_copy(i + 1).start()

    o_ref[...] = jnp.dot(xb_ref[...], w_ref[...].astype(BF16),
                         preferred_element_type=F32).astype(o_ref.dtype)


def _in_proj(xb, w_all, layer, tm=2048, tn=768):
    t, k = xb.shape
    n = w_all.shape[2]
    if xb.dtype == jnp.float32:
        return pl.pallas_call(
            functools.partial(_matmul_f32x_kernel, tm=tm),
            grid=(t // tm, n // tn),
            in_specs=[pl.BlockSpec(memory_space=pl.ANY),
                      pl.BlockSpec((None, k, tn), lambda i, j: (layer, 0, j))],
            out_specs=pl.BlockSpec((tm, tn), lambda i, j: (i, j)),
            out_shape=jax.ShapeDtypeStruct((t, n), BF16),
            scratch_shapes=[pltpu.VMEM((tm, k), F32), pltpu.VMEM((tm, k), BF16),
                            pltpu.SemaphoreType.DMA(())],
            compiler_params=pltpu.CompilerParams(
                dimension_semantics=("arbitrary", "arbitrary"),
                vmem_limit_bytes=VMEM_LIMIT),
            name="in_proj",
        )(xb, w_all)
    return pl.pallas_call(
        _matmul_kernel,
        grid=(t // tm, n // tn),
        in_specs=[pl.BlockSpec((tm, k), lambda i, j: (i, 0)),
                  pl.BlockSpec((None, k, tn), lambda i, j: (layer, 0, j))],
        out_specs=pl.BlockSpec((tm, tn), lambda i, j: (i, j)),
        out_shape=jax.ShapeDtypeStruct((t, n), BF16),
        compiler_params=pltpu.CompilerParams(
            dimension_semantics=("arbitrary", "arbitrary"),
            vmem_limit_bytes=VMEM_LIMIT),
        name="in_proj",
    )(xb, w_all)


def _conv_sgu_body(ab_ref, ac_ref, ax_ref, ga_ref, du_ref, dv_ref, gd_ref,
                   cw_ref, lng_ref, lnb_ref, ws_ref, bst_ref,
                   ya_ref, yd_ref, carry_ref, *, ts):
    z = ac_ref[...].astype(F32) * ax_ref[...].astype(F32)
    rows = lax.broadcasted_iota(jnp.int32, z.shape, 0)
    prev1 = carry_ref[7:8, :]
    prev2 = carry_ref[6:7, :]
    z1 = jnp.where(rows == 0, prev1, pltpu.roll(z, 1, 0))
    z2 = jnp.where(rows == 0, prev2, jnp.where(rows == 1, prev1, pltpu.roll(z, 2, 0)))
    cw = cw_ref[...]
    y = cw[0:1, :] * z2 + cw[1:2, :] * z1 + cw[2:3, :] * z
    ya = ab_ref[...].astype(F32) * y * _silu(ga_ref[...].astype(F32))
    ya_ref[...] = ya.astype(ya_ref.dtype)
    carry_ref[...] = z[ts - 8:ts, :]
    yield

    u = _gelu_tanh(du_ref[...].astype(F32))
    v = _gelu_tanh(dv_ref[...].astype(F32))
    mu = jnp.mean(v, axis=-1, keepdims=True)
    vc = v - mu
    var = jnp.mean(vc * vc, axis=-1, keepdims=True)
    vn = (vc * lax.rsqrt(var + LN_EPS) * lng_ref[...] + lnb_ref[...]).astype(BF16)
    gate = _silu(gd_ref[...].astype(F32))
    yield
    tri_r = lax.broadcasted_iota(jnp.int32, (SG_CHUNK, SG_CHUNK), 0)
    tri_c = lax.broadcasted_iota(jnp.int32, (SG_CHUNK, SG_CHUNK), 1)
    bst = bst_ref[...]
    for g in range(N_HEADS):
        w = jnp.where(tri_c <= tri_r, ws_ref[g], 0.0).astype(BF16)
        bias = bst[:, g:g + 1]
        lo = g * HEAD_W
        for c in range(ts // SG_CHUNK):
            r0 = c * SG_CHUNK
            sv = jnp.dot(w, vn[r0:r0 + SG_CHUNK, lo:lo + HEAD_W],
                         preferred_element_type=F32) + bias
            yd = u[r0:r0 + SG_CHUNK, lo:lo + HEAD_W] * sv * gate[r0:r0 + SG_CHUNK, lo:lo + HEAD_W]
            yd_ref[r0:r0 + SG_CHUNK, lo:lo + HEAD_W] = yd.astype(yd_ref.dtype)
        if g % 2 == 1:
            yield


def _hgrn_constants():
    c = HG_CHUNK
    t = np.arange(c)[:, None]
    s = np.arange(c)[None, :]
    mats = [(s <= t)]
    for lv in HG_LEVELS[1:]:
        mid = (t // lv) * lv + lv // 2
        qside = t >= mid
        mats.append(np.where(qside, (s >= mid) & (s <= t), (s > t) & (s < mid)))
    rng = np.concatenate(mats, axis=0).astype(np.float32)
    rng = np.concatenate([rng, rng], axis=1)
    masks = np.stack([t == s] + [(t // lv) == (s // lv) for lv in HG_LEVELS]).astype(np.float32)
    masks = np.concatenate([masks, masks], axis=2)
    return jnp.asarray(rng, BF16), jnp.asarray(masks, F32)


def _hgrn_body(q_ref, f_ref, i_ref, g_ref, lb_ref, ng_ref, rng_ref, msk_ref,
               y_ref, st_ref, *, nsub):
    c = HG_CHUNK
    fz = f_ref[...].astype(F32)
    lb = lb_ref[...]
    e = jnp.exp(-jnp.abs(fz))
    r = 1.0 / (1.0 + e)
    pos = fz >= 0.0
    sig = jnp.where(pos, r, e * r)
    nsig = jnp.where(pos, e * r, r)
    f_all = jnp.maximum(lb + (1.0 - lb) * sig, F_FLOOR)
    lg = jnp.log(f_all)
    kk_all = (1.0 - lb) * nsig

    g1 = lg.astype(BF16)
    g2 = (lg - g1.astype(F32)).astype(BF16)
    rng = rng_ref[...]
    exs = [jnp.dot(rng, jnp.concatenate([g1[u * c:(u + 1) * c], g2[u * c:(u + 1) * c]], axis=0),
                   preferred_element_type=F32)
           for u in range(nsub)]
    yield

    rows = lax.broadcasted_iota(jnp.int32, (c, HEAD_W), 0)
    zblk = jnp.zeros((c, HEAD_W), BF16)
    nt = (((1,), (1,)), ((), ()))

    def pair_rows(x0, x1):
        return jnp.concatenate([jnp.concatenate([x0, zblk], axis=1),
                                jnp.concatenate([zblk, x1], axis=1)], axis=0)

    for hp in range(N_HEADS // 2):
        los = (2 * hp * HEAD_W, (2 * hp + 1) * HEAD_W)
        sts = [st_ref[2 * hp], st_ref[2 * hp + 1]]
        for u in range(nsub):
            ex = exs[u]
            r0 = u * c
            qs, kks, vbs, outs = [], [], [], []
            for n, lo in enumerate(los):
                q = q_ref[r0:r0 + c, lo:lo + HEAD_W].astype(F32)
                vb = i_ref[r0:r0 + c, lo:lo + HEAD_W]
                kk = kk_all[r0:r0 + c, lo:lo + HEAD_W]
                bcum = ex[0:c, lo:lo + HEAD_W]
                q_in = (q * jnp.exp(bcum)).astype(BF16)
                outs.append(lax.dot_general(q_in, sts[n].astype(BF16), nt,
                                            preferred_element_type=F32))
                k_out = (kk * jnp.exp(bcum[c - 1:c, :] - bcum)).astype(BF16)
                e_last = jnp.exp(bcum[c - 1:c, :])
                sts[n] = e_last * sts[n] + jnp.dot(vb.astype(F32).T.astype(BF16), k_out,
                                                   preferred_element_type=F32)
                qs.append(q)
                kks.append(kk)
                vbs.append(vb)
            yield

            a = msk_ref[0] * lax.dot_general(
                jnp.concatenate([qs[0].astype(BF16), qs[1].astype(BF16)], axis=1),
                pair_rows(kks[0].astype(BF16), kks[1].astype(BF16)), nt,
                preferred_element_type=F32)
            for li, lv in enumerate(HG_LEVELS):
                qside = (rows & (lv - 1)) >= (lv // 2)
                qls, kls = [], []
                for n, lo in enumerate(los):
                    if lv == 2:
                        eq, ek = f_all[r0:r0 + c, lo:lo + HEAD_W], None
                    else:
                        eq = ek = jnp.exp(ex[li * c:(li + 1) * c, lo:lo + HEAD_W])
                    qls.append(jnp.where(qside, qs[n] * eq, 0.0).astype(BF16))
                    kls.append(jnp.where(qside, 0.0, kks[n] if ek is None else kks[n] * ek)
                               .astype(BF16))
                al = lax.dot_general(jnp.concatenate(qls, axis=1), pair_rows(kls[0], kls[1]), nt,
                                     preferred_element_type=F32)
                a = a + msk_ref[1 + li] * al
                if li % 2 == 0:
                    yield
            o2 = jnp.dot(a.astype(BF16), pair_rows(vbs[0], vbs[1]),
                         preferred_element_type=F32)

            for n, lo in enumerate(los):
                o = outs[n] + o2[:, n * HEAD_W:(n + 1) * HEAD_W]
                ms = jnp.mean(o * o, axis=-1, keepdims=True)
                on = o * lax.rsqrt(ms + RMS_EPS) * ng_ref[:, lo:lo + HEAD_W]
                y = on * _silu(g_ref[r0:r0 + c, lo:lo + HEAD_W].astype(F32))
                y_ref[r0:r0 + c, lo:lo + HEAD_W] = y.astype(y_ref.dtype)
            yield
        st_ref[2 * hp] = sts[0]
        st_ref[2 * hp + 1] = sts[1]


ATTN_ROWS = 64


def _diff_attn_kernel(lam_ref, q_ref, k_ref, v_ref, g_ref, ng_ref, wo_ref, wpg_ref, wpe_ref,
                      o_ref, wo_o, wpg_o, wpe_o,
                      qq_ref, pa_ref, pb_ref, pt_ref, pu_ref,
                      ma_ref, la_ref, aa_ref, mb_ref, lb_ref, ab_ref,
                      mt_ref, lt_ref, at_ref, mu_ref, lu_ref, au_ref,
                      acc_ref, *, tk, nq, out_scale):
    wo_o[...] = wo_ref[...].astype(BF16)
    wpg_o[...] = wpg_ref[...].astype(BF16)
    wpe_o[...] = wpe_ref[...].astype(BF16)

    lam = lam_ref[0, 0]
    scale = DA_DQK ** -0.5
    tq = 2 * tk
    nrow = 2 * tq
    st_a = (ma_ref, la_ref, aa_ref)
    st_b = (mb_ref, lb_ref, ab_ref)
    st_t = (mt_ref, lt_ref, at_ref)
    st_u = (mu_ref, lu_ref, au_ref)

    hq = tk // 2

    def scores(j, ra=0, rb=nrow, nkeys=tk):
        kb = k_ref[pl.ds(pl.multiple_of(j * tk, tk), nkeys), :]
        return lax.dot_general(qq_ref[ra:rb, :], kb, (((1,), (1,)), ((), ())),
                               preferred_element_type=F32)

    def accumulate(j, p_ref, st, ra=0, rb=nrow, nkeys=tk):
        vb = v_ref[pl.ds(pl.multiple_of(j * tk, tk), nkeys), :]
        acc_ref[ra:rb, :] = (st[2][ra:rb, :] * acc_ref[ra:rb, :]
                             + jnp.dot(p_ref[ra:rb, 0:nkeys], vb, preferred_element_type=F32))

    def softmax(s_all, p_ref, st_in, st_out, diag=None, ra=0, rb=nrow):
        masked = diag is not None
        for r0 in range(ra, rb, ATTN_ROWS):
            rs = slice(r0, r0 + ATTN_ROWS)
            row_lo = (r0 // tk) * hq + r0 % hq
            key_lo = diag * tk if masked else 0
            tiles = []
            for cb in range(s_all.shape[1] // HEAD_W):
                k0 = key_lo + cb * HEAD_W
                if masked and k0 > row_lo + ATTN_ROWS - 1:
                    tiles.append(None)
                    continue
                s = s_all[r0 - ra:r0 - ra + ATTN_ROWS, cb * HEAD_W:(cb + 1) * HEAD_W]
                if masked and k0 + HEAD_W - 1 > row_lo:
                    r = row_lo + lax.broadcasted_iota(jnp.int32, s.shape, 0)
                    cidx = k0 + lax.broadcasted_iota(jnp.int32, s.shape, 1)
                    s = jnp.where(cidx <= r, s, MASK_VALUE)
                tiles.append(s)
            live = [s for s in tiles if s is not None]
            mx = functools.reduce(jnp.maximum, live)
            m_blk = jnp.broadcast_to(jnp.max(mx, axis=-1, keepdims=True), mx.shape)
            m_old = st_in[0][rs, :]
            m_new = jnp.maximum(m_old, m_blk)
            alpha = jnp.exp2(m_old - m_new)
            psum = None
            for cb, s in enumerate(tiles):
                if s is None:
                    p_ref[rs, cb * HEAD_W:(cb + 1) * HEAD_W] = jnp.zeros((ATTN_ROWS, HEAD_W), BF16)
                    continue
                p = jnp.exp2(s - m_new)
                psum = p if psum is None else psum + p
                p_ref[rs, cb * HEAD_W:(cb + 1) * HEAD_W] = p.astype(BF16)
            st_out[0][rs, :] = m_new
            st_out[1][rs, :] = alpha * st_in[1][rs, :] + psum
            st_out[2][rs, :] = alpha

    def step(j, p_in, p_out, st_in, st_out):
        s = scores(j)
        accumulate(jnp.maximum(j - 1, 0), p_in, st_in)
        softmax(s, p_out, st_in, st_out)

    def pair(jj, carry):
        step(2 * jj, pb_ref, pa_ref, st_b, st_a)
        step(2 * jj + 1, pa_ref, pb_ref, st_a, st_b)
        return carry

    def quad(qq, carry):
        pair(2 * qq, carry)
        return pair(2 * qq + 1, carry)

    def q_tile(i, carry):
        for c in range(nrow // tk):
            rows = pl.ds(pl.multiple_of(i * tq + c * hq, hq), hq)
            q = (q_ref[rows, :].astype(F32) * (scale * math.log2(math.e))).astype(BF16)
            lane = lax.broadcasted_iota(jnp.int32, q.shape, 1)
            zero = jnp.zeros_like(q)
            qq_ref[c * tk:c * tk + hq, :] = jnp.where(lane < DA_DQK, q, zero)
            qq_ref[c * tk + hq:(c + 1) * tk, :] = jnp.where(lane >= DA_DQK, q, zero)
        acc_ref[...] = jnp.zeros_like(acc_ref)
        mb_ref[...] = jnp.full_like(mb_ref, MASK_VALUE)
        lb_ref[...] = jnp.zeros_like(lb_ref)
        ab_ref[...] = jnp.ones_like(ab_ref)
        pb_ref[...] = jnp.zeros_like(pb_ref)

        lax.fori_loop(0, i // 2, quad, 0)
        lax.fori_loop(2 * (i // 2), i, pair, 0)

        j0 = 2 * i
        s_q0 = scores(j0, 0, tk, hq)
        s_rest = scores(j0, tk, nrow)
        accumulate(jnp.maximum(j0 - 1, 0), pb_ref, st_b)
        softmax(s_q0, pt_ref, st_b, st_t, diag=0, ra=0, rb=tk)
        softmax(s_rest, pt_ref, st_b, st_t, diag=0, ra=tk, rb=nrow)
        s_q2 = scores(j0 + 1, 2 * tk, 3 * tk, hq)
        s_q3 = scores(j0 + 1, 3 * tk, nrow)
        accumulate(j0, pt_ref, st_t, 0, tk, hq)
        accumulate(j0, pt_ref, st_t, tk, nrow)
        softmax(s_q2, pu_ref, st_t, st_u, diag=1, ra=2 * tk, rb=3 * tk)
        softmax(s_q3, pu_ref, st_t, st_u, diag=1, ra=3 * tk, rb=nrow)
        accumulate(j0 + 1, pu_ref, st_u, 2 * tk, 3 * tk, hq)
        accumulate(j0 + 1, pu_ref, st_u, 3 * tk, nrow)

        for c in range(nrow // tk):
            st = st_t if c < 2 else st_u
            r0 = c * tk
            l = jnp.sum(st[1][r0:r0 + tk, :], axis=-1, keepdims=True)
            acc = acc_ref[r0:r0 + tk, :] / l
            o = acc[0:hq] - lam * acc[hq:tk]
            ms = jnp.mean(o * o, axis=-1, keepdims=True)
            on = o * lax.rsqrt(ms + RMS_EPS) * ng_ref[...] * out_scale
            rows = pl.ds(pl.multiple_of(i * tq + c * hq, hq), hq)
            o_ref[rows, :] = (on * _silu(g_ref[rows, :].astype(F32))).astype(o_ref.dtype)
        return carry

    lax.fori_loop(0, nq, q_tile, 0)


def _diff_attn(h, lam, lam_init, norm_g, wo_all, wpg_all, wpe_all, layer, batch, seq, tk=512):
    t = h.shape[0]
    tq = 2 * tk
    nq = seq // tq
    per = GROUP_W // HEAD_W
    ng2 = norm_g.reshape(1, GROUP_W).astype(F32)
    lam2 = jnp.reshape(lam, (1, 1)).astype(F32)
    kern = functools.partial(_diff_attn_kernel, tk=tk, nq=nq, out_scale=1.0 - lam_init)
    cast_cols = wo_all.shape[2] // (batch * N_HEADS)

    def head_cols(blk):
        return pl.BlockSpec((seq, HEAD_W), lambda b, hh, blk=blk: (b, blk * per + hh))

    def cast_in(a):
        return pl.BlockSpec((None, a.shape[1], cast_cols),
                            lambda b, hh: (layer, 0, b * N_HEADS + hh))

    def cast_out(a):
        return pl.BlockSpec((a.shape[1], cast_cols), lambda b, hh: (0, b * N_HEADS + hh))

    weights = (wo_all, wpg_all, wpe_all)
    return pl.pallas_call(
        kern,
        grid=(batch, N_HEADS),
        in_specs=[pl.BlockSpec(memory_space=pltpu.SMEM),
                  head_cols(C_Q), head_cols(C_K), head_cols(C_V), head_cols(G_C),
                  pl.BlockSpec((1, HEAD_W), lambda b, hh: (0, hh))]
                 + [cast_in(a) for a in weights],
        out_specs=[pl.BlockSpec((seq, HEAD_W), lambda b, hh: (b, hh))]
                  + [cast_out(a) for a in weights],
        out_shape=[jax.ShapeDtypeStruct((t, GROUP_W), BF16)]
                  + [jax.ShapeDtypeStruct(a.shape[1:], BF16) for a in weights],
        scratch_shapes=[pltpu.VMEM((2 * tq, HEAD_W), BF16),
                        *([pltpu.VMEM((2 * tq, tk), BF16)] * 4),
                        *([pltpu.VMEM((2 * tq, HEAD_W), F32)] * 13)],
        compiler_params=pltpu.CompilerParams(
            dimension_semantics=("arbitrary", "arbitrary"),
            vmem_limit_bytes=VMEM_LIMIT),
        name="diff_attn",
    )(lam2, h, h, h, h, ng2, *weights)


TAIL_ROWS = 256


def _tail_kernel(ab_ref, ac_ref, ax_ref, ga_ref, du_ref, dv_ref, gd_ref,
                 bq_ref, bf_ref, bi_ref, gb_ref, yc_ref, x_ref, p_ref,
                 cw_ref, sglng_ref, sglnb_ref, ws_ref, bst_ref,
                 hlb_ref, hng_ref, rng_ref, msk_ref,
                 wo_hbm, wpg_hbm, wpe_hbm, lng_ref, lnb_ref,
                 xo_ref, xb_ref,
                 ya_ref, yb_ref, yd_ref, carry_ref, st_ref,
                 wo_ref, wpg_ref, wpe_ref, wsem, *, seq, alpha):
    n = pl.program_id(0)

    @pl.when((n * TAIL_ROWS) % seq == 0)
    def _():
        carry_ref[...] = jnp.zeros_like(carry_ref)
        st_ref[...] = jnp.zeros_like(st_ref)

    wslot = n % 2
    rslot = 1 - wslot

    def mixers():
        yield from _conv_sgu_body(ab_ref, ac_ref, ax_ref, ga_ref, du_ref, dv_ref, gd_ref,
                                  cw_ref, sglng_ref, sglnb_ref, ws_ref, bst_ref,
                                  ya_ref.at[wslot], yd_ref.at[wslot], carry_ref, ts=TAIL_ROWS)
        yield from _hgrn_body(bq_ref, bf_ref, bi_ref, gb_ref, hlb_ref, hng_ref, rng_ref, msk_ref,
                              yb_ref.at[wslot], st_ref, nsub=TAIL_ROWS // HG_CHUNK)

    def projections():
        d = x_ref.shape[-1]
        blocks = [slice(n0, n0 + GROUP_W) for n0 in range(0, d, GROUP_W)]
        ys = ((yc_ref, 2), (ya_ref.at[rslot], 0), (yd_ref.at[rslot], 3), (yb_ref.at[rslot], 1))
        accs = []
        for cols in blocks:
            acc = alpha * x_ref[:, cols]
            for y, g in ys:
                acc = acc + jnp.dot(y[...], wo_ref[g * GROUP_W:(g + 1) * GROUP_W, cols],
                                    preferred_element_type=F32)
                yield
            accs.append(acc)
        pe = jnp.dot(p_ref[...].astype(BF16), wpe_ref[...], preferred_element_type=F32)
        mu = sum(jnp.sum(a, axis=-1, keepdims=True) for a in accs) * (1.0 / d)
        xcs = [a - mu for a in accs]
        var = sum(jnp.sum(xc * xc, axis=-1, keepdims=True) for xc in xcs) * (1.0 / d)
        rstd = lax.rsqrt(var + LN_EPS)
        xns = [xc * rstd * lng_ref[:, cols] + lnb_ref[:, cols] for xc, cols in zip(xcs, blocks)]
        xnbs = [xn.astype(BF16) for xn in xns]
        yield
        for xn, cols in zip(xns, blocks):
            z = None
            for xnb, rows in zip(xnbs, blocks):
                part = jnp.dot(xnb, wpg_ref[rows, cols], preferred_element_type=F32)
                z = part if z is None else z + part
                yield
            out = xn + pe[:, cols] * _sigmoid(z)
            xo_ref[:, cols] = out
            xb_ref[:, cols] = out.astype(BF16)

    def weight_copies():
        pairs = ((wo_hbm, wo_ref), (wpg_hbm, wpg_ref), (wpe_hbm, wpe_ref))
        return [pltpu.make_async_copy(src, dst, wsem.at[k]) for k, (src, dst) in enumerate(pairs)]

    @pl.when(n == 0)
    def _():
        for cp in weight_copies():
            cp.start()
        for _ in mixers():
            pass

    @pl.when(n == 1)
    def _():
        for cp in weight_copies():
            cp.wait()

    @pl.when(n > 0)
    def _():
        major, minor = mixers(), projections()
        live_major = live_minor = True
        while live_major or live_minor:
            live_major = live_major and next(major, False) is not False
            live_minor = live_minor and next(minor, False) is not False


def _tail_block(h, yc, x, p_all, small, wo_all, wpg_all, wpe_all, lng_all, lnb_all,
                layer, seq, alpha):
    t, d = x.shape
    tm = TAIL_ROWS
    pdim = p_all.shape[1]
    steps = t // tm

    def hblk(blk):
        return pl.BlockSpec((tm, GROUP_W), lambda i, blk=blk: (jnp.minimum(i, steps - 1), blk))

    def rows(w):
        return pl.BlockSpec((tm, w), lambda i: (jnp.maximum(i - 1, 0), 0))

    def full(a):
        nd = a.ndim
        return pl.BlockSpec(a.shape, lambda i, nd=nd: (0,) * nd)

    def resident(a):
        if a.ndim == 2:
            return pl.BlockSpec(a.shape, lambda i: (0, 0), pipeline_mode=pl.Buffered(1))
        return pl.BlockSpec((None,) + a.shape[1:], lambda i: (layer, 0, 0),
                            pipeline_mode=pl.Buffered(1))

    h_blocks = (A_B, A_C, A_X, G_A, D_U, D_V, G_D, B_Q, B_F, B_I, G_B)
    kern = functools.partial(_tail_kernel, seq=seq, alpha=alpha)
    return pl.pallas_call(
        kern,
        grid=(steps + 1,),
        in_specs=[hblk(b) for b in h_blocks]
                 + [rows(GROUP_W), rows(d),
                    pl.BlockSpec((tm, pdim),
                                 lambda i: (layer * steps + jnp.maximum(i - 1, 0), 0))]
                 + [full(a) for a in small]
                 + [pl.BlockSpec(memory_space=pl.ANY)] * 3
                 + [resident(lng_all), resident(lnb_all)],
        out_specs=[rows(d), rows(d)],
        out_shape=[jax.ShapeDtypeStruct((t, d), F32), jax.ShapeDtypeStruct((t, d), BF16)],
        scratch_shapes=[pltpu.VMEM((2, tm, GROUP_W), BF16), pltpu.VMEM((2, tm, GROUP_W), BF16),
                        pltpu.VMEM((2, tm, GROUP_W), BF16),
                        pltpu.VMEM((8, GROUP_W), F32),
                        pltpu.VMEM((N_HEADS, HEAD_W, HEAD_W), F32),
                        pltpu.VMEM(wo_all.shape, BF16), pltpu.VMEM(wpg_all.shape, BF16),
                        pltpu.VMEM(wpe_all.shape, BF16), pltpu.SemaphoreType.DMA((3,))],
        compiler_params=pltpu.CompilerParams(
            dimension_semantics=("arbitrary",), vmem_limit_bytes=VMEM_LIMIT),
        name="mix_out",
    )(*([h] * len(h_blocks)), yc, x, p_all, *small,
      wo_all, wpg_all, wpe_all, lng_all, lnb_all)


def kernel(x, p, w_in, conv_w, hgrn_lb, hgrn_norm_g, diff_lambda, diff_norm_g,
           sg_ln_g, sg_ln_b, sg_w, sg_b, w_out, ln_g, ln_b, w_pe, w_pg):
    batch, seq, d_model = x.shape
    depth = w_in.shape[0]
    t = batch * seq
    alpha = (2 * depth) ** 0.25

    lb_sm = jax.nn.softmax(hgrn_lb.astype(F32), axis=0)
    lower_bounds = jnp.cumsum(lb_sm, axis=0) - lb_sm[0]

    xf = x.reshape(t, d_model)
    xb = xf
    p_all = p.reshape(depth * t, p.shape[-1])
    lng_all = ln_g.reshape(depth, 1, d_model).astype(F32)
    lnb_all = ln_b.reshape(depth, 1, d_model).astype(F32)
    rng, masks = _hgrn_constants()
    for i in range(depth):
        lam_init = 0.8 - 0.6 * math.exp(-0.3 * i)
        dl = diff_lambda[i].astype(F32)
        lam = (jnp.exp(jnp.sum(dl[0] * dl[1])) - jnp.exp(jnp.sum(dl[2] * dl[3])) + lam_init)

        h = _in_proj(xb, w_in, i)
        yc, wo_b, wpg_b, wpe_b = _diff_attn(h, lam, lam_init, diff_norm_g[i],
                                            w_out, w_pg, w_pe, i, batch, seq)
        small = (conv_w[i].astype(F32),
                 sg_ln_g[i].reshape(1, GROUP_W).astype(F32), sg_ln_b[i].reshape(1, GROUP_W).astype(F32),
                 sg_w[i].astype(F32), sg_b[i].T.astype(F32),
                 lower_bounds[i].reshape(1, GROUP_W), hgrn_norm_g[i].reshape(1, GROUP_W).astype(F32),
                 rng, masks)
        xf, xb = _tail_block(h, yc, xf, p_all, small, wo_b, wpg_b, wpe_b,
                             lng_all, lnb_all, i, seq, alpha)
    return xf.reshape(batch, seq, d_model)
```
